```python
import jax, jax.numpy as jnp
from jax import lax
import numpy as np

D_MODEL = 1024
BATCH = 8
SEQ = 2048
DEPTH = 1

M_INNER = D_MODEL
M_HEADDIM = 64
M_HEADS = M_INNER // M_HEADDIM
M_GROUPS = 2
M_STATE = 64
M_CONV = 4
M_CHUNK = 128
M_CONV_DIM = M_INNER + 2 * M_GROUPS * M_STATE
R_HEADS = 8
R_QK_DIM = D_MODEL // 2
R_HEAD_QK = R_QK_DIM // R_HEADS
R_V_DIM = D_MODEL
R_HEAD_V = R_V_DIM // R_HEADS
R_CHUNK = 128
ROPE_BASE = 10000.0
EPS = 1e-6
SPLITS = (M_INNER, M_CONV_DIM, M_HEADS, R_QK_DIM, R_QK_DIM, R_V_DIM, R_V_DIM, D_MODEL, D_MODEL)
D_IN_PROJ = sum(SPLITS)

kernel_name = 'hybrid_ssd_retention_gated_block'


def rms_norm(x, w=None):
    xf = x.astype(jnp.float32)
    y = xf * lax.rsqrt(jnp.mean(xf * xf, axis=-1, keepdims=True) + EPS)
    if w is not None:
        y = y * w.astype(jnp.float32)
    return y


def causal_dwconv(u, w, b):
    K = w.shape[0]
    out = lax.conv_general_dilated(
        u, w.astype(u.dtype)[:, None, :], window_strides=(1,), padding=[(K - 1, 0)],
        dimension_numbers=('NWC', 'WIO', 'NWC'), feature_group_count=u.shape[-1])
    return out + b.astype(u.dtype)


def ssd_chunked(x, dt, A, Bm, Cm):
    Bsz, L, H, P = x.shape
    G, N = Bm.shape[-2], Bm.shape[-1]
    Hg = H // G
    Q = M_CHUNK
    nc = L // Q
    x = x.reshape(Bsz, nc, Q, G, Hg, P)
    dt = dt.reshape(Bsz, nc, Q, G, Hg)
    Bm = Bm.reshape(Bsz, nc, Q, G, N)
    Cm = Cm.reshape(Bsz, nc, Q, G, N)
    a = dt * A.reshape(G, Hg)
    a_cs = jnp.cumsum(a, axis=2)
    seg = a_cs[:, :, :, None] - a_cs[:, :, None, :]
    causal = jnp.tril(jnp.ones((Q, Q), dtype=bool))[:, :, None, None]
    Lmat = jnp.exp(jnp.where(causal, seg, -jnp.inf))
    cb = jnp.einsum('bcign,bcjgn->bcijg', Cm, Bm)
    w_ij = cb[..., None] * Lmat * dt[:, :, None]
    y_diag = jnp.einsum('bcijgh,bcjghp->bcighp', w_ij, x)
    decay_to_end = jnp.exp(a_cs[:, :, -1:] - a_cs)
    x_w = x * (decay_to_end * dt)[..., None]
    states = jnp.einsum('bcjgn,bcjghp->bcghpn', Bm, x_w)
    chunk_decay = jnp.exp(a_cs[:, :, -1])

    def step(S, inp):
        st, dec = inp
        return S * dec[..., None, None] + st, S

    S0 = jnp.zeros_like(states[:, 0])
    _, S_prev = lax.scan(step, S0, (jnp.moveaxis(states, 1, 0), jnp.moveaxis(chunk_decay, 1, 0)))
    S_prev = jnp.moveaxis(S_prev, 0, 1)
    y_off = jnp.einsum('bcign,bcghpn->bcighp', Cm, S_prev) * jnp.exp(a_cs)[..., None]
    return (y_diag + y_off).reshape(Bsz, L, H, P)


def rotary(t, pos):
    half = t.shape[-1] // 2
    inv = ROPE_BASE ** (-jnp.arange(half, dtype=jnp.float32) / half)
    ang = pos[:, None] * inv[None, :]
    cos = jnp.cos(ang)[:, None, :]
    sin = jnp.sin(ang)[:, None, :]
    t1, t2 = t[..., :half], t[..., half:]
    return jnp.concatenate([t1 * cos - t2 * sin, t1 * sin + t2 * cos], axis=-1)


def retention_chunked(q, k, v):
    Bsz, L, H, dk = q.shape
    dv = v.shape[-1]
    Q = R_CHUNK
    nc = L // Q
    log_g = jnp.log1p(-jnp.exp2(-5.0 - jnp.arange(H, dtype=jnp.float32)))
    idx = jnp.arange(Q, dtype=jnp.float32)
    rel = idx[:, None] - idx[None, :]
    dmat = jnp.exp(jnp.where(rel[None] >= 0, rel[None] * log_g[:, None, None], -jnp.inf))
    q = q.reshape(Bsz, nc, Q, H, dk)
    k = k.reshape(Bsz, nc, Q, H, dk)
    v = v.reshape(Bsz, nc, Q, H, dv)
    s = jnp.einsum('bcihd,bcjhd->bchij', q, k) * dmat
    inner = jnp.einsum('bchij,bcjhe->bcihe', s, v)
    k_w = k * jnp.exp((Q - 1 - idx)[:, None] * log_g[None, :])[:, :, None]
    kv = jnp.einsum('bcjhd,bcjhe->bchde', k_w, v)
    chunk_decay = jnp.exp(Q * log_g)

    def step(R, kv_c):
        return R * chunk_decay[:, None, None] + kv_c, R

    R0 = jnp.zeros_like(kv[:, 0])
    _, R_prev = lax.scan(step, R0, jnp.moveaxis(kv, 1, 0))
    R_prev = jnp.moveaxis(R_prev, 0, 1)
    q_w = q * jnp.exp((idx + 1)[:, None] * log_g[None, :])[:, :, None]
    cross = jnp.einsum('bcihd,bchde->bcihe', q_w, R_prev)
    return (inner + cross).reshape(Bsz, L, H, dv)


def hybrid_layer(x, c, w_ada, b_ada, norm_w, w_in, conv_w, conv_b, dt_bias, a_log, d_skip,
                 m_norm_w, w_proj_m, w_proj_r, w_out):
    f32 = jnp.float32
    Bsz, L, _ = x.shape
    mod = c.astype(f32) @ w_ada.astype(f32) + b_ada.astype(f32)
    shift, scale, gate = jnp.split(mod, 3, axis=-1)
    h = rms_norm(x, norm_w) * (1.0 + scale[:, None]) + shift[:, None]
    proj = h @ w_in.astype(f32)
    offs = []
    acc = 0
    for s_ in SPLITS[:-1]:
        acc += s_
        offs.append(acc)
    z_m, xbc, dt_raw, q, k, v, g_r, ga_m, ga_r = jnp.split(proj, offs, axis=-1)

    xbc = jax.nn.silu(causal_dwconv(xbc, conv_w, conv_b))
    xm, Bm, Cm = jnp.split(xbc, [M_INNER, M_INNER + M_GROUPS * M_STATE], axis=-1)
    xm = xm.reshape(Bsz, L, M_HEADS, M_HEADDIM)
    dt = jax.nn.softplus(dt_raw + dt_bias.astype(f32))
    A = -jnp.exp(a_log.astype(f32))
    y = ssd_chunked(xm, dt, A,
                    Bm.reshape(Bsz, L, M_GROUPS, M_STATE), Cm.reshape(Bsz, L, M_GROUPS, M_STATE))
    y = (y + d_skip.astype(f32)[:, None] * xm).reshape(Bsz, L, M_INNER)
    yz = (y * jax.nn.silu(z_m)).reshape(Bsz, L, M_GROUPS, M_INNER // M_GROUPS)
    y_m = rms_norm(yz).reshape(Bsz, L, M_INNER) * m_norm_w.astype(f32)
    u_m = y_m @ w_proj_m.astype(f32)

    pos = jnp.arange(L, dtype=f32)
    q = rotary(q.reshape(Bsz, L, R_HEADS, R_HEAD_QK), pos)
    k = rotary(k.reshape(Bsz, L, R_HEADS, R_HEAD_QK), pos) * (R_HEAD_QK ** -0.5)
    v = v.reshape(Bsz, L, R_HEADS, R_HEAD_V)
    o = rms_norm(retention_chunked(q, k, v))
    y_r = jax.nn.silu(g_r) * o.reshape(Bsz, L, R_V_DIM)
    u_r = y_r @ w_proj_r.astype(f32)

    merged = jax.nn.sigmoid(ga_m) * u_m + jax.nn.sigmoid(ga_r) * u_r
    out = merged @ w_out.astype(f32)
    return (x.astype(f32) + gate[:, None] * out).astype(x.dtype)


def setup_inputs(seed: int = 0) -> dict:
    key = jax.random.key(seed)
    ks = jax.random.split(key, 18)
    D = D_MODEL
    nrm = jax.random.normal
    x = nrm(ks[0], (BATCH, SEQ, D), jnp.float32)
    c = nrm(ks[1], (BATCH, D), jnp.float32)
    w_ada = nrm(ks[2], (DEPTH, D, 3 * D), jnp.float32) * (0.5 * D ** -0.5)
    b_ada = nrm(ks[3], (DEPTH, 3 * D), jnp.float32) * 0.01
    norm_w = 1.0 + 0.02 * nrm(ks[4], (DEPTH, D), jnp.float32)
    w_in = nrm(ks[5], (DEPTH, D, D_IN_PROJ), jnp.float32) * D ** -0.5
    conv_w = nrm(ks[6], (DEPTH, M_CONV, M_CONV_DIM), jnp.float32) * M_CONV ** -0.5
    conv_b = nrm(ks[7], (DEPTH, M_CONV_DIM), jnp.float32) * 0.01
    dt0 = jnp.exp(jax.random.uniform(ks[8], (DEPTH, M_HEADS), jnp.float32,
                                     jnp.log(1e-3), jnp.log(1e-1)))
    dt_bias = dt0 + jnp.log(-jnp.expm1(-dt0))
    a_log = jnp.log(jax.random.uniform(ks[9], (DEPTH, M_HEADS), jnp.float32, 1.0, 16.0))
    d_skip = 1.0 + 0.02 * nrm(ks[10], (DEPTH, M_HEADS), jnp.float32)
    m_norm_w = 1.0 + 0.02 * nrm(ks[11], (DEPTH, M_INNER), jnp.float32)
    w_proj_m = nrm(ks[12], (DEPTH, M_INNER, D), jnp.float32) * M_INNER ** -0.5
    w_proj_r = nrm(ks[13], (DEPTH, R_V_DIM, D), jnp.float32) * R_V_DIM ** -0.5
    w_out = nrm(ks[14], (DEPTH, D, D), jnp.float32) * D ** -0.5
    final_norm_w = 1.0 + 0.02 * nrm(ks[15], (D,), jnp.float32)
    return {'x': x, 'c': c, 'w_ada': w_ada, 'b_ada': b_ada, 'norm_w': norm_w, 'w_in': w_in,
            'conv_w': conv_w, 'conv_b': conv_b, 'dt_bias': dt_bias, 'a_log': a_log,
            'd_skip': d_skip, 'm_norm_w': m_norm_w, 'w_proj_m': w_proj_m,
            'w_proj_r': w_proj_r, 'w_out': w_out, 'final_norm_w': final_norm_w}


def reference(x, c, w_ada, b_ada, norm_w, w_in, conv_w, conv_b, dt_bias, a_log, d_skip,
              m_norm_w, w_proj_m, w_proj_r, w_out, final_norm_w):
    for l in range(DEPTH):
        x = hybrid_layer(x, c, w_ada[l], b_ada[l], norm_w[l], w_in[l], conv_w[l], conv_b[l],
                         dt_bias[l], a_log[l], d_skip[l], m_norm_w[l], w_proj_m[l],
                         w_proj_r[l], w_out[l])
    return rms_norm(x, final_norm_w).astype(x.dtype)
```

```python
import functools

import jax
import jax.numpy as jnp
from jax import lax
from jax.experimental import pallas as pl
from jax.experimental.pallas import tpu as pltpu

f32 = jnp.float32
bf16 = jnp.bfloat16

D_MODEL = 1024
M_HEADDIM = 64
M_HEADS = 16
M_GROUPS = 2
M_STATE = 64
M_CONV = 4
M_INNER = D_MODEL
GROUP_W = M_INNER // M_GROUPS
HEADS_PER_GROUP = M_HEADS // M_GROUPS
CONV_DIM = M_INNER + 2 * M_GROUPS * M_STATE
R_HEADS = 8
R_QK = 512
R_HEAD_QK = 64
R_V = 1024
R_HEAD_V = 128
CHUNK = 128
ROPE_BASE = 10000.0
EPS = 1e-6

LANES = 128
SUBLANES = 8
TOK = 256
N_CHUNKS = TOK // CHUNK
VMEM_LIMIT_BYTES = 56 * 1024 * 1024

OFF_Z = 0
OFF_XBC = OFF_Z + M_INNER
OFF_DT = OFF_XBC + CONV_DIM
OFF_Q = OFF_DT + LANES
OFF_K = OFF_Q + R_QK
OFF_V = OFF_K + R_QK
OFF_GR = OFF_V + R_V
OFF_GAM = OFF_GR + R_V
OFF_GAR = OFF_GAM + D_MODEL
W_COLS = OFF_GAR + D_MODEL

NT_DIMS = (((1,), (1,)), ((), ()))
TN_DIMS = (((0,), (0,)), ((), ()))


def _mod_kernel(c_ref, w_ref, b_ref, o_ref):
    o_ref[...] = jnp.dot(c_ref[...], w_ref[...], preferred_element_type=f32,
                         precision=lax.Precision.HIGHEST) + b_ref[...]


def _sigmoid(v):
    return 1.0 / (1.0 + jnp.exp(-v))


def _silu(v):
    return v * _sigmoid(v)


def _softplus(v):
    return jnp.maximum(v, 0.0) + jnp.log(1.0 + jnp.exp(-jnp.abs(v)))


def _dot(a, b):
    return jnp.dot(a, b, preferred_element_type=f32)


def _split3(v):
    hi = v.astype(bf16)
    r1 = v - hi.astype(f32)
    mid = r1.astype(bf16)
    lo = (r1 - mid.astype(f32)).astype(bf16)
    return hi, mid, lo


def _block_kernel(x_ref, scale_ref, shift_ref, gate_ref, normw_ref, win_ref, convw_ref, convb_ref,
                  dtb_ref, alog_ref, dskip_ref, mnw_ref, wpm_ref, wpr_ref, wout_ref, fnw_ref,
                  cos_ref, sin_ref, dmat_ref, qdec_ref, kdec_ref, rdec_ref, expand_ref, tril_ref,
                  o_ref,
                  xbc_buf, s_state, r_state, y_buf, o_buf):
    @pl.when(pl.program_id(1) == 0)
    def _start_of_sequence():
        xbc_buf[0:SUBLANES, :] = jnp.zeros((SUBLANES, CONV_DIM), f32)
        s_state[...] = jnp.zeros_like(s_state)
        r_state[...] = jnp.zeros_like(r_state)

    x = x_ref[...]
    rs = lax.rsqrt(jnp.mean(x * x, axis=-1, keepdims=True) + EPS)
    g1 = normw_ref[...] * (1.0 + scale_ref[...])
    hb = (x * rs * g1 + shift_ref[...]).astype(bf16)

    def proj(off, n):
        return _dot(hb, win_ref[:, off:off + n])

    xbc_buf[SUBLANES:SUBLANES + TOK, :] = proj(OFF_XBC, CONV_DIM)
    conv = convb_ref[...] + xbc_buf[SUBLANES:SUBLANES + TOK, :] * convw_ref[M_CONV - 1:M_CONV, :]
    for kk in range(M_CONV - 1):
        r0 = SUBLANES - (M_CONV - 1) + kk
        conv = conv + xbc_buf[r0:r0 + TOK, :] * convw_ref[kk:kk + 1, :]
    xbc_buf[0:SUBLANES, :] = xbc_buf[TOK:TOK + SUBLANES, :]
    xa = _silu(conv)
    xm = xa[:, :M_INNER]
    bmat = xa[:, M_INNER:M_INNER + LANES]
    cmat = xa[:, M_INNER + LANES:CONV_DIM]
    xm_b = xm.astype(bf16)

    lane = lax.broadcasted_iota(jnp.int32, (1, LANES), 1)
    a_neg = jnp.where(lane < M_HEADS, -jnp.exp(alog_ref[...]), 0.0)
    dt = _softplus(proj(OFF_DT, LANES) + dtb_ref[...])
    a = dt * a_neg
    tril = tril_ref[...]
    a_hi, a_mid, a_lo = _split3(a)
    a_cs = _dot(tril, a_hi) + _dot(tril, a_mid) + _dot(tril, a_lo)
    a_last = jnp.concatenate(
        [jnp.broadcast_to(a_cs[(c + 1) * CHUNK - 1:(c + 1) * CHUNK, :], (CHUNK, LANES))
         for c in range(N_CHUNKS)], axis=0)
    decay_to_end = jnp.exp(a_last - a_cs)
    exp_acs = jnp.exp(a_cs)

    def expand(v):
        hi = v.astype(bf16)
        lo = (v - hi.astype(f32)).astype(bf16)
        return _dot(hi, expand_ref[...]) + _dot(lo, expand_ref[...])

    w_e = expand(decay_to_end * dt)
    ea_e = expand(exp_acs)
    xw_b = (xm * w_e).astype(bf16)

    ii = lax.broadcasted_iota(jnp.int32, (CHUNK, CHUNK), 0)
    jj = lax.broadcasted_iota(jnp.int32, (CHUNK, CHUNK), 1)
    causal = ii >= jj

    for c in range(N_CHUNKS):
        r0 = c * CHUNK
        acs_c = a_cs[r0:r0 + CHUNK, :]
        acs_t = acs_c.T
        dt_t = dt[r0:r0 + CHUNK, :].T
        cdec_e = expand(jnp.broadcast_to(jnp.exp(a_cs[r0 + CHUNK - 1:r0 + CHUNK, :]),
                                         (SUBLANES, LANES)))[0:1, :]
        for g in range(M_GROUPS):
            b_g = bmat[r0:r0 + CHUNK, g * M_STATE:(g + 1) * M_STATE].astype(bf16)
            c_g = cmat[r0:r0 + CHUNK, g * M_STATE:(g + 1) * M_STATE].astype(bf16)
            cb = lax.dot_general(c_g, b_g, NT_DIMS, preferred_element_type=f32)
            gcols = slice(g * GROUP_W, (g + 1) * GROUP_W)
            s_prev = s_state[g]
            y_off = _dot(c_g, s_prev.astype(bf16)) * ea_e[r0:r0 + CHUNK, gcols]
            s_state[g] = s_prev * cdec_e[:, gcols] + lax.dot_general(
                b_g, xw_b[r0:r0 + CHUNK, gcols], TN_DIMS, preferred_element_type=f32)
            for hh in range(HEADS_PER_GROUP):
                h = g * HEADS_PER_GROUP + hh
                col = jnp.broadcast_to(acs_c[:, h:h + 1], (CHUNK, CHUNK))
                seg = col - acs_t[h:h + 1, :]
                lmat = jnp.exp(jnp.where(causal, seg, -jnp.inf))
                w_h = (cb * lmat * dt_t[h:h + 1, :]).astype(bf16)
                hcols = slice(h * M_HEADDIM, (h + 1) * M_HEADDIM)
                y_h = _dot(w_h, xm_b[r0:r0 + CHUNK, hcols])
                y_buf[r0:r0 + CHUNK, hcols] = y_h + y_off[:, hh * M_HEADDIM:(hh + 1) * M_HEADDIM]

    y = y_buf[...] + dskip_ref[...] * xm
    yz = y * _silu(proj(OFF_Z, M_INNER))
    y_m = []
    for g in range(M_GROUPS):
        yz_g = yz[:, g * GROUP_W:(g + 1) * GROUP_W]
        y_m.append(yz_g * lax.rsqrt(jnp.mean(yz_g * yz_g, axis=-1, keepdims=True) + EPS))
    y_m = (jnp.concatenate(y_m, axis=1) * mnw_ref[...]).astype(bf16)
    u_m = _dot(y_m, wpm_ref[...])

    cos = jnp.concatenate([cos_ref[...]] * (R_QK // LANES), axis=1)
    sin = jnp.concatenate([sin_ref[...]] * (R_QK // LANES), axis=1)
    lane_q = lax.broadcasted_iota(jnp.int32, (TOK, R_QK), 1)
    first_half = (lane_q % R_HEAD_QK) < (R_HEAD_QK // 2)

    def rotary(t):
        swapped = jnp.where(first_half, pltpu.roll(t, R_QK - R_HEAD_QK // 2, 1),
                            pltpu.roll(t, R_HEAD_QK // 2, 1))
        return t * cos + swapped * sin

    q = rotary(proj(OFF_Q, R_QK))
    k = rotary(proj(OFF_K, R_QK)) * (R_HEAD_QK ** -0.5)
    v_b = proj(OFF_V, R_V).astype(bf16)
    qdec = jnp.concatenate([qdec_ref[...]] * N_CHUNKS, axis=0)
    kdec = jnp.concatenate([kdec_ref[...]] * N_CHUNKS, axis=0)
    q_b = q.astype(bf16)
    k_b = k.astype(bf16)
    qw_b = (q * qdec).astype(bf16)
    kw_b = (k * kdec).astype(bf16)

    for c in range(N_CHUNKS):
        r0 = c * CHUNK
        for h in range(R_HEADS):
            qk_cols = slice(h * R_HEAD_QK, (h + 1) * R_HEAD_QK)
            v_cols = slice(h * R_HEAD_V, (h + 1) * R_HEAD_V)
            v_h = v_b[r0:r0 + CHUNK, v_cols]
            s_h = lax.dot_general(q_b[r0:r0 + CHUNK, qk_cols], k_b[r0:r0 + CHUNK, qk_cols], NT_DIMS,
                                  preferred_element_type=f32) * dmat_ref[h]
            r_prev = r_state[h]
            o_h = _dot(s_h.astype(bf16), v_h) + _dot(qw_b[r0:r0 + CHUNK, qk_cols], r_prev.astype(bf16))
            r_state[h] = r_prev * rdec_ref[h] + lax.dot_general(
                kw_b[r0:r0 + CHUNK, qk_cols], v_h, TN_DIMS, preferred_element_type=f32)
            o_buf[r0:r0 + CHUNK, v_cols] = o_h * lax.rsqrt(
                jnp.mean(o_h * o_h, axis=-1, keepdims=True) + EPS)

    y_r = (_silu(proj(OFF_GR, R_V)) * o_buf[...]).astype(bf16)
    u_r = _dot(y_r, wpr_ref[...])

    merged = _sigmoid(proj(OFF_GAM, D_MODEL)) * u_m + _sigmoid(proj(OFF_GAR, D_MODEL)) * u_r
    out = _dot(merged.astype(bf16), wout_ref[...])
    xo = x + gate_ref[...] * out
    o_ref[...] = xo * lax.rsqrt(jnp.mean(xo * xo, axis=-1, keepdims=True) + EPS) * fnw_ref[...]


def _resident(shape):
    zeros = (0,) * len(shape)
    return pl.BlockSpec(shape, lambda b, s: zeros, pipeline_mode=pl.Buffered(1))


def _tables(seq_len):
    half = R_HEAD_QK // 2
    pos = jnp.arange(seq_len, dtype=f32)
    inv = ROPE_BASE ** (-jnp.arange(half, dtype=f32) / half)
    ang = pos[:, None] * inv[None, :]
    cos_t = jnp.tile(jnp.cos(ang), (1, LANES // half))
    sin_t = jnp.tile(jnp.concatenate([-jnp.sin(ang), jnp.sin(ang)], axis=1), (1, LANES // R_HEAD_QK))
    log_g = jnp.log1p(-jnp.exp2(-5.0 - jnp.arange(R_HEADS, dtype=f32)))
    idx = jnp.arange(CHUNK, dtype=f32)
    rel = idx[:, None] - idx[None, :]
    dmat = jnp.exp(jnp.where(rel[None] >= 0, rel[None] * log_g[:, None, None], -jnp.inf))
    qdec = jnp.repeat(jnp.exp((idx + 1)[:, None] * log_g[None, :]), R_HEAD_QK, axis=1)
    kdec = jnp.repeat(jnp.exp((CHUNK - 1 - idx)[:, None] * log_g[None, :]), R_HEAD_QK, axis=1)
    rdec = jnp.broadcast_to(jnp.exp(CHUNK * log_g)[:, None, None], (R_HEADS, 1, R_HEAD_V))
    head_of_col = jnp.arange(M_INNER) // M_HEADDIM
    expand = (jnp.arange(LANES)[:, None] == head_of_col[None, :]).astype(bf16)
    t = jnp.arange(TOK)
    tril = ((t[:, None] >= t[None, :]) & (t[:, None] // CHUNK == t[None, :] // CHUNK)).astype(bf16)
    return cos_t, sin_t, dmat, qdec, kdec, rdec, expand, tril


def _layer(x, c, w_ada, b_ada, norm_w, w_in, conv_w, conv_b, dt_bias, a_log, d_skip,
           m_norm_w, w_proj_m, w_proj_r, w_out, out_norm_w):
    bsz, seq_len, d = x.shape
    assert d == D_MODEL and seq_len % TOK == 0

    mod = pl.pallas_call(
        _mod_kernel,
        grid=(3,),
        in_specs=[pl.BlockSpec((bsz, d), lambda j: (0, 0)),
                  pl.BlockSpec((d, d), lambda j: (0, j)),
                  pl.BlockSpec((1, d), lambda j: (0, j))],
        out_specs=pl.BlockSpec((bsz, d), lambda j: (0, j)),
        out_shape=jax.ShapeDtypeStruct((bsz, 3 * d), f32),
        name="adaln_mod",
    )(c, w_ada, b_ada.reshape(1, 3 * d))
    shift = mod[:, 0 * d:1 * d].reshape(bsz, 1, d)
    scale = mod[:, 1 * d:2 * d].reshape(bsz, 1, d)
    gate = mod[:, 2 * d:3 * d].reshape(bsz, 1, d)

    n_pre = M_INNER + CONV_DIM
    w_cat = jnp.concatenate(
        [w_in[:, :n_pre], jnp.pad(w_in[:, n_pre:n_pre + M_HEADS], ((0, 0), (0, LANES - M_HEADS))),
         w_in[:, n_pre + M_HEADS:]], axis=1).astype(bf16)
    assert w_cat.shape == (d, W_COLS)
    pad_heads = lambda p: jnp.pad(p.reshape(1, M_HEADS), ((0, 0), (0, LANES - M_HEADS)))
    dskip_e = jnp.repeat(d_skip, M_HEADDIM).reshape(1, M_INNER)
    cos_t, sin_t, dmat, qdec, kdec, rdec, expand, tril = _tables(seq_len)

    per_batch = pl.BlockSpec((None, 1, d), lambda b, s: (b, 0, 0))
    tok_block = lambda w: pl.BlockSpec((TOK, w), lambda b, s: (s, 0))
    return pl.pallas_call(
        _block_kernel,
        grid=(bsz, seq_len // TOK),
        in_specs=[
            pl.BlockSpec((None, TOK, d), lambda b, s: (b, s, 0)),
            per_batch, per_batch, per_batch,
            _resident((1, d)),
            _resident((d, W_COLS)),
            _resident((M_CONV, CONV_DIM)), _resident((1, CONV_DIM)),
            _resident((1, LANES)), _resident((1, LANES)),
            _resident((1, M_INNER)), _resident((1, M_INNER)),
            _resident((M_INNER, d)), _resident((R_V, d)), _resident((d, d)),
            _resident((1, d)),
            tok_block(LANES), tok_block(LANES),
            _resident((R_HEADS, CHUNK, CHUNK)),
            _resident((CHUNK, R_QK)), _resident((CHUNK, R_QK)),
            _resident((R_HEADS, 1, R_HEAD_V)),
            _resident((LANES, M_INNER)),
            _resident((TOK, TOK)),
        ],
        out_specs=pl.BlockSpec((None, TOK, d), lambda b, s: (b, s, 0)),
        out_shape=jax.ShapeDtypeStruct((bsz, seq_len, d), x.dtype),
        scratch_shapes=[
            pltpu.VMEM((TOK + 2 * SUBLANES, CONV_DIM), f32),
            pltpu.VMEM((M_GROUPS, M_STATE, GROUP_W), f32),
            pltpu.VMEM((R_HEADS, R_HEAD_QK, R_HEAD_V), f32),
            pltpu.VMEM((TOK, M_INNER), f32),
            pltpu.VMEM((TOK, R_V), f32),
        ],
        compiler_params=pltpu.CompilerParams(
            dimension_semantics=("parallel", "arbitrary"),
            vmem_limit_bytes=VMEM_LIMIT_BYTES),
        name="hybrid_block",
    )(x, scale, shift, gate, norm_w.reshape(1, d), w_cat, conv_w, conv_b.reshape(1, CONV_DIM),
      pad_heads(dt_bias), pad_heads(a_log), dskip_e, m_norm_w.reshape(1, M_INNER),
      w_proj_m.astype(bf16), w_proj_r.astype(bf16), w_out.astype(bf16), out_norm_w.reshape(1, d),
      cos_t, sin_t, dmat, qdec, kdec, rdec, expand, tril)


@jax.jit
def kernel(x, c, w_ada, b_ada, norm_w, w_in, conv_w, conv_b, dt_bias, a_log, d_skip, m_norm_w,
           w_proj_m, w_proj_r, w_out, final_norm_w):
    assert w_ada.shape[0] == 1
    return _layer(x, c, w_ada[0], b_ada[0], norm_w[0], w_in[0], conv_w[0], conv_b[0], dt_bias[0],
                  a_log[0], d_skip[0], m_norm_w[0], w_proj_m[0], w_proj_r[0], w_out[0], final_norm_w)
```

```python
import functools

import jax
import jax.numpy as jnp
from jax import lax
from jax.experimental import pallas as pl
from jax.experimental.pallas import tpu as pltpu

f32 = jnp.float32
bf16 = jnp.bfloat16

D_MODEL = 1024
M_HEADDIM = 64
M_HEADS = 16
M_GROUPS = 2
M_STATE = 64
M_CONV = 4
M_INNER = D_MODEL
GROUP_W = M_INNER // M_GROUPS
HEADS_PER_GROUP = M_HEADS // M_GROUPS
CONV_DIM = M_INNER + 2 * M_GROUPS * M_STATE
R_HEADS = 8
R_QK = 512
R_HEAD_QK = 64
R_V = 1024
R_HEAD_V = 128
CHUNK = 128
ROPE_BASE = 10000.0
EPS = 1e-6

LANES = 128
SUBLANES = 8
MXU_W = 256
TOK = 256
N_CHUNKS = TOK // CHUNK
VMEM_LIMIT_BYTES = 56 * 1024 * 1024

OFF_Z = 0
OFF_XBC = OFF_Z + M_INNER
OFF_DT = OFF_XBC + CONV_DIM
OFF_Q = OFF_DT + LANES
OFF_K = OFF_Q + R_QK
OFF_V = OFF_K + R_QK
OFF_GR = OFF_V + R_V
OFF_GAM = OFF_GR + R_V
OFF_GAR = OFF_GAM + D_MODEL
W_COLS = OFF_GAR + D_MODEL

NT_DIMS = (((1,), (1,)), ((), ()))
TN_DIMS = (((0,), (0,)), ((), ()))


def _mod_kernel(c_ref, w_ref, b_ref, o_ref):
    o_ref[...] = jnp.dot(c_ref[...], w_ref[...], preferred_element_type=f32,
                         precision=lax.Precision.HIGHEST) + b_ref[...]


def _sigmoid(v):
    return 1.0 / (1.0 + jnp.exp(-v))


def _silu(v):
    return v * _sigmoid(v)


def _softplus(v):
    return jnp.maximum(v, 0.0) + jnp.log(1.0 + jnp.exp(-jnp.abs(v)))


def _dot(a, b):
    return jnp.dot(a, b, preferred_element_type=f32)


def _split3(v):
    hi = v.astype(bf16)
    r1 = v - hi.astype(f32)
    mid = r1.astype(bf16)
    lo = (r1 - mid.astype(f32)).astype(bf16)
    return hi, mid, lo


def _block_kernel(x_ref, scale_ref, shift_ref, gate_ref, normw_ref, win_ref, convw_ref, convb_ref,
                  dtb_ref, alog_ref, dskip_ref, mnw_ref, wpm_ref, wpr_ref, wout_ref, fnw_ref,
                  cos_ref, sin_ref, dmat_ref, qdec_ref, kdec_ref, rdec_ref, expand_ref, tril_ref,
                  o_ref,
                  xbc_buf, s_state, r_state, y_buf, o_buf, z_buf, gr_buf, gam_buf, gar_buf, um_buf):
    @pl.when(pl.program_id(1) == 0)
    def _start_of_sequence():
        xbc_buf[0:SUBLANES, :] = jnp.zeros((SUBLANES, CONV_DIM), f32)
        s_state[...] = jnp.zeros_like(s_state)
        r_state[...] = jnp.zeros_like(r_state)

    x = x_ref[...]
    rs = lax.rsqrt(jnp.mean(x * x, axis=-1, keepdims=True) + EPS)
    g1 = normw_ref[...] * (1.0 + scale_ref[...])
    hb = (x * rs * g1 + shift_ref[...]).astype(bf16)

    def proj(off, n):
        return _dot(hb, win_ref[:, off:off + n])

    def proj_tiles(dst, off):
        def tile(c0):
            dst[:, c0:c0 + MXU_W] = proj(off + c0, MXU_W)
        return [functools.partial(tile, c0) for c0 in range(0, D_MODEL, MXU_W)]

    def emit(queue, n=1):
        for _ in range(min(n, len(queue))):
            queue.pop(0)()

    ssd_fill = proj_tiles(z_buf, OFF_Z) + proj_tiles(gr_buf, OFF_GR)
    ret_fill = proj_tiles(gam_buf, OFF_GAM) + proj_tiles(gar_buf, OFF_GAR)

    xbc_buf[SUBLANES:SUBLANES + TOK, :] = proj(OFF_XBC, CONV_DIM)
    dt_raw = proj(OFF_DT, LANES)
    q_raw = proj(OFF_Q, R_QK)
    k_raw = proj(OFF_K, R_QK)
    v_b = proj(OFF_V, R_V).astype(bf16)

    conv = convb_ref[...] + xbc_buf[SUBLANES:SUBLANES + TOK, :] * convw_ref[M_CONV - 1:M_CONV, :]
    for kk in range(M_CONV - 1):
        r0 = SUBLANES - (M_CONV - 1) + kk
        conv = conv + xbc_buf[r0:r0 + TOK, :] * convw_ref[kk:kk + 1, :]
    xbc_buf[0:SUBLANES, :] = xbc_buf[TOK:TOK + SUBLANES, :]
    xa = _silu(conv)
    xm = xa[:, :M_INNER]
    bmat = xa[:, M_INNER:M_INNER + LANES]
    cmat = xa[:, M_INNER + LANES:CONV_DIM]
    xm_b = xm.astype(bf16)

    lane = lax.broadcasted_iota(jnp.int32, (1, LANES), 1)
    a_neg = jnp.where(lane < M_HEADS, -jnp.exp(alog_ref[...]), 0.0)
    dt = _softplus(dt_raw + dtb_ref[...])
    a = dt * a_neg
    tril = tril_ref[...]
    a_hi, a_mid, a_lo = _split3(a)
    a_cs = _dot(tril, a_hi) + _dot(tril, a_mid) + _dot(tril, a_lo)
    a_last = jnp.concatenate(
        [jnp.broadcast_to(a_cs[(c + 1) * CHUNK - 1:(c + 1) * CHUNK, :], (CHUNK, LANES))
         for c in range(N_CHUNKS)], axis=0)
    decay_to_end = jnp.exp(a_last - a_cs)
    exp_acs = jnp.exp(a_cs)

    def expand(v):
        hi = v.astype(bf16)
        lo = (v - hi.astype(f32)).astype(bf16)
        return _dot(hi, expand_ref[...]) + _dot(lo, expand_ref[...])

    w_e = expand(decay_to_end * dt)
    ea_e = expand(exp_acs)
    xw_b = (xm * w_e).astype(bf16)

    ii = lax.broadcasted_iota(jnp.int32, (CHUNK, CHUNK), 0)
    jj = lax.broadcasted_iota(jnp.int32, (CHUNK, CHUNK), 1)
    causal = ii >= jj
    left_head = jj < M_HEADDIM

    for c in range(N_CHUNKS):
        r0 = c * CHUNK
        acs_c = a_cs[r0:r0 + CHUNK, :]
        acs_t = acs_c.T
        dt_t = dt[r0:r0 + CHUNK, :].T
        cdec_e = ea_e[r0 + CHUNK - 1:r0 + CHUNK, :]
        cb, y_off = [], []
        for g in range(M_GROUPS):
            b_g = bmat[r0:r0 + CHUNK, g * M_STATE:(g + 1) * M_STATE].astype(bf16)
            c_g = cmat[r0:r0 + CHUNK, g * M_STATE:(g + 1) * M_STATE].astype(bf16)
            cb.append(lax.dot_general(c_g, b_g, NT_DIMS, preferred_element_type=f32))
            gcols = slice(g * GROUP_W, (g + 1) * GROUP_W)
            s_prev = s_state[g]
            y_off.append(_dot(c_g, s_prev.astype(bf16)) * ea_e[r0:r0 + CHUNK, gcols])
            s_state[g] = s_prev * cdec_e[:, gcols] + lax.dot_general(
                b_g, xw_b[r0:r0 + CHUNK, gcols], TN_DIMS, preferred_element_type=f32)
        for pair in range(M_HEADS // 2):
            g = pair // (HEADS_PER_GROUP // 2)
            pcols = slice(pair * LANES, (pair + 1) * LANES)
            slab = xm_b[r0:r0 + CHUNK, pcols]
            rhs = jnp.concatenate([jnp.where(left_head, slab, jnp.zeros_like(slab)),
                                   jnp.where(left_head, jnp.zeros_like(slab), slab)], axis=0)
            w_pair = []
            for h in (2 * pair, 2 * pair + 1):
                col = jnp.broadcast_to(acs_c[:, h:h + 1], (CHUNK, CHUNK))
                seg = col - acs_t[h:h + 1, :]
                lmat = jnp.exp(jnp.where(causal, seg, -jnp.inf))
                w_pair.append((cb[g] * lmat * dt_t[h:h + 1, :]).astype(bf16))
            y_pair = _dot(jnp.concatenate(w_pair, axis=1), rhs)
            gp = pair % (HEADS_PER_GROUP // 2)
            y_buf[r0:r0 + CHUNK, pcols] = y_pair + y_off[g][:, gp * LANES:(gp + 1) * LANES]
            if pair % 2 == 1:
                emit(ssd_fill)
    emit(ssd_fill, len(ssd_fill))

    y = y_buf[...] + dskip_ref[...] * xm
    yz = y * _silu(z_buf[...])
    y_m = []
    for g in range(M_GROUPS):
        yz_g = yz[:, g * GROUP_W:(g + 1) * GROUP_W]
        y_m.append(yz_g * lax.rsqrt(jnp.mean(yz_g * yz_g, axis=-1, keepdims=True) + EPS))
    y_m = (jnp.concatenate(y_m, axis=1) * mnw_ref[...]).astype(bf16)

    def um_tile(c0):
        um_buf[:, c0:c0 + MXU_W] = _dot(y_m, wpm_ref[:, c0:c0 + MXU_W])
    ret_fill += [functools.partial(um_tile, c0) for c0 in range(0, D_MODEL, MXU_W)]

    cos = jnp.concatenate([cos_ref[...]] * (R_QK // LANES), axis=1)
    sin = jnp.concatenate([sin_ref[...]] * (R_QK // LANES), axis=1)
    lane_q = lax.broadcasted_iota(jnp.int32, (TOK, R_QK), 1)
    first_half = (lane_q % R_HEAD_QK) < (R_HEAD_QK // 2)

    def rotary(t):
        swapped = jnp.where(first_half, pltpu.roll(t, R_QK - R_HEAD_QK // 2, 1),
                            pltpu.roll(t, R_HEAD_QK // 2, 1))
        return t * cos + swapped * sin

    q = rotary(q_raw)
    k = rotary(k_raw) * (R_HEAD_QK ** -0.5)
    qdec = jnp.concatenate([qdec_ref[...]] * N_CHUNKS, axis=0)
    kdec = jnp.concatenate([kdec_ref[...]] * N_CHUNKS, axis=0)
    q_b = q.astype(bf16)
    k_b = k.astype(bf16)
    qw_b = (q * qdec).astype(bf16)
    kw_b = (k * kdec).astype(bf16)

    s_b = {}
    for c in range(N_CHUNKS):
        r0 = c * CHUNK
        for h in range(R_HEADS):
            qk_cols = slice(h * R_HEAD_QK, (h + 1) * R_HEAD_QK)
            s_h = lax.dot_general(q_b[r0:r0 + CHUNK, qk_cols], k_b[r0:r0 + CHUNK, qk_cols], NT_DIMS,
                                  preferred_element_type=f32) * dmat_ref[h]
            s_b[c, h] = s_h.astype(bf16)
            if h % 2 == 1:
                emit(ret_fill)
    for c in range(N_CHUNKS):
        r0 = c * CHUNK
        for h in range(R_HEADS):
            qk_cols = slice(h * R_HEAD_QK, (h + 1) * R_HEAD_QK)
            v_cols = slice(h * R_HEAD_V, (h + 1) * R_HEAD_V)
            v_h = v_b[r0:r0 + CHUNK, v_cols]
            r_prev = r_state[h]
            o_h = _dot(jnp.concatenate([s_b[c, h], qw_b[r0:r0 + CHUNK, qk_cols]], axis=1),
                       jnp.concatenate([v_h, r_prev.astype(bf16)], axis=0))
            r_state[h] = r_prev * rdec_ref[h] + lax.dot_general(
                kw_b[r0:r0 + CHUNK, qk_cols], v_h, TN_DIMS, preferred_element_type=f32)
            o_buf[r0:r0 + CHUNK, v_cols] = o_h * lax.rsqrt(
                jnp.mean(o_h * o_h, axis=-1, keepdims=True) + EPS)
            if h % 4 == 3:
                emit(ret_fill)
    emit(ret_fill, len(ret_fill))

    y_r = (_silu(gr_buf[...]) * o_buf[...]).astype(bf16)
    u_r = _dot(y_r, wpr_ref[...])

    merged = _sigmoid(gam_buf[...]) * um_buf[...] + _sigmoid(gar_buf[...]) * u_r
    out = _dot(merged.astype(bf16), wout_ref[...])
    xo = x + gate_ref[...] * out
    o_ref[...] = xo * lax.rsqrt(jnp.mean(xo * xo, axis=-1, keepdims=True) + EPS) * fnw_ref[...]


def _resident(shape):
    zeros = (0,) * len(shape)
    return pl.BlockSpec(shape, lambda b, s: zeros, pipeline_mode=pl.Buffered(1))


def _tables(seq_len):
    half = R_HEAD_QK // 2
    pos = jnp.arange(seq_len, dtype=f32)
    inv = ROPE_BASE ** (-jnp.arange(half, dtype=f32) / half)
    ang = pos[:, None] * inv[None, :]
    cos_t = jnp.tile(jnp.cos(ang), (1, LANES // half))
    sin_t = jnp.tile(jnp.concatenate([-jnp.sin(ang), jnp.sin(ang)], axis=1), (1, LANES // R_HEAD_QK))
    log_g = jnp.log1p(-jnp.exp2(-5.0 - jnp.arange(R_HEADS, dtype=f32)))
    idx = jnp.arange(CHUNK, dtype=f32)
    rel = idx[:, None] - idx[None, :]
    dmat = jnp.exp(jnp.where(rel[None] >= 0, rel[None] * log_g[:, None, None], -jnp.inf))
    qdec = jnp.repeat(jnp.exp((idx + 1)[:, None] * log_g[None, :]), R_HEAD_QK, axis=1)
    kdec = jnp.repeat(jnp.exp((CHUNK - 1 - idx)[:, None] * log_g[None, :]), R_HEAD_QK, axis=1)
    rdec = jnp.broadcast_to(jnp.exp(CHUNK * log_g)[:, None, None], (R_HEADS, 1, R_HEAD_V))
    head_of_col = jnp.arange(M_INNER) // M_HEADDIM
    expand = (jnp.arange(LANES)[:, None] == head_of_col[None, :]).astype(bf16)
    t = jnp.arange(TOK)
    tril = ((t[:, None] >= t[None, :]) & (t[:, None] // CHUNK == t[None, :] // CHUNK)).astype(bf16)
    return cos_t, sin_t, dmat, qdec, kdec, rdec, expand, tril


def _layer(x, c, w_ada, b_ada, norm_w, w_in, conv_w, conv_b, dt_bias, a_log, d_skip,
           m_norm_w, w_proj_m, w_proj_r, w_out, out_norm_w):
    bsz, seq_len, d = x.shape
    assert d == D_MODEL and seq_len % TOK == 0

    mod = pl.pallas_call(
        _mod_kernel,
        grid=(3,),
        in_specs=[pl.BlockSpec((bsz, d), lambda j: (0, 0)),
                  pl.BlockSpec((d, d), lambda j: (0, j)),
                  pl.BlockSpec((1, d), lambda j: (0, j))],
        out_specs=pl.BlockSpec((bsz, d), lambda j: (0, j)),
        out_shape=jax.ShapeDtypeStruct((bsz, 3 * d), f32),
        name="adaln_mod",
    )(c, w_ada, b_ada.reshape(1, 3 * d))
    shift = mod[:, 0 * d:1 * d].reshape(bsz, 1, d)
    scale = mod[:, 1 * d:2 * d].reshape(bsz, 1, d)
    gate = mod[:, 2 * d:3 * d].reshape(bsz, 1, d)

    n_pre = M_INNER + CONV_DIM
    w_cat = jnp.concatenate(
        [w_in[:, :n_pre], jnp.pad(w_in[:, n_pre:n_pre + M_HEADS], ((0, 0), (0, LANES - M_HEADS))),
         w_in[:, n_pre + M_HEADS:]], axis=1).astype(bf16)
    assert w_cat.shape == (d, W_COLS)
    pad_heads = lambda p: jnp.pad(p.reshape(1, M_HEADS), ((0, 0), (0, LANES - M_HEADS)))
    dskip_e = jnp.repeat(d_skip, M_HEADDIM).reshape(1, M_INNER)
    cos_t, sin_t, dmat, qdec, kdec, rdec, expand, tril = _tables(seq_len)

    per_batch = pl.BlockSpec((None, 1, d), lambda b, s: (b, 0, 0))
    tok_block = lambda w: pl.BlockSpec((TOK, w), lambda b, s: (s, 0))
    return pl.pallas_call(
        _block_kernel,
        grid=(bsz, seq_len // TOK),
        in_specs=[
            pl.BlockSpec((None, TOK, d), lambda b, s: (b, s, 0)),
            per_batch, per_batch, per_batch,
            _resident((1, d)),
            _resident((d, W_COLS)),
            _resident((M_CONV, CONV_DIM)), _resident((1, CONV_DIM)),
            _resident((1, LANES)), _resident((1, LANES)),
            _resident((1, M_INNER)), _resident((1, M_INNER)),
            _resident((M_INNER, d)), _resident((R_V, d)), _resident((d, d)),
            _resident((1, d)),
            tok_block(LANES), tok_block(LANES),
            _resident((R_HEADS, CHUNK, CHUNK)),
            _resident((CHUNK, R_QK)), _resident((CHUNK, R_QK)),
            _resident((R_HEADS, 1, R_HEAD_V)),
            _resident((LANES, M_INNER)),
            _resident((TOK, TOK)),
        ],
        out_specs=pl.BlockSpec((None, TOK, d), lambda b, s: (b, s, 0)),
        out_shape=jax.ShapeDtypeStruct((bsz, seq_len, d), x.dtype),
        scratch_shapes=[
            pltpu.VMEM((TOK + 2 * SUBLANES, CONV_DIM), f32),
            pltpu.VMEM((M_GROUPS, M_STATE, GROUP_W), f32),
            pltpu.VMEM((R_HEADS, R_HEAD_QK, R_HEAD_V), f32),
            pltpu.VMEM((TOK, M_INNER), f32),
            pltpu.VMEM((TOK, R_V), f32),
            pltpu.VMEM((TOK, M_INNER), f32),
            pltpu.VMEM((TOK, R_V), f32),
            pltpu.VMEM((TOK, D_MODEL), f32),
            pltpu.VMEM((TOK, D_MODEL), f32),
            pltpu.VMEM((TOK, D_MODEL), f32),
        ],
        compiler_params=pltpu.CompilerParams(
            dimension_semantics=("parallel", "arbitrary"),
            vmem_limit_bytes=VMEM_LIMIT_BYTES),
        name="hybrid_block",
    )(x, scale, shift, gate, norm_w.reshape(1, d), w_cat, conv_w, conv_b.reshape(1, CONV_DIM),
      pad_heads(dt_bias), pad_heads(a_log), dskip_e, m_norm_w.reshape(1, M_INNER),
      w_proj_m.astype(bf16), w_proj_r.astype(bf16), w_out.astype(bf16), out_norm_w.reshape(1, d),
      cos_t, sin_t, dmat, qdec, kdec, rdec, expand, tril)


@jax.jit
def kernel(x, c, w_ada, b_ada, norm_w, w_in, conv_w, conv_b, dt_bias, a_log, d_skip, m_norm_w,
           w_proj_m, w_proj_r, w_out, final_norm_w):
    assert w_ada.shape[0] == 1
    return _layer(x, c, w_ada[0], b_ada[0], norm_w[0], w_in[0], conv_w[0], conv_b[0], dt_bias[0],
                  a_log[0], d_skip[0], m_norm_w[0], w_proj_m[0], w_proj_r[0], w_out[0], final_norm_w)
```

```python
import functools

import jax
import jax.numpy as jnp
import numpy as np
from jax import lax
from jax.experimental import pallas as pl
from jax.experimental.pallas import tpu as pltpu

f32 = jnp.float32
bf16 = jnp.bfloat16

D_MODEL = 1024
M_HEADDIM = 64
M_HEADS = 16
M_GROUPS = 2
M_STATE = 64
M_CONV = 4
M_INNER = D_MODEL
GROUP_W = M_INNER // M_GROUPS
HEADS_PER_GROUP = M_HEADS // M_GROUPS
CONV_DIM = M_INNER + 2 * M_GROUPS * M_STATE
R_HEADS = 8
R_QK = 512
R_HEAD_QK = 64
R_V = 1024
R_HEAD_V = 128
CHUNK = 128
ROPE_BASE = 10000.0
EPS = 1e-6

LANES = 128
SUBLANES = 8
MXU_W = 256
TOK = 256
RELAYOUT_ROWS = 128
N_CHUNKS = TOK // CHUNK
VMEM_LIMIT_BYTES = 56 * 1024 * 1024

OFF_Z = 0
OFF_XBC = OFF_Z + M_INNER
OFF_DT = OFF_XBC + CONV_DIM
OFF_Q = OFF_DT + LANES
OFF_K = OFF_Q + R_QK
OFF_V = OFF_K + R_QK
OFF_GR = OFF_V + R_V
OFF_GAM = OFF_GR + R_V
OFF_GAR = OFF_GAM + D_MODEL
W_COLS = OFF_GAR + D_MODEL

NT_DIMS = (((1,), (1,)), ((), ()))
TN_DIMS = (((0,), (0,)), ((), ()))


def _mod_kernel(c_ref, w_ref, b_ref, o_ref):
    o_ref[...] = jnp.dot(c_ref[...], w_ref[...], preferred_element_type=f32,
                         precision=lax.Precision.HIGHEST) + b_ref[...]


def _sigmoid(v):
    return 1.0 / (1.0 + jnp.exp(-v))


def _silu(v):
    return v * _sigmoid(v)


def _softplus(v):
    return jnp.maximum(v, 0.0) + jnp.log(1.0 + jnp.exp(-jnp.abs(v)))


def _dot(a, b):
    return jnp.dot(a, b, preferred_element_type=f32)


def _split3(v):
    hi = v.astype(bf16)
    r1 = v - hi.astype(f32)
    mid = r1.astype(bf16)
    lo = (r1 - mid.astype(f32)).astype(bf16)
    return hi, mid, lo


def _block_kernel(x_ref, mod_ref, normw_ref, win_ref, convw_ref, convb_ref,
                  dtb_ref, alog_ref, dskip_ref, mnw_ref, wpm_ref, wpr_ref, wout_ref, fnw_ref,
                  cos_ref, sin_ref, dmat_ref, qdec_ref, kdec_ref, rdec_ref, expand_ref, tril_ref,
                  o_ref,
                  xbc_buf, s_state, r_state, y_buf, o_buf, z_buf, gr_buf, gam_buf, gar_buf, um_buf,
                  q_buf, k_buf, v_buf):
    @pl.when(pl.program_id(1) == 0)
    def _start_of_sequence():
        xbc_buf[0:SUBLANES, :] = jnp.zeros((SUBLANES, CONV_DIM), f32)
        s_state[...] = jnp.zeros_like(s_state)
        r_state[...] = jnp.zeros_like(r_state)

    x = x_ref[...]
    rs = lax.rsqrt(jnp.mean(x * x, axis=-1, keepdims=True) + EPS)
    mod = mod_ref[pl.ds(pl.program_id(0), 1), :]
    shift, scale, gate = (mod[:, i * D_MODEL:(i + 1) * D_MODEL] for i in range(3))
    g1 = normw_ref[...] * (1.0 + scale)
    hb = (x * rs * g1 + shift).astype(bf16)

    def proj(off, n):
        return _dot(hb, win_ref[:, off:off + n])

    def proj_tiles(dst, off):
        def tile(c0):
            dst[:, c0:c0 + MXU_W] = proj(off + c0, MXU_W).astype(dst.dtype)
        return [functools.partial(tile, c0) for c0 in range(0, dst.shape[1], MXU_W)]

    def emit(queue, n=1):
        for _ in range(min(n, len(queue))):
            queue.pop(0)()

    ssd_fill = (proj_tiles(v_buf, OFF_V) + proj_tiles(q_buf, OFF_Q) + proj_tiles(k_buf, OFF_K)
                + proj_tiles(z_buf, OFF_Z) + proj_tiles(gr_buf, OFF_GR))
    ret_fill = proj_tiles(gam_buf, OFF_GAM) + proj_tiles(gar_buf, OFF_GAR)

    xbc_buf[SUBLANES:SUBLANES + TOK, :] = proj(OFF_XBC, CONV_DIM)
    dt_raw = proj(OFF_DT, LANES)
    emit(ssd_fill, 6)

    conv = convb_ref[...] + xbc_buf[SUBLANES:SUBLANES + TOK, :] * convw_ref[M_CONV - 1:M_CONV, :]
    for kk in range(M_CONV - 1):
        r0 = SUBLANES - (M_CONV - 1) + kk
        conv = conv + xbc_buf[r0:r0 + TOK, :] * convw_ref[kk:kk + 1, :]
    xbc_buf[0:SUBLANES, :] = xbc_buf[TOK:TOK + SUBLANES, :]
    xa = _silu(conv)
    xm = xa[:, :M_INNER]
    bmat = xa[:, M_INNER:M_INNER + LANES]
    cmat = xa[:, M_INNER + LANES:CONV_DIM]
    xm_b = xm.astype(bf16)

    lane = lax.broadcasted_iota(jnp.int32, (1, LANES), 1)
    a_neg = jnp.where(lane < M_HEADS, -jnp.exp(alog_ref[...]), 0.0)
    dt = _softplus(dt_raw + dtb_ref[...])
    a = dt * a_neg
    tril = tril_ref[...]
    a_parts = _dot(tril, jnp.concatenate(_split3(a), axis=1))
    a_cs = a_parts[:, :LANES] + a_parts[:, LANES:2 * LANES] + a_parts[:, 2 * LANES:]
    emit(ssd_fill, 2)
    a_last = jnp.concatenate(
        [jnp.broadcast_to(a_cs[(c + 1) * CHUNK - 1:(c + 1) * CHUNK, :], (CHUNK, LANES))
         for c in range(N_CHUNKS)], axis=0)
    decay_to_end = jnp.exp(a_last - a_cs)
    exp_acs = jnp.exp(a_cs)

    def expand(v):
        hi = v.astype(bf16)
        lo = (v - hi.astype(f32)).astype(bf16)
        return _dot(hi, expand_ref[...]) + _dot(lo, expand_ref[...])

    w_e = expand(decay_to_end * dt)
    ea_e = expand(exp_acs)
    emit(ssd_fill, 1)
    xw_b = (xm * w_e).astype(bf16)

    ii = lax.broadcasted_iota(jnp.int32, (CHUNK, CHUNK), 0)
    jj = lax.broadcasted_iota(jnp.int32, (CHUNK, CHUNK), 1)
    causal = ii >= jj
    left_head = jj < M_HEADDIM

    for c in range(N_CHUNKS):
        r0 = c * CHUNK
        acs_c = a_cs[r0:r0 + CHUNK, :]
        acs_t = acs_c.T
        dt_t = dt[r0:r0 + CHUNK, :].T
        cdec_e = ea_e[r0 + CHUNK - 1:r0 + CHUNK, :]
        cb, y_off = [], []
        for g in range(M_GROUPS):
            b_g = bmat[r0:r0 + CHUNK, g * M_STATE:(g + 1) * M_STATE].astype(bf16)
            c_g = cmat[r0:r0 + CHUNK, g * M_STATE:(g + 1) * M_STATE].astype(bf16)
            cb.append(lax.dot_general(c_g, b_g, NT_DIMS, preferred_element_type=f32))
            gcols = slice(g * GROUP_W, (g + 1) * GROUP_W)
            s_prev = s_state[g]
            y_off.append(_dot(c_g, s_prev.astype(bf16)) * ea_e[r0:r0 + CHUNK, gcols])
            s_state[g] = s_prev * cdec_e[:, gcols] + lax.dot_general(
                b_g, xw_b[r0:r0 + CHUNK, gcols], TN_DIMS, preferred_element_type=f32)
        for pair in range(M_HEADS // 2):
            g = pair // (HEADS_PER_GROUP // 2)
            pcols = slice(pair * LANES, (pair + 1) * LANES)
            slab = xm_b[r0:r0 + CHUNK, pcols]
            rhs = jnp.concatenate([jnp.where(left_head, slab, jnp.zeros_like(slab)),
                                   jnp.where(left_head, jnp.zeros_like(slab), slab)], axis=0)
            w_pair = []
            for h in (2 * pair, 2 * pair + 1):
                col = jnp.broadcast_to(acs_c[:, h:h + 1], (CHUNK, CHUNK))
                seg = col - acs_t[h:h + 1, :]
                lmat = jnp.exp(jnp.where(causal, seg, -jnp.inf))
                w_pair.append((cb[g] * lmat * dt_t[h:h + 1, :]).astype(bf16))
            y_pair = _dot(jnp.concatenate(w_pair, axis=1), rhs)
            gp = pair % (HEADS_PER_GROUP // 2)
            y_buf[r0:r0 + CHUNK, pcols] = y_pair + y_off[g][:, gp * LANES:(gp + 1) * LANES]
            emit(ssd_fill)
    emit(ssd_fill, len(ssd_fill))

    y = y_buf[...] + dskip_ref[...] * xm
    yz = y * _silu(z_buf[...])
    y_m = []
    for g in range(M_GROUPS):
        yz_g = yz[:, g * GROUP_W:(g + 1) * GROUP_W]
        y_m.append(yz_g * lax.rsqrt(jnp.mean(yz_g * yz_g, axis=-1, keepdims=True) + EPS))
    y_m = (jnp.concatenate(y_m, axis=1) * mnw_ref[...]).astype(bf16)

    def um_tile(c0):
        um_buf[:, c0:c0 + MXU_W] = _dot(y_m, wpm_ref[:, c0:c0 + MXU_W])
    ret_fill += [functools.partial(um_tile, c0) for c0 in range(0, D_MODEL, MXU_W)]

    cos = jnp.concatenate([cos_ref[...]] * (R_QK // LANES), axis=1)
    sin = jnp.concatenate([sin_ref[...]] * (R_QK // LANES), axis=1)
    lane_q = lax.broadcasted_iota(jnp.int32, (TOK, R_QK), 1)
    first_half = (lane_q % R_HEAD_QK) < (R_HEAD_QK // 2)

    def rotary(t):
        swapped = jnp.where(first_half, pltpu.roll(t, R_QK - R_HEAD_QK // 2, 1),
                            pltpu.roll(t, R_HEAD_QK // 2, 1))
        return t * cos + swapped * sin

    q = rotary(q_buf[...])
    k = rotary(k_buf[...]) * (R_HEAD_QK ** -0.5)
    qdec = jnp.concatenate([qdec_ref[...]] * N_CHUNKS, axis=0)
    kdec = jnp.concatenate([kdec_ref[...]] * N_CHUNKS, axis=0)
    q_b = q.astype(bf16)
    k_b = k.astype(bf16)
    qw_b = (q * qdec).astype(bf16)
    kw_b = (k * kdec).astype(bf16)

    s_b = {}
    for c in range(N_CHUNKS):
        r0 = c * CHUNK
        for h in range(R_HEADS):
            qk_cols = slice(h * R_HEAD_QK, (h + 1) * R_HEAD_QK)
            s_h = lax.dot_general(q_b[r0:r0 + CHUNK, qk_cols], k_b[r0:r0 + CHUNK, qk_cols], NT_DIMS,
                                  preferred_element_type=f32) * dmat_ref[h]
            s_b[c, h] = s_h.astype(bf16)
            if h % 2 == 1:
                emit(ret_fill)
    for c in range(N_CHUNKS):
        r0 = c * CHUNK
        for h in range(R_HEADS):
            qk_cols = slice(h * R_HEAD_QK, (h + 1) * R_HEAD_QK)
            v_cols = slice(h * R_HEAD_V, (h + 1) * R_HEAD_V)
            v_h = v_buf[r0:r0 + CHUNK, v_cols]
            r_prev = r_state[h]
            o_h = _dot(jnp.concatenate([s_b[c, h], qw_b[r0:r0 + CHUNK, qk_cols]], axis=1),
                       jnp.concatenate([v_h, r_prev.astype(bf16)], axis=0))
            r_state[h] = r_prev * rdec_ref[h] + lax.dot_general(
                kw_b[r0:r0 + CHUNK, qk_cols], v_h, TN_DIMS, preferred_element_type=f32)
            o_buf[r0:r0 + CHUNK, v_cols] = o_h * lax.rsqrt(
                jnp.mean(o_h * o_h, axis=-1, keepdims=True) + EPS)
            if h % 4 == 3:
                emit(ret_fill)
    emit(ret_fill, len(ret_fill))

    y_r = (_silu(gr_buf[...]) * o_buf[...]).astype(bf16)
    u_r = _dot(y_r, wpr_ref[...])

    merged = _sigmoid(gam_buf[...]) * um_buf[...] + _sigmoid(gar_buf[...]) * u_r
    out = _dot(merged.astype(bf16), wout_ref[...])
    xo = x + gate * out
    o_ref[...] = xo * lax.rsqrt(jnp.mean(xo * xo, axis=-1, keepdims=True) + EPS) * fnw_ref[...]


def _resident(shape):
    zeros = (0,) * len(shape)
    return pl.BlockSpec(shape, lambda b, s: zeros, pipeline_mode=pl.Buffered(1))


def _tables(seq_len):
    half = R_HEAD_QK // 2
    pos = np.arange(seq_len, dtype=np.float64)
    inv = ROPE_BASE ** (-np.arange(half, dtype=np.float64) / half)
    ang = pos[:, None] * inv[None, :]
    cos_t = np.tile(np.cos(ang), (1, LANES // half))
    sin_t = np.tile(np.concatenate([-np.sin(ang), np.sin(ang)], axis=1), (1, LANES // R_HEAD_QK))
    log_g = np.log1p(-np.exp2(-5.0 - np.arange(R_HEADS, dtype=np.float64)))
    idx = np.arange(CHUNK, dtype=np.float64)
    rel = idx[:, None] - idx[None, :]
    dmat = np.where(rel[None] >= 0, np.exp(np.minimum(rel[None], CHUNK) * log_g[:, None, None]), 0.0)
    qdec = np.repeat(np.exp((idx + 1)[:, None] * log_g[None, :]), R_HEAD_QK, axis=1)
    kdec = np.repeat(np.exp((CHUNK - 1 - idx)[:, None] * log_g[None, :]), R_HEAD_QK, axis=1)
    rdec = np.broadcast_to(np.exp(CHUNK * log_g)[:, None, None], (R_HEADS, 1, R_HEAD_V))
    head_of_col = np.arange(M_INNER) // M_HEADDIM
    expand = np.arange(LANES)[:, None] == head_of_col[None, :]
    t = np.arange(TOK)
    tril = (t[:, None] >= t[None, :]) & (t[:, None] // CHUNK == t[None, :] // CHUNK)
    as_f32 = lambda a: jnp.asarray(np.ascontiguousarray(a, dtype=np.float32))
    as_bf16 = lambda a: jnp.asarray(np.ascontiguousarray(a, dtype=np.float32), dtype=bf16)
    return (as_f32(cos_t), as_f32(sin_t), as_f32(dmat), as_f32(qdec), as_f32(kdec), as_f32(rdec),
            as_bf16(expand), as_bf16(tril))


def _relayout_kernel(w_ref, o_ref):
    n_pre = M_INNER + CONV_DIM
    o_ref[:, :n_pre] = w_ref[:, :n_pre].astype(bf16)
    lane = lax.broadcasted_iota(jnp.int32, (o_ref.shape[0], LANES), 1)
    o_ref[:, n_pre:n_pre + LANES] = jnp.where(lane < M_HEADS, w_ref[:, n_pre:n_pre + LANES], 0.0).astype(bf16)
    o_ref[:, n_pre + LANES:] = w_ref[:, n_pre + M_HEADS:].astype(bf16)


def _layer(x, c, w_ada, b_ada, norm_w, w_in, conv_w, conv_b, dt_bias, a_log, d_skip,
           m_norm_w, w_proj_m, w_proj_r, w_out, out_norm_w):
    bsz, seq_len, d = x.shape
    assert d == D_MODEL and seq_len % TOK == 0

    mod = pl.pallas_call(
        _mod_kernel,
        grid=(3,),
        in_specs=[pl.BlockSpec((bsz, d), lambda j: (0, 0)),
                  pl.BlockSpec((d, d), lambda j: (0, j)),
                  pl.BlockSpec((1, d), lambda j: (0, j))],
        out_specs=pl.BlockSpec((bsz, d), lambda j: (0, j)),
        out_shape=jax.ShapeDtypeStruct((bsz, 3 * d), f32),
        name="adaln_mod",
    )(c, w_ada, b_ada.reshape(1, 3 * d))

    assert w_in.shape == (d, W_COLS - LANES + M_HEADS)
    w_cat = pl.pallas_call(
        _relayout_kernel,
        grid=(d // RELAYOUT_ROWS,),
        in_specs=[pl.BlockSpec((RELAYOUT_ROWS, w_in.shape[1]), lambda i: (i, 0))],
        out_specs=pl.BlockSpec((RELAYOUT_ROWS, W_COLS), lambda i: (i, 0)),
        out_shape=jax.ShapeDtypeStruct((d, W_COLS), bf16),
        name="w_in_relayout",
    )(w_in)
    pad_heads = lambda p: jnp.pad(p.reshape(1, M_HEADS), ((0, 0), (0, LANES - M_HEADS)))
    dskip_e = jnp.repeat(d_skip, M_HEADDIM).reshape(1, M_INNER)
    cos_t, sin_t, dmat, qdec, kdec, rdec, expand, tril = _tables(seq_len)

    tok_block = lambda w: pl.BlockSpec((TOK, w), lambda b, s: (s, 0))
    return pl.pallas_call(
        _block_kernel,
        grid=(bsz, seq_len // TOK),
        in_specs=[
            pl.BlockSpec((None, TOK, d), lambda b, s: (b, s, 0)),
            _resident((bsz, 3 * d)),
            _resident((1, d)),
            _resident((d, W_COLS)),
            _resident((M_CONV, CONV_DIM)), _resident((1, CONV_DIM)),
            _resident((1, LANES)), _resident((1, LANES)),
            _resident((1, M_INNER)), _resident((1, M_INNER)),
            _resident((M_INNER, d)), _resident((R_V, d)), _resident((d, d)),
            _resident((1, d)),
            tok_block(LANES), tok_block(LANES),
            _resident((R_HEADS, CHUNK, CHUNK)),
            _resident((CHUNK, R_QK)), _resident((CHUNK, R_QK)),
            _resident((R_HEADS, 1, R_HEAD_V)),
            _resident((LANES, M_INNER)),
            _resident((TOK, TOK)),
        ],
        out_specs=pl.BlockSpec((None, TOK, d), lambda b, s: (b, s, 0)),
        out_shape=jax.ShapeDtypeStruct((bsz, seq_len, d), x.dtype),
        scratch_shapes=[
            pltpu.VMEM((TOK + 2 * SUBLANES, CONV_DIM), f32),
            pltpu.VMEM((M_GROUPS, M_STATE, GROUP_W), f32),
            pltpu.VMEM((R_HEADS, R_HEAD_QK, R_HEAD_V), f32),
            pltpu.VMEM((TOK, M_INNER), f32),
            pltpu.VMEM((TOK, R_V), f32),
            pltpu.VMEM((TOK, M_INNER), f32),
            pltpu.VMEM((TOK, R_V), f32),
            pltpu.VMEM((TOK, D_MODEL), f32),
            pltpu.VMEM((TOK, D_MODEL), f32),
            pltpu.VMEM((TOK, D_MODEL), f32),
            pltpu.VMEM((TOK, R_QK), f32),
            pltpu.VMEM((TOK, R_QK), f32),
            pltpu.VMEM((TOK, R_V), bf16),
        ],
        compiler_params=pltpu.CompilerParams(
            dimension_semantics=("parallel", "arbitrary"),
            vmem_limit_bytes=VMEM_LIMIT_BYTES),
        name="hybrid_block",
    )(x, mod, norm_w.reshape(1, d), w_cat, conv_w, conv_b.reshape(1, CONV_DIM),
      pad_heads(dt_bias), pad_heads(a_log), dskip_e, m_norm_w.reshape(1, M_INNER),
      w_proj_m.astype(bf16), w_proj_r.astype(bf16), w_out.astype(bf16), out_norm_w.reshape(1, d),
      cos_t, sin_t, dmat, qdec, kdec, rdec, expand, tril)


@jax.jit
def kernel(x, c, w_ada, b_ada, norm_w, w_in, conv_w, conv_b, dt_bias, a_log, d_skip, m_norm_w,
           w_proj_m, w_proj_r, w_out, final_norm_w):
    assert w_ada.shape[0] == 1
    return _layer(x, c, w_ada[0], b_ada[0], norm_w[0], w_in[0], conv_w[0], conv_b[0], dt_bias[0],
                  a_log[0], d_skip[0], m_norm_w[0], w_proj_m[0], w_proj_r[0], w_out[0], final_norm_w)
```

```python
import functools

import jax
import jax.numpy as jnp
import numpy as np
from jax import lax
from jax.experimental import pallas as pl
from jax.experimental.pallas import tpu as pltpu

f32 = jnp.float32
bf16 = jnp.bfloat16

D_MODEL = 1024
M_HEADDIM = 64
M_HEADS = 16
M_GROUPS = 2
M_STATE = 64
M_CONV = 4
M_INNER = D_MODEL
GROUP_W = M_INNER // M_GROUPS
HEADS_PER_GROUP = M_HEADS // M_GROUPS
CONV_DIM = M_INNER + 2 * M_GROUPS * M_STATE
R_HEADS = 8
R_QK = 512
R_HEAD_QK = 64
R_V = 1024
R_HEAD_V = 128
CHUNK = 128
ROPE_BASE = 10000.0
EPS = 1e-6

LANES = 128
SUBLANES = 8
MXU_W = 256
TOK = 256
RELAYOUT_ROWS = 128
N_CHUNKS = TOK // CHUNK
VMEM_LIMIT_BYTES = 56 * 1024 * 1024

OFF_Z = 0
OFF_XBC = OFF_Z + M_INNER
OFF_DT = OFF_XBC + CONV_DIM
OFF_Q = OFF_DT + LANES
OFF_K = OFF_Q + R_QK
OFF_V = OFF_K + R_QK
OFF_GR = OFF_V + R_V
OFF_GAM = OFF_GR + R_V
OFF_GAR = OFF_GAM + D_MODEL
W_COLS = OFF_GAR + D_MODEL

NT_DIMS = (((1,), (1,)), ((), ()))


def _mod_kernel(c_ref, w_ref, b_ref, o_ref):
    o_ref[...] = jnp.dot(c_ref[...], w_ref[...], preferred_element_type=f32,
                         precision=lax.Precision.HIGHEST) + b_ref[...]


def _sigmoid(v):
    return 1.0 / (1.0 + jnp.exp(-v))


def _silu(v):
    return v * _sigmoid(v)


def _softplus(v):
    return jnp.maximum(v, 0.0) + jnp.log(1.0 + jnp.exp(-jnp.abs(v)))


def _dot(a, b):
    return jnp.dot(a, b, preferred_element_type=f32)


def _split3(v):
    hi = v.astype(bf16)
    r1 = v - hi.astype(f32)
    mid = r1.astype(bf16)
    lo = (r1 - mid.astype(f32)).astype(bf16)
    return hi, mid, lo


def _block_kernel(x_ref, mod_ref, normw_ref, win_ref, convw_ref, convb_ref,
                  dtb_ref, alog_ref, dskip_ref, mnw_ref, wpm_ref, wpr_ref, wout_ref, fnw_ref,
                  cos_ref, sin_ref, dmat_ref, qdec_ref, kdec_ref, rdec_ref, tril_ref,
                  o_ref,
                  xbc_buf, s_state, r_state, y_buf, o_buf, z_buf, gr_buf, gam_buf, gar_buf, um_buf,
                  q_buf, k_buf, v_buf):
    @pl.when(pl.program_id(1) == 0)
    def _start_of_sequence():
        xbc_buf[0:SUBLANES, :] = jnp.zeros((SUBLANES, CONV_DIM), f32)
        s_state[...] = jnp.zeros_like(s_state)
        r_state[...] = jnp.zeros_like(r_state)

    x = x_ref[...]
    rs = lax.rsqrt(jnp.mean(x * x, axis=-1, keepdims=True) + EPS)
    mod = mod_ref[pl.ds(pl.program_id(0), 1), :]
    shift, scale, gate = (mod[:, i * D_MODEL:(i + 1) * D_MODEL] for i in range(3))
    g1 = normw_ref[...] * (1.0 + scale)
    hb = (x * rs * g1 + shift).astype(bf16)

    def proj(off, n):
        return _dot(hb, win_ref[:, off:off + n])

    def proj_tiles(dst, off):
        def tile(c0):
            dst[:, c0:c0 + MXU_W] = proj(off + c0, MXU_W).astype(dst.dtype)
        return [functools.partial(tile, c0) for c0 in range(0, dst.shape[1], MXU_W)]

    def emit(queue, n=1):
        for _ in range(min(n, len(queue))):
            queue.pop(0)()

    ssd_fill = (proj_tiles(v_buf, OFF_V) + proj_tiles(q_buf, OFF_Q) + proj_tiles(k_buf, OFF_K)
                + proj_tiles(z_buf, OFF_Z) + proj_tiles(gr_buf, OFF_GR))
    ret_fill = proj_tiles(gam_buf, OFF_GAM) + proj_tiles(gar_buf, OFF_GAR)

    xbc_buf[SUBLANES:SUBLANES + TOK, :] = proj(OFF_XBC, CONV_DIM)
    dt_raw = proj(OFF_DT, LANES)
    emit(ssd_fill, 6)

    conv = convb_ref[...] + xbc_buf[SUBLANES:SUBLANES + TOK, :] * convw_ref[M_CONV - 1:M_CONV, :]
    for kk in range(M_CONV - 1):
        r0 = SUBLANES - (M_CONV - 1) + kk
        conv = conv + xbc_buf[r0:r0 + TOK, :] * convw_ref[kk:kk + 1, :]
    xbc_buf[0:SUBLANES, :] = xbc_buf[TOK:TOK + SUBLANES, :]
    xa = _silu(conv)
    xm = xa[:, :M_INNER]
    bmat = xa[:, M_INNER:M_INNER + LANES]
    cmat = xa[:, M_INNER + LANES:CONV_DIM]
    xm_b = xm.astype(bf16)

    lane = lax.broadcasted_iota(jnp.int32, (1, LANES), 1)
    a_neg = jnp.where(lane < M_HEADS, -jnp.exp(alog_ref[...]), 0.0)
    dt = _softplus(dt_raw + dtb_ref[...])
    a = dt * a_neg
    tril = tril_ref[...]
    a_parts = _dot(tril, jnp.concatenate(_split3(a), axis=1))
    a_cs = a_parts[:, :LANES] + a_parts[:, LANES:2 * LANES] + a_parts[:, 2 * LANES:]
    emit(ssd_fill, 3)

    ii = lax.broadcasted_iota(jnp.int32, (CHUNK, CHUNK), 0)
    jj = lax.broadcasted_iota(jnp.int32, (CHUNK, CHUNK), 1)
    causal = ii >= jj
    left_head = jj < M_HEADDIM

    def blockdiag(slab):
        zero = jnp.zeros_like(slab)
        return jnp.concatenate([jnp.where(left_head, slab, zero), jnp.where(left_head, zero, slab)], axis=0)

    for c in range(N_CHUNKS):
        r0 = c * CHUNK
        acs_c = a_cs[r0:r0 + CHUNK, :]
        acs_t = acs_c.T[:M_HEADS, :]
        dt_t = dt[r0:r0 + CHUNK, :].T[:M_HEADS, :]
        w_t = jnp.exp(acs_t[:, CHUNK - 1:CHUNK] - acs_t) * dt_t
        b_t = bmat[r0:r0 + CHUNK, :].T
        c_c = cmat[r0:r0 + CHUNK, :].astype(bf16)
        cb2 = lax.dot_general(c_c, blockdiag(bmat[r0:r0 + CHUNK, :].astype(bf16)), NT_DIMS,
                              preferred_element_type=f32)
        cb = [cb2[:, :CHUNK], cb2[:, CHUNK:]]
        s_prev = [s_state[g] for g in range(M_GROUPS)]
        y_off = [_dot(c_c[:, g * M_STATE:(g + 1) * M_STATE], s_prev[g].astype(bf16))
                 for g in range(M_GROUPS)]
        for pair in range(M_HEADS // 2):
            g = pair // (HEADS_PER_GROUP // 2)
            gslab = slice((pair % (HEADS_PER_GROUP // 2)) * LANES, (pair % (HEADS_PER_GROUP // 2) + 1) * LANES)
            pcols = slice(pair * LANES, (pair + 1) * LANES)
            heads = (2 * pair, 2 * pair + 1)
            cols = [jnp.broadcast_to(acs_c[:, h:h + 1], (CHUNK, CHUNK)) for h in heads]
            w_rows = [(cb[g] * jnp.exp(jnp.where(causal, col - acs_t[h:h + 1, :], -jnp.inf))
                       * dt_t[h:h + 1, :]).astype(bf16) for h, col in zip(heads, cols)]
            bt_g = b_t[g * M_STATE:(g + 1) * M_STATE, :]
            st_rows = [(bt_g * w_t[h:h + 1, :]).astype(bf16) for h in heads]
            lhs = jnp.concatenate([jnp.concatenate(w_rows, axis=1), jnp.concatenate(st_rows, axis=1)], axis=0)
            res = _dot(lhs, blockdiag(xm_b[r0:r0 + CHUNK, pcols]))
            decay = jnp.exp(jnp.where(left_head, cols[0], cols[1]))
            y_buf[r0:r0 + CHUNK, pcols] = res[:CHUNK] + y_off[g][:, gslab] * decay
            s_state[g, :, gslab] = s_prev[g][:, gslab] * decay[CHUNK - 1:CHUNK, :] + res[CHUNK:]
            emit(ssd_fill)
    emit(ssd_fill, len(ssd_fill))

    y = y_buf[...] + dskip_ref[...] * xm
    yz = y * _silu(z_buf[...])
    y_m = []
    for g in range(M_GROUPS):
        yz_g = yz[:, g * GROUP_W:(g + 1) * GROUP_W]
        y_m.append(yz_g * lax.rsqrt(jnp.mean(yz_g * yz_g, axis=-1, keepdims=True) + EPS))
    y_m = (jnp.concatenate(y_m, axis=1) * mnw_ref[...]).astype(bf16)

    def um_tile(c0):
        um_buf[:, c0:c0 + MXU_W] = _dot(y_m, wpm_ref[:, c0:c0 + MXU_W])
    ret_fill += [functools.partial(um_tile, c0) for c0 in range(0, D_MODEL, MXU_W)]

    cos = jnp.concatenate([cos_ref[...]] * (R_QK // LANES), axis=1)
    sin = jnp.concatenate([sin_ref[...]] * (R_QK // LANES), axis=1)
    lane_q = lax.broadcasted_iota(jnp.int32, (TOK, R_QK), 1)
    first_half = (lane_q % R_HEAD_QK) < (R_HEAD_QK // 2)

    def rotary(t):
        swapped = jnp.where(first_half, pltpu.roll(t, R_QK - R_HEAD_QK // 2, 1),
                            pltpu.roll(t, R_HEAD_QK // 2, 1))
        return t * cos + swapped * sin

    q = rotary(q_buf[...])
    k = rotary(k_buf[...]) * (R_HEAD_QK ** -0.5)
    qdec = jnp.concatenate([qdec_ref[...]] * N_CHUNKS, axis=0)
    kdec = jnp.concatenate([kdec_ref[...]] * N_CHUNKS, axis=0)
    q_b = q.astype(bf16)
    k_b = k.astype(bf16)
    qw_b = (q * qdec).astype(bf16)
    kw = k * kdec

    s_b = {}
    for c in range(N_CHUNKS):
        r0 = c * CHUNK
        for pair in range(R_HEADS // 2):
            pcols = slice(pair * LANES, (pair + 1) * LANES)
            s_pair = (lax.dot_general(q_b[r0:r0 + CHUNK, pcols], blockdiag(k_b[r0:r0 + CHUNK, pcols]), NT_DIMS,
                                      preferred_element_type=f32) * dmat_ref[pair]).astype(bf16)
            s_b[c, 2 * pair], s_b[c, 2 * pair + 1] = s_pair[:, :CHUNK], s_pair[:, CHUNK:]
            emit(ret_fill)
    for c in range(N_CHUNKS):
        r0 = c * CHUNK
        kw_t = kw[r0:r0 + CHUNK, :].T.astype(bf16)
        for h in range(R_HEADS):
            qk_cols = slice(h * R_HEAD_QK, (h + 1) * R_HEAD_QK)
            v_cols = slice(h * R_HEAD_V, (h + 1) * R_HEAD_V)
            r_prev = r_state[h]
            lhs = jnp.concatenate(
                [jnp.concatenate([s_b[c, h], qw_b[r0:r0 + CHUNK, qk_cols]], axis=1),
                 jnp.concatenate([kw_t[qk_cols, :], jnp.zeros((R_HEAD_QK, R_HEAD_QK), bf16)], axis=1)], axis=0)
            res = _dot(lhs, jnp.concatenate([v_buf[r0:r0 + CHUNK, v_cols], r_prev.astype(bf16)], axis=0))
            o_h = res[:CHUNK]
            r_state[h] = r_prev * rdec_ref[h] + res[CHUNK:]
            o_buf[r0:r0 + CHUNK, v_cols] = o_h * lax.rsqrt(
                jnp.mean(o_h * o_h, axis=-1, keepdims=True) + EPS)
            if h % 4 == 3:
                emit(ret_fill)
    emit(ret_fill, len(ret_fill))

    y_r = (_silu(gr_buf[...]) * o_buf[...]).astype(bf16)
    u_r = _dot(y_r, wpr_ref[...])

    merged = _sigmoid(gam_buf[...]) * um_buf[...] + _sigmoid(gar_buf[...]) * u_r
    out = _dot(merged.astype(bf16), wout_ref[...])
    xo = x + gate * out
    o_ref[...] = xo * lax.rsqrt(jnp.mean(xo * xo, axis=-1, keepdims=True) + EPS) * fnw_ref[...]


def _resident(shape):
    zeros = (0,) * len(shape)
    return pl.BlockSpec(shape, lambda b, s: zeros, pipeline_mode=pl.Buffered(1))


def _tables(seq_len):
    half = R_HEAD_QK // 2
    pos = np.arange(seq_len, dtype=np.float64)
    inv = ROPE_BASE ** (-np.arange(half, dtype=np.float64) / half)
    ang = pos[:, None] * inv[None, :]
    cos_t = np.tile(np.cos(ang), (1, LANES // half))
    sin_t = np.tile(np.concatenate([-np.sin(ang), np.sin(ang)], axis=1), (1, LANES // R_HEAD_QK))
    log_g = np.log1p(-np.exp2(-5.0 - np.arange(R_HEADS, dtype=np.float64)))
    idx = np.arange(CHUNK, dtype=np.float64)
    rel = idx[:, None] - idx[None, :]
    dmat = np.where(rel[None] >= 0, np.exp(np.minimum(rel[None], CHUNK) * log_g[:, None, None]), 0.0)
    dmat = np.concatenate([dmat[0::2], dmat[1::2]], axis=2)
    qdec = np.repeat(np.exp((idx + 1)[:, None] * log_g[None, :]), R_HEAD_QK, axis=1)
    kdec = np.repeat(np.exp((CHUNK - 1 - idx)[:, None] * log_g[None, :]), R_HEAD_QK, axis=1)
    rdec = np.broadcast_to(np.exp(CHUNK * log_g)[:, None, None], (R_HEADS, 1, R_HEAD_V))
    t = np.arange(TOK)
    tril = (t[:, None] >= t[None, :]) & (t[:, None] // CHUNK == t[None, :] // CHUNK)
    as_f32 = lambda a: jnp.asarray(np.ascontiguousarray(a, dtype=np.float32))
    as_bf16 = lambda a: jnp.asarray(np.ascontiguousarray(a, dtype=np.float32), dtype=bf16)
    return (as_f32(cos_t), as_f32(sin_t), as_f32(dmat), as_f32(qdec), as_f32(kdec), as_f32(rdec),
            as_bf16(tril))


def _relayout_kernel(w_ref, o_ref):
    n_pre = M_INNER + CONV_DIM
    o_ref[:, :n_pre] = w_ref[:, :n_pre].astype(bf16)
    lane = lax.broadcasted_iota(jnp.int32, (o_ref.shape[0], LANES), 1)
    o_ref[:, n_pre:n_pre + LANES] = jnp.where(lane < M_HEADS, w_ref[:, n_pre:n_pre + LANES], 0.0).astype(bf16)
    o_ref[:, n_pre + LANES:] = w_ref[:, n_pre + M_HEADS:].astype(bf16)


def _layer(x, c, w_ada, b_ada, norm_w, w_in, conv_w, conv_b, dt_bias, a_log, d_skip,
           m_norm_w, w_proj_m, w_proj_r, w_out, out_norm_w):
    bsz, seq_len, d = x.shape
    assert d == D_MODEL and seq_len % TOK == 0

    mod = pl.pallas_call(
        _mod_kernel,
        grid=(3,),
        in_specs=[pl.BlockSpec((bsz, d), lambda j: (0, 0)),
                  pl.BlockSpec((d, d), lambda j: (0, j)),
                  pl.BlockSpec((1, d), lambda j: (0, j))],
        out_specs=pl.BlockSpec((bsz, d), lambda j: (0, j)),
        out_shape=jax.ShapeDtypeStruct((bsz, 3 * d), f32),
        name="adaln_mod",
    )(c, w_ada, b_ada.reshape(1, 3 * d))

    assert w_in.shape == (d, W_COLS - LANES + M_HEADS)
    w_cat = pl.pallas_call(
        _relayout_kernel,
        grid=(d // RELAYOUT_ROWS,),
        in_specs=[pl.BlockSpec((RELAYOUT_ROWS, w_in.shape[1]), lambda i: (i, 0))],
        out_specs=pl.BlockSpec((RELAYOUT_ROWS, W_COLS), lambda i: (i, 0)),
        out_shape=jax.ShapeDtypeStruct((d, W_COLS), bf16),
        name="w_in_relayout",
    )(w_in)
    pad_heads = lambda p: jnp.pad(p.reshape(1, M_HEADS), ((0, 0), (0, LANES - M_HEADS)))
    dskip_e = jnp.repeat(d_skip, M_HEADDIM).reshape(1, M_INNER)
    cos_t, sin_t, dmat, qdec, kdec, rdec, tril = _tables(seq_len)

    tok_block = lambda w: pl.BlockSpec((TOK, w), lambda b, s: (s, 0))
    return pl.pallas_call(
        _block_kernel,
        grid=(bsz, seq_len // TOK),
        in_specs=[
            pl.BlockSpec((None, TOK, d), lambda b, s: (b, s, 0)),
            _resident((bsz, 3 * d)),
            _resident((1, d)),
            _resident((d, W_COLS)),
            _resident((M_CONV, CONV_DIM)), _resident((1, CONV_DIM)),
            _resident((1, LANES)), _resident((1, LANES)),
            _resident((1, M_INNER)), _resident((1, M_INNER)),
            _resident((M_INNER, d)), _resident((R_V, d)), _resident((d, d)),
            _resident((1, d)),
            tok_block(LANES), tok_block(LANES),
            _resident((R_HEADS // 2, CHUNK, 2 * CHUNK)),
            _resident((CHUNK, R_QK)), _resident((CHUNK, R_QK)),
            _resident((R_HEADS, 1, R_HEAD_V)),
            _resident((TOK, TOK)),
        ],
        out_specs=pl.BlockSpec((None, TOK, d), lambda b, s: (b, s, 0)),
        out_shape=jax.ShapeDtypeStruct((bsz, seq_len, d), x.dtype),
        scratch_shapes=[
            pltpu.VMEM((TOK + 2 * SUBLANES, CONV_DIM), f32),
            pltpu.VMEM((M_GROUPS, M_STATE, GROUP_W), f32),
            pltpu.VMEM((R_HEADS, R_HEAD_QK, R_HEAD_V), f32),
            pltpu.VMEM((TOK, M_INNER), f32),
            pltpu.VMEM((TOK, R_V), f32),
            pltpu.VMEM((TOK, M_INNER), f32),
            pltpu.VMEM((TOK, R_V), f32),
            pltpu.VMEM((TOK, D_MODEL), f32),
            pltpu.VMEM((TOK, D_MODEL), f32),
            pltpu.VMEM((TOK, D_MODEL), f32),
            pltpu.VMEM((TOK, R_QK), f32),
            pltpu.VMEM((TOK, R_QK), f32),
            pltpu.VMEM((TOK, R_V), bf16),
        ],
        compiler_params=pltpu.CompilerParams(
            dimension_semantics=("parallel", "arbitrary"),
            vmem_limit_bytes=VMEM_LIMIT_BYTES),
        name="hybrid_block",
    )(x, mod, norm_w.reshape(1, d), w_cat, conv_w, conv_b.reshape(1, CONV_DIM),
      pad_heads(dt_bias), pad_heads(a_log), dskip_e, m_norm_w.reshape(1, M_INNER),
      w_proj_m.astype(bf16), w_proj_r.astype(bf16), w_out.astype(bf16), out_norm_w.reshape(1, d),
      cos_t, sin_t, dmat, qdec, kdec, rdec, tril)


@jax.jit
def kernel(x, c, w_ada, b_ada, norm_w, w_in, conv_w, conv_b, dt_bias, a_log, d_skip, m_norm_w,
           w_proj_m, w_proj_r, w_out, final_norm_w):
    assert w_ada.shape[0] == 1
    return _layer(x, c, w_ada[0], b_ada[0], norm_w[0], w_in[0], conv_w[0], conv_b[0], dt_bias[0],
                  a_log[0], d_skip[0], m_norm_w[0], w_proj_m[0], w_proj_r[0], w_out[0], final_norm_w)
```

```python
import functools

import jax
import jax.numpy as jnp
import numpy as np
from jax import lax
from jax.experimental import pallas as pl
from jax.experimental.pallas import tpu as pltpu

f32 = jnp.float32
bf16 = jnp.bfloat16

D_MODEL = 1024
M_HEADDIM = 64
M_HEADS = 16
M_GROUPS = 2
M_STATE = 64
M_CONV = 4
M_INNER = D_MODEL
GROUP_W = M_INNER // M_GROUPS
HEADS_PER_GROUP = M_HEADS // M_GROUPS
CONV_DIM = M_INNER + 2 * M_GROUPS * M_STATE
R_HEADS = 8
R_QK = 512
R_HEAD_QK = 64
R_V = 1024
R_HEAD_V = 128
CHUNK = 128
ROPE_BASE = 10000.0
EPS = 1e-6

LANES = 128
SUBLANES = 8
MXU_W = 256
TOK = 256
N_CHUNKS = TOK // CHUNK
VMEM_LIMIT_BYTES = 56 * 1024 * 1024

OFF_Z = 0
OFF_XBC = OFF_Z + M_INNER
OFF_DT = OFF_XBC + CONV_DIM
OFF_Q = OFF_DT + MXU_W
OFF_K = OFF_Q + R_QK
OFF_V = OFF_K + R_QK
OFF_GR = OFF_V + R_V
OFF_GAM = OFF_GR + R_V
OFF_GAR = OFF_GAM + D_MODEL
W_COLS = OFF_GAR + D_MODEL

NT_DIMS = (((1,), (1,)), ((), ()))


def _mod_kernel(c_ref, w_ref, b_ref, o_ref):
    o_ref[...] = jnp.dot(c_ref[...], w_ref[...], preferred_element_type=f32,
                         precision=lax.Precision.HIGHEST) + b_ref[...]


def _sigmoid(v):
    return 1.0 / (1.0 + jnp.exp(-v))


def _silu(v):
    return v * _sigmoid(v)


def _softplus(v):
    return jnp.maximum(v, 0.0) + jnp.log(1.0 + jnp.exp(-jnp.abs(v)))


def _dot(a, b):
    return jnp.dot(a, b, preferred_element_type=f32)


def _split3(v):
    hi = v.astype(bf16)
    r1 = v - hi.astype(f32)
    mid = r1.astype(bf16)
    lo = (r1 - mid.astype(f32)).astype(bf16)
    return hi, mid, lo


def _block_kernel(x_ref, mod_ref, normw_ref, win_ref, convw_ref, convb_ref,
                  dtb_ref, alog_ref, dskip_ref, mnw_ref, wpm_ref, wpr_ref, wout_ref, fnw_ref,
                  cos_ref, sin_ref, dmat_ref, qdec_ref, kdec_ref, rdec_ref, tril_ref,
                  o_ref,
                  xbc_buf, s_state, r_state, y_buf, o_buf, z_buf, gr_buf, gam_buf, gar_buf, um_buf,
                  q_buf, k_buf, v_buf):
    @pl.when(pl.program_id(1) == 0)
    def _start_of_sequence():
        xbc_buf[0:SUBLANES, :] = jnp.zeros((SUBLANES, CONV_DIM), f32)
        s_state[...] = jnp.zeros_like(s_state)
        r_state[...] = jnp.zeros_like(r_state)

    x = x_ref[...]
    rs = lax.rsqrt(jnp.mean(x * x, axis=-1, keepdims=True) + EPS)
    mod = mod_ref[pl.ds(pl.program_id(0), 1), :]
    shift, scale, gate = (mod[:, i * D_MODEL:(i + 1) * D_MODEL] for i in range(3))
    g1 = normw_ref[...] * (1.0 + scale)
    hb = (x * rs * g1 + shift).astype(bf16)

    def proj(off, n):
        return _dot(hb, win_ref[:, off:off + n])

    def proj_tiles(dst, off):
        def tile(c0):
            dst[:, c0:c0 + MXU_W] = proj(off + c0, MXU_W).astype(dst.dtype)
        return [functools.partial(tile, c0) for c0 in range(0, dst.shape[1], MXU_W)]

    def emit(queue, n=1):
        for _ in range(min(n, len(queue))):
            queue.pop(0)()

    ssd_fill = (proj_tiles(v_buf, OFF_V) + proj_tiles(q_buf, OFF_Q) + proj_tiles(k_buf, OFF_K)
                + proj_tiles(z_buf, OFF_Z) + proj_tiles(gr_buf, OFF_GR))
    ret_fill = proj_tiles(gam_buf, OFF_GAM) + proj_tiles(gar_buf, OFF_GAR)

    xbc_buf[SUBLANES:SUBLANES + TOK, :] = proj(OFF_XBC, CONV_DIM)
    dt_raw = proj(OFF_DT, LANES)
    emit(ssd_fill, 6)

    conv = convb_ref[...] + xbc_buf[SUBLANES:SUBLANES + TOK, :] * convw_ref[M_CONV - 1:M_CONV, :]
    for kk in range(M_CONV - 1):
        r0 = SUBLANES - (M_CONV - 1) + kk
        conv = conv + xbc_buf[r0:r0 + TOK, :] * convw_ref[kk:kk + 1, :]
    xbc_buf[0:SUBLANES, :] = xbc_buf[TOK:TOK + SUBLANES, :]
    xa = _silu(conv)
    xm = xa[:, :M_INNER]
    bmat = xa[:, M_INNER:M_INNER + LANES]
    cmat = xa[:, M_INNER + LANES:CONV_DIM]
    xm_b = xm.astype(bf16)

    lane = lax.broadcasted_iota(jnp.int32, (1, LANES), 1)
    a_neg = jnp.where(lane < M_HEADS, -jnp.exp(alog_ref[...]), 0.0)
    dt = _softplus(dt_raw + dtb_ref[...])
    a = dt * a_neg
    tril = tril_ref[...]
    a_parts = _dot(tril, jnp.concatenate(_split3(a), axis=1))
    a_cs = a_parts[:, :LANES] + a_parts[:, LANES:2 * LANES] + a_parts[:, 2 * LANES:]
    emit(ssd_fill, 3)

    ii = lax.broadcasted_iota(jnp.int32, (CHUNK, CHUNK), 0)
    jj = lax.broadcasted_iota(jnp.int32, (CHUNK, CHUNK), 1)
    causal = ii >= jj
    left_head = jj < M_HEADDIM

    def blockdiag(slab):
        zero = jnp.zeros_like(slab)
        return jnp.concatenate([jnp.where(left_head, slab, zero), jnp.where(left_head, zero, slab)], axis=0)

    for c in range(N_CHUNKS):
        r0 = c * CHUNK
        acs_c = a_cs[r0:r0 + CHUNK, :]
        acs_t = acs_c.T[:M_HEADS, :]
        dt_t = dt[r0:r0 + CHUNK, :].T[:M_HEADS, :]
        w_t = jnp.exp(acs_t[:, CHUNK - 1:CHUNK] - acs_t) * dt_t
        b_t = bmat[r0:r0 + CHUNK, :].T
        c_c = cmat[r0:r0 + CHUNK, :].astype(bf16)
        cb2 = lax.dot_general(c_c, blockdiag(bmat[r0:r0 + CHUNK, :].astype(bf16)), NT_DIMS,
                              preferred_element_type=f32)
        cb = [cb2[:, :CHUNK], cb2[:, CHUNK:]]
        s_prev = [s_state[g] for g in range(M_GROUPS)]
        y_off = [_dot(c_c[:, g * M_STATE:(g + 1) * M_STATE], s_prev[g].astype(bf16))
                 for g in range(M_GROUPS)]
        for pair in range(M_HEADS // 2):
            g = pair // (HEADS_PER_GROUP // 2)
            gslab = slice((pair % (HEADS_PER_GROUP // 2)) * LANES, (pair % (HEADS_PER_GROUP // 2) + 1) * LANES)
            pcols = slice(pair * LANES, (pair + 1) * LANES)
            heads = (2 * pair, 2 * pair + 1)
            cols = [jnp.broadcast_to(acs_c[:, h:h + 1], (CHUNK, CHUNK)) for h in heads]
            w_rows = [(cb[g] * jnp.exp(jnp.where(causal, col - acs_t[h:h + 1, :], -jnp.inf))
                       * dt_t[h:h + 1, :]).astype(bf16) for h, col in zip(heads, cols)]
            bt_g = b_t[g * M_STATE:(g + 1) * M_STATE, :]
            st_rows = [(bt_g * w_t[h:h + 1, :]).astype(bf16) for h in heads]
            lhs = jnp.concatenate([jnp.concatenate(w_rows, axis=1), jnp.concatenate(st_rows, axis=1)], axis=0)
            res = _dot(lhs, blockdiag(xm_b[r0:r0 + CHUNK, pcols]))
            decay = jnp.exp(jnp.where(left_head, cols[0], cols[1]))
            y_buf[r0:r0 + CHUNK, pcols] = res[:CHUNK] + y_off[g][:, gslab] * decay
            s_state[g, :, gslab] = s_prev[g][:, gslab] * decay[CHUNK - 1:CHUNK, :] + res[CHUNK:]
            emit(ssd_fill)
    emit(ssd_fill, len(ssd_fill))

    y = y_buf[...] + dskip_ref[...] * xm
    yz = y * _silu(z_buf[...])
    y_m = []
    for g in range(M_GROUPS):
        yz_g = yz[:, g * GROUP_W:(g + 1) * GROUP_W]
        y_m.append(yz_g * lax.rsqrt(jnp.mean(yz_g * yz_g, axis=-1, keepdims=True) + EPS))
    y_m = (jnp.concatenate(y_m, axis=1) * mnw_ref[...]).astype(bf16)

    def um_tile(c0):
        um_buf[:, c0:c0 + MXU_W] = _dot(y_m, wpm_ref[:, c0:c0 + MXU_W])
    ret_fill += [functools.partial(um_tile, c0) for c0 in range(0, D_MODEL, MXU_W)]

    cos = jnp.concatenate([cos_ref[...]] * (R_QK // LANES), axis=1)
    sin = jnp.concatenate([sin_ref[...]] * (R_QK // LANES), axis=1)
    lane_q = lax.broadcasted_iota(jnp.int32, (TOK, R_QK), 1)
    first_half = (lane_q % R_HEAD_QK) < (R_HEAD_QK // 2)

    def rotary(t):
        swapped = jnp.where(first_half, pltpu.roll(t, R_QK - R_HEAD_QK // 2, 1),
                            pltpu.roll(t, R_HEAD_QK // 2, 1))
        return t * cos + swapped * sin

    q = rotary(q_buf[...])
    k = rotary(k_buf[...]) * (R_HEAD_QK ** -0.5)
    qdec = jnp.concatenate([qdec_ref[...]] * N_CHUNKS, axis=0)
    kdec = jnp.concatenate([kdec_ref[...]] * N_CHUNKS, axis=0)
    q_b = q.astype(bf16)
    k_b = k.astype(bf16)
    qw_b = (q * qdec).astype(bf16)
    kw = k * kdec

    s_b = {}
    for c in range(N_CHUNKS):
        r0 = c * CHUNK
        for pair in range(R_HEADS // 2):
            pcols = slice(pair * LANES, (pair + 1) * LANES)
            s_pair = (lax.dot_general(q_b[r0:r0 + CHUNK, pcols], blockdiag(k_b[r0:r0 + CHUNK, pcols]), NT_DIMS,
                                      preferred_element_type=f32) * dmat_ref[pair]).astype(bf16)
            s_b[c, 2 * pair], s_b[c, 2 * pair + 1] = s_pair[:, :CHUNK], s_pair[:, CHUNK:]
            emit(ret_fill)
    for c in range(N_CHUNKS):
        r0 = c * CHUNK
        kw_t = kw[r0:r0 + CHUNK, :].T.astype(bf16)
        for h in range(R_HEADS):
            qk_cols = slice(h * R_HEAD_QK, (h + 1) * R_HEAD_QK)
            v_cols = slice(h * R_HEAD_V, (h + 1) * R_HEAD_V)
            r_prev = r_state[h]
            lhs = jnp.concatenate(
                [jnp.concatenate([s_b[c, h], qw_b[r0:r0 + CHUNK, qk_cols]], axis=1),
                 jnp.concatenate([kw_t[qk_cols, :], jnp.zeros((R_HEAD_QK, R_HEAD_QK), bf16)], axis=1)], axis=0)
            res = _dot(lhs, jnp.concatenate([v_buf[r0:r0 + CHUNK, v_cols], r_prev.astype(bf16)], axis=0))
            o_h = res[:CHUNK]
            r_state[h] = r_prev * rdec_ref[h] + res[CHUNK:]
            o_buf[r0:r0 + CHUNK, v_cols] = o_h * lax.rsqrt(
                jnp.mean(o_h * o_h, axis=-1, keepdims=True) + EPS)
            if h % 4 == 3:
                emit(ret_fill)
    emit(ret_fill, len(ret_fill))

    y_r = (_silu(gr_buf[...]) * o_buf[...]).astype(bf16)
    u_r = _dot(y_r, wpr_ref[...])

    merged = _sigmoid(gam_buf[...]) * um_buf[...] + _sigmoid(gar_buf[...]) * u_r
    out = _dot(merged.astype(bf16), wout_ref[...])
    xo = x + gate * out
    o_ref[...] = xo * lax.rsqrt(jnp.mean(xo * xo, axis=-1, keepdims=True) + EPS) * fnw_ref[...]


def _resident(shape):
    zeros = (0,) * len(shape)
    return pl.BlockSpec(shape, lambda b, s: zeros, pipeline_mode=pl.Buffered(1))


def _tables(seq_len):
    half = R_HEAD_QK // 2
    pos = np.arange(seq_len, dtype=np.float64)
    inv = ROPE_BASE ** (-np.arange(half, dtype=np.float64) / half)
    ang = pos[:, None] * inv[None, :]
    cos_t = np.tile(np.cos(ang), (1, LANES // half))
    sin_t = np.tile(np.concatenate([-np.sin(ang), np.sin(ang)], axis=1), (1, LANES // R_HEAD_QK))
    log_g = np.log1p(-np.exp2(-5.0 - np.arange(R_HEADS, dtype=np.float64)))
    idx = np.arange(CHUNK, dtype=np.float64)
    rel = idx[:, None] - idx[None, :]
    dmat = np.where(rel[None] >= 0, np.exp(np.minimum(rel[None], CHUNK) * log_g[:, None, None]), 0.0)
    dmat = np.concatenate([dmat[0::2], dmat[1::2]], axis=2)
    qdec = np.repeat(np.exp((idx + 1)[:, None] * log_g[None, :]), R_HEAD_QK, axis=1)
    kdec = np.repeat(np.exp((CHUNK - 1 - idx)[:, None] * log_g[None, :]), R_HEAD_QK, axis=1)
    rdec = np.broadcast_to(np.exp(CHUNK * log_g)[:, None, None], (R_HEADS, 1, R_HEAD_V))
    t = np.arange(TOK)
    tril = (t[:, None] >= t[None, :]) & (t[:, None] // CHUNK == t[None, :] // CHUNK)
    as_f32 = lambda a: jnp.asarray(np.ascontiguousarray(a, dtype=np.float32))
    as_bf16 = lambda a: jnp.asarray(np.ascontiguousarray(a, dtype=np.float32), dtype=bf16)
    return (as_f32(cos_t), as_f32(sin_t), as_f32(dmat), as_f32(qdec), as_f32(kdec), as_f32(rdec),
            as_bf16(tril))


def _relayout_kernel(wt_ref, o_ref):
    w = wt_ref[...].T
    col = lax.broadcasted_iota(jnp.int32, w.shape, 1)
    is_dt_tile = pl.program_id(0) == OFF_DT // MXU_W
    o_ref[...] = jnp.where(jnp.logical_and(is_dt_tile, col >= M_HEADS), 0.0, w).astype(bf16)


def _relayout_row_start(j):
    tile, skip = MXU_W // SUBLANES, (MXU_W - M_HEADS) // SUBLANES
    return jnp.where(j <= OFF_DT // MXU_W, j * tile, j * tile - skip) * SUBLANES


def _layer(x, c, w_ada, b_ada, norm_w, w_in_t, conv_w, conv_b, dt_bias, a_log, d_skip,
           m_norm_w, w_proj_m, w_proj_r, w_out, out_norm_w):
    bsz, seq_len, d = x.shape
    assert d == D_MODEL and seq_len % TOK == 0

    mod = pl.pallas_call(
        _mod_kernel,
        grid=(3,),
        in_specs=[pl.BlockSpec((bsz, d), lambda j: (0, 0)),
                  pl.BlockSpec((d, d), lambda j: (0, j)),
                  pl.BlockSpec((1, d), lambda j: (0, j))],
        out_specs=pl.BlockSpec((bsz, d), lambda j: (0, j)),
        out_shape=jax.ShapeDtypeStruct((bsz, 3 * d), f32),
        name="adaln_mod",
    )(c, w_ada, b_ada.reshape(1, 3 * d))

    assert w_in_t.shape == (W_COLS - MXU_W + M_HEADS, d)
    w_cat = pl.pallas_call(
        _relayout_kernel,
        grid=(W_COLS // MXU_W,),
        in_specs=[pl.BlockSpec((pl.Element(MXU_W), pl.Element(d)), lambda j: (_relayout_row_start(j), 0))],
        out_specs=pl.BlockSpec((d, MXU_W), lambda j: (0, j)),
        out_shape=jax.ShapeDtypeStruct((d, W_COLS), bf16),
        name="w_in_relayout",
    )(w_in_t)
    pad_heads = lambda p: jnp.pad(p.reshape(1, M_HEADS), ((0, 0), (0, LANES - M_HEADS)))
    dskip_e = jnp.repeat(d_skip, M_HEADDIM).reshape(1, M_INNER)
    cos_t, sin_t, dmat, qdec, kdec, rdec, tril = _tables(seq_len)

    tok_block = lambda w: pl.BlockSpec((TOK, w), lambda b, s: (s, 0))
    return pl.pallas_call(
        _block_kernel,
        grid=(bsz, seq_len // TOK),
        in_specs=[
            pl.BlockSpec((None, TOK, d), lambda b, s: (b, s, 0)),
            _resident((bsz, 3 * d)),
            _resident((1, d)),
            _resident((d, W_COLS)),
            _resident((M_CONV, CONV_DIM)), _resident((1, CONV_DIM)),
            _resident((1, LANES)), _resident((1, LANES)),
            _resident((1, M_INNER)), _resident((1, M_INNER)),
            _resident((M_INNER, d)), _resident((R_V, d)), _resident((d, d)),
            _resident((1, d)),
            tok_block(LANES), tok_block(LANES),
            _resident((R_HEADS // 2, CHUNK, 2 * CHUNK)),
            _resident((CHUNK, R_QK)), _resident((CHUNK, R_QK)),
            _resident((R_HEADS, 1, R_HEAD_V)),
            _resident((TOK, TOK)),
        ],
        out_specs=pl.BlockSpec((None, TOK, d), lambda b, s: (b, s, 0)),
        out_shape=jax.ShapeDtypeStruct((bsz, seq_len, d), x.dtype),
        scratch_shapes=[
            pltpu.VMEM((TOK + 2 * SUBLANES, CONV_DIM), f32),
            pltpu.VMEM((M_GROUPS, M_STATE, GROUP_W), f32),
            pltpu.VMEM((R_HEADS, R_HEAD_QK, R_HEAD_V), f32),
            pltpu.VMEM((TOK, M_INNER), f32),
            pltpu.VMEM((TOK, R_V), f32),
            pltpu.VMEM((TOK, M_INNER), f32),
            pltpu.VMEM((TOK, R_V), f32),
            pltpu.VMEM((TOK, D_MODEL), f32),
            pltpu.VMEM((TOK, D_MODEL), f32),
            pltpu.VMEM((TOK, D_MODEL), f32),
            pltpu.VMEM((TOK, R_QK), f32),
            pltpu.VMEM((TOK, R_QK), f32),
            pltpu.VMEM((TOK, R_V), bf16),
        ],
        compiler_params=pltpu.CompilerParams(
            dimension_semantics=("parallel", "arbitrary"),
            vmem_limit_bytes=VMEM_LIMIT_BYTES),
        name="hybrid_block",
    )(x, mod, norm_w.reshape(1, d), w_cat, conv_w, conv_b.reshape(1, CONV_DIM),
      pad_heads(dt_bias), pad_heads(a_log), dskip_e, m_norm_w.reshape(1, M_INNER),
      w_proj_m.astype(bf16), w_proj_r.astype(bf16), w_out.astype(bf16), out_norm_w.reshape(1, d),
      cos_t, sin_t, dmat, qdec, kdec, rdec, tril)


@jax.jit
def kernel(x, c, w_ada, b_ada, norm_w, w_in, conv_w, conv_b, dt_bias, a_log, d_skip, m_norm_w,
           w_proj_m, w_proj_r, w_out, final_norm_w):
    assert w_ada.shape[0] == 1
    return _layer(x, c, w_ada[0], b_ada[0], norm_w[0], jnp.swapaxes(w_in, 1, 2)[0], conv_w[0], conv_b[0], dt_bias[0],
                  a_log[0], d_skip[0], m_norm_w[0], w_proj_m[0], w_proj_r[0], w_out[0], final_norm_w)
```

```python
import functools

import jax
import jax.numpy as jnp
import numpy as np
from jax import lax
from jax.experimental import pallas as pl
from jax.experimental.pallas import tpu as pltpu

f32 = jnp.float32
bf16 = jnp.bfloat16

D_MODEL = 1024
M_HEADDIM = 64
M_HEADS = 16
M_GROUPS = 2
M_STATE = 64
M_CONV = 4
M_INNER = D_MODEL
GROUP_W = M_INNER // M_GROUPS
HEADS_PER_GROUP = M_HEADS // M_GROUPS
CONV_DIM = M_INNER + 2 * M_GROUPS * M_STATE
R_HEADS = 8
R_QK = 512
R_HEAD_QK = 64
R_V = 1024
R_HEAD_V = 128
CHUNK = 128
ROPE_BASE = 10000.0
EPS = 1e-6

LANES = 128
SUBLANES = 8
MXU_W = 256
TOK = 256
N_CHUNKS = TOK // CHUNK
VMEM_LIMIT_BYTES = 56 * 1024 * 1024

OFF_Z = 0
OFF_XBC = OFF_Z + M_INNER
OFF_DT = OFF_XBC + CONV_DIM
OFF_Q = OFF_DT + MXU_W
OFF_K = OFF_Q + R_QK
OFF_V = OFF_K + R_QK
OFF_GR = OFF_V + R_V
OFF_GAM = OFF_GR + R_V
OFF_GAR = OFF_GAM + D_MODEL
W_COLS = OFF_GAR + D_MODEL

NT_DIMS = (((1,), (1,)), ((), ()))


def _mod_kernel(c_ref, w_ref, b_ref, o_ref):
    o_ref[...] = jnp.dot(c_ref[...], w_ref[...], preferred_element_type=f32,
                         precision=lax.Precision.HIGHEST) + b_ref[...]


def _sigmoid(v):
    return 1.0 / (1.0 + jnp.exp(-v))


def _silu(v):
    return v * _sigmoid(v)


def _softplus(v):
    return jnp.maximum(v, 0.0) + jnp.log(1.0 + jnp.exp(-jnp.abs(v)))


def _dot(a, b):
    return jnp.dot(a, b, preferred_element_type=f32)


def _split3(v):
    hi = v.astype(bf16)
    r1 = v - hi.astype(f32)
    mid = r1.astype(bf16)
    lo = (r1 - mid.astype(f32)).astype(bf16)
    return hi, mid, lo


def _block_kernel(x_ref, mod_ref, normw_ref, win_ref, convw_ref, convb_ref,
                  dtb_ref, alog_ref, dskip_ref, mnw_ref, wpm_ref, wpr_ref, wout_ref, fnw_ref,
                  cos_ref, sin_ref, dmat_ref, qdec_ref, kdec_ref, rdec_ref, tril_ref,
                  o_ref,
                  xbc_buf, s_state, r_state, y_buf, o_buf, z_buf, gr_buf, gam_buf, gar_buf,
                  q_buf, k_buf, v_buf):
    @pl.when(pl.program_id(1) == 0)
    def _start_of_sequence():
        xbc_buf[0:SUBLANES, :] = jnp.zeros((SUBLANES, CONV_DIM), f32)
        s_state[...] = jnp.zeros_like(s_state)
        r_state[...] = jnp.zeros_like(r_state)

    x = x_ref[...]
    rs = lax.rsqrt(jnp.mean(x * x, axis=-1, keepdims=True) + EPS)
    mod = mod_ref[pl.ds(pl.program_id(0), 1), :]
    shift, scale, gate = (mod[:, i * D_MODEL:(i + 1) * D_MODEL] for i in range(3))
    g1 = normw_ref[...] * (1.0 + scale)
    hb = (x * rs * g1 + shift).astype(bf16)

    def proj(off, n):
        return _dot(hb, win_ref[:, off:off + n])

    def proj_tiles(dst, off):
        def tile(c0):
            dst[:, c0:c0 + MXU_W] = proj(off + c0, MXU_W).astype(dst.dtype)
        return [functools.partial(tile, c0) for c0 in range(0, dst.shape[1], MXU_W)]

    def emit(queue, n=1):
        for _ in range(min(n, len(queue))):
            queue.pop(0)()

    early_fill = proj_tiles(q_buf, OFF_Q) + proj_tiles(k_buf, OFF_K) + proj_tiles(v_buf, OFF_V)
    loop_fill = (proj_tiles(z_buf, OFF_Z) + proj_tiles(gr_buf, OFF_GR)
                 + proj_tiles(gam_buf, OFF_GAM) + proj_tiles(gar_buf, OFF_GAR))

    xbc_dt = proj(OFF_XBC, CONV_DIM + MXU_W)
    xbc_buf[SUBLANES:SUBLANES + TOK, :] = xbc_dt[:, :CONV_DIM]
    dt_raw = xbc_dt[:, CONV_DIM:CONV_DIM + LANES]
    emit(early_fill, 5)

    conv = convb_ref[...] + xbc_buf[SUBLANES:SUBLANES + TOK, :] * convw_ref[M_CONV - 1:M_CONV, :]
    for kk in range(M_CONV - 1):
        r0 = SUBLANES - (M_CONV - 1) + kk
        conv = conv + xbc_buf[r0:r0 + TOK, :] * convw_ref[kk:kk + 1, :]
    xbc_buf[0:SUBLANES, :] = xbc_buf[TOK:TOK + SUBLANES, :]
    xa = _silu(conv)
    xm = xa[:, :M_INNER]
    bmat = xa[:, M_INNER:M_INNER + LANES]
    cmat = xa[:, M_INNER + LANES:CONV_DIM]
    xm_b = xm.astype(bf16)

    lane = lax.broadcasted_iota(jnp.int32, (1, LANES), 1)
    a_neg = jnp.where(lane < M_HEADS, -jnp.exp(alog_ref[...]), 0.0)
    dt = _softplus(dt_raw + dtb_ref[...])
    a = dt * a_neg
    tril = tril_ref[...]
    a_parts = _dot(tril, jnp.concatenate(_split3(a), axis=1))
    a_cs = a_parts[:, :LANES] + a_parts[:, LANES:2 * LANES] + a_parts[:, 2 * LANES:]
    emit(early_fill, len(early_fill))

    ii = lax.broadcasted_iota(jnp.int32, (CHUNK, CHUNK), 0)
    jj = lax.broadcasted_iota(jnp.int32, (CHUNK, CHUNK), 1)
    causal = ii >= jj
    left_head = jj < M_HEADDIM

    def blockdiag(slab):
        zero = jnp.zeros_like(slab)
        return jnp.concatenate([jnp.where(left_head, slab, zero), jnp.where(left_head, zero, slab)], axis=0)

    def ssd_chunk(c):
        r0 = c * CHUNK
        acs_c = a_cs[r0:r0 + CHUNK, :]
        acs_t = acs_c.T[:M_HEADS, :]
        dt_t = dt[r0:r0 + CHUNK, :].T[:M_HEADS, :]
        w_t = jnp.exp(acs_t[:, CHUNK - 1:CHUNK] - acs_t) * dt_t
        b_t = bmat[r0:r0 + CHUNK, :].T
        c_c = cmat[r0:r0 + CHUNK, :].astype(bf16)
        cb2 = lax.dot_general(c_c, blockdiag(bmat[r0:r0 + CHUNK, :].astype(bf16)), NT_DIMS,
                              preferred_element_type=f32)
        s_prev = [s_state[g] for g in range(M_GROUPS)]
        y_off = [_dot(c_c[:, g * M_STATE:(g + 1) * M_STATE], s_prev[g].astype(bf16))
                 for g in range(M_GROUPS)]
        return acs_c, acs_t, dt_t, w_t, b_t, [cb2[:, :CHUNK], cb2[:, CHUNK:]], s_prev, y_off

    def ssd_pair(c, pair, chunk):
        acs_c, acs_t, dt_t, w_t, b_t, cb, s_prev, y_off = chunk
        r0 = c * CHUNK
        g = pair // (HEADS_PER_GROUP // 2)
        gslab = slice((pair % (HEADS_PER_GROUP // 2)) * LANES, (pair % (HEADS_PER_GROUP // 2) + 1) * LANES)
        pcols = slice(pair * LANES, (pair + 1) * LANES)
        heads = (2 * pair, 2 * pair + 1)
        cols = [jnp.broadcast_to(acs_c[:, h:h + 1], (CHUNK, CHUNK)) for h in heads]
        w_rows = [(cb[g] * jnp.exp(jnp.where(causal, col - acs_t[h:h + 1, :], -jnp.inf))
                   * dt_t[h:h + 1, :]).astype(bf16) for h, col in zip(heads, cols)]
        bt_g = b_t[g * M_STATE:(g + 1) * M_STATE, :]
        st_rows = [(bt_g * w_t[h:h + 1, :]).astype(bf16) for h in heads]
        lhs = jnp.concatenate([jnp.concatenate(w_rows, axis=1), jnp.concatenate(st_rows, axis=1)], axis=0)
        res = _dot(lhs, blockdiag(xm_b[r0:r0 + CHUNK, pcols]))
        decay = jnp.exp(jnp.where(left_head, cols[0], cols[1]))
        y_buf[r0:r0 + CHUNK, pcols] = res[:CHUNK] + y_off[g][:, gslab] * decay
        s_state[g, :, gslab] = s_prev[g][:, gslab] * decay[CHUNK - 1:CHUNK, :] + res[CHUNK:]

    cos = jnp.concatenate([cos_ref[...]] * (R_QK // LANES), axis=1)
    sin = jnp.concatenate([sin_ref[...]] * (R_QK // LANES), axis=1)
    lane_q = lax.broadcasted_iota(jnp.int32, (TOK, R_QK), 1)
    first_half = (lane_q % R_HEAD_QK) < (R_HEAD_QK // 2)

    def rotary(t):
        swapped = jnp.where(first_half, pltpu.roll(t, R_QK - R_HEAD_QK // 2, 1),
                            pltpu.roll(t, R_HEAD_QK // 2, 1))
        return t * cos + swapped * sin

    q = rotary(q_buf[...])
    k = rotary(k_buf[...]) * (R_HEAD_QK ** -0.5)
    qdec = jnp.concatenate([qdec_ref[...]] * N_CHUNKS, axis=0)
    kdec = jnp.concatenate([kdec_ref[...]] * N_CHUNKS, axis=0)
    q_b = q.astype(bf16)
    k_b = k.astype(bf16)
    qw_b = (q * qdec).astype(bf16)
    kw = k * kdec
    s_b = {}

    def score_pair(c, pair):
        r0 = c * CHUNK
        pcols = slice(pair * LANES, (pair + 1) * LANES)
        s_pair = (lax.dot_general(q_b[r0:r0 + CHUNK, pcols], blockdiag(k_b[r0:r0 + CHUNK, pcols]), NT_DIMS,
                                  preferred_element_type=f32) * dmat_ref[pair]).astype(bf16)
        s_b[c, 2 * pair], s_b[c, 2 * pair + 1] = s_pair[:, :CHUNK], s_pair[:, CHUNK:]

    def retention_head(c, h, kw_t):
        r0 = c * CHUNK
        qk_cols = slice(h * R_HEAD_QK, (h + 1) * R_HEAD_QK)
        v_cols = slice(h * R_HEAD_V, (h + 1) * R_HEAD_V)
        r_prev = r_state[h]
        lhs = jnp.concatenate(
            [jnp.concatenate([s_b[c, h], qw_b[r0:r0 + CHUNK, qk_cols]], axis=1),
             jnp.concatenate([kw_t[qk_cols, :], jnp.zeros((R_HEAD_QK, R_HEAD_QK), bf16)], axis=1)], axis=0)
        res = _dot(lhs, jnp.concatenate([v_buf[r0:r0 + CHUNK, v_cols], r_prev.astype(bf16)], axis=0))
        o_h = res[:CHUNK]
        r_state[h] = r_prev * rdec_ref[h] + res[CHUNK:]
        o_buf[r0:r0 + CHUNK, v_cols] = o_h * lax.rsqrt(jnp.mean(o_h * o_h, axis=-1, keepdims=True) + EPS)

    for pair in range(R_HEADS // 2):
        score_pair(0, pair)
        emit(loop_fill)
    for c in range(N_CHUNKS):
        chunk = ssd_chunk(c)
        kw_t = kw[c * CHUNK:(c + 1) * CHUNK, :].T.astype(bf16)
        for i in range(R_HEADS):
            ssd_pair(c, i, chunk)
            if c + 1 < N_CHUNKS and i < R_HEADS // 2:
                score_pair(c + 1, i)
            retention_head(c, i, kw_t)
            emit(loop_fill)
    assert not loop_fill and M_HEADS // 2 == R_HEADS

    y_r = (_silu(gr_buf[...]) * o_buf[...]).astype(bf16)
    u_r = _dot(y_r, wpr_ref[...])

    y = y_buf[...] + dskip_ref[...] * xm
    yz = y * _silu(z_buf[...])
    y_m = []
    for g in range(M_GROUPS):
        yz_g = yz[:, g * GROUP_W:(g + 1) * GROUP_W]
        y_m.append(yz_g * lax.rsqrt(jnp.mean(yz_g * yz_g, axis=-1, keepdims=True) + EPS))
    y_m = (jnp.concatenate(y_m, axis=1) * mnw_ref[...]).astype(bf16)
    u_m = _dot(y_m, wpm_ref[...])

    merged = _sigmoid(gam_buf[...]) * u_m + _sigmoid(gar_buf[...]) * u_r
    out = _dot(merged.astype(bf16), wout_ref[...])
    xo = x + gate * out
    o_ref[...] = xo * lax.rsqrt(jnp.mean(xo * xo, axis=-1, keepdims=True) + EPS) * fnw_ref[...]


def _resident(shape):
    zeros = (0,) * len(shape)
    return pl.BlockSpec(shape, lambda b, s: zeros, pipeline_mode=pl.Buffered(1))


def _tables(seq_len):
    half = R_HEAD_QK // 2
    pos = np.arange(seq_len, dtype=np.float64)
    inv = ROPE_BASE ** (-np.arange(half, dtype=np.float64) / half)
    ang = pos[:, None] * inv[None, :]
    cos_t = np.tile(np.cos(ang), (1, LANES // half))
    sin_t = np.tile(np.concatenate([-np.sin(ang), np.sin(ang)], axis=1), (1, LANES // R_HEAD_QK))
    log_g = np.log1p(-np.exp2(-5.0 - np.arange(R_HEADS, dtype=np.float64)))
    idx = np.arange(CHUNK, dtype=np.float64)
    rel = idx[:, None] - idx[None, :]
    dmat = np.where(rel[None] >= 0, np.exp(np.minimum(rel[None], CHUNK) * log_g[:, None, None]), 0.0)
    dmat = np.concatenate([dmat[0::2], dmat[1::2]], axis=2)
    qdec = np.repeat(np.exp((idx + 1)[:, None] * log_g[None, :]), R_HEAD_QK, axis=1)
    kdec = np.repeat(np.exp((CHUNK - 1 - idx)[:, None] * log_g[None, :]), R_HEAD_QK, axis=1)
    rdec = np.broadcast_to(np.exp(CHUNK * log_g)[:, None, None], (R_HEADS, 1, R_HEAD_V))
    t = np.arange(TOK)
    tril = (t[:, None] >= t[None, :]) & (t[:, None] // CHUNK == t[None, :] // CHUNK)
    as_f32 = lambda a: jnp.asarray(np.ascontiguousarray(a, dtype=np.float32))
    as_bf16 = lambda a: jnp.asarray(np.ascontiguousarray(a, dtype=np.float32), dtype=bf16)
    return (as_f32(cos_t), as_f32(sin_t), as_f32(dmat), as_f32(qdec), as_f32(kdec), as_f32(rdec),
            as_bf16(tril))


def _relayout_kernel(wt_ref, o_ref):
    w = wt_ref[...].T
    col = lax.broadcasted_iota(jnp.int32, w.shape, 1)
    is_dt_tile = pl.program_id(0) == OFF_DT // MXU_W
    o_ref[...] = jnp.where(jnp.logical_and(is_dt_tile, col >= M_HEADS), 0.0, w).astype(bf16)


def _relayout_row_start(j):
    tile, skip = MXU_W // SUBLANES, (MXU_W - M_HEADS) // SUBLANES
    return jnp.where(j <= OFF_DT // MXU_W, j * tile, j * tile - skip) * SUBLANES


def _layer(x, c, w_ada, b_ada, norm_w, w_in_t, conv_w, conv_b, dt_bias, a_log, d_skip,
           m_norm_w, w_proj_m, w_proj_r, w_out, out_norm_w):
    bsz, seq_len, d = x.shape
    assert d == D_MODEL and seq_len % TOK == 0

    mod = pl.pallas_call(
        _mod_kernel,
        grid=(3,),
        in_specs=[pl.BlockSpec((bsz, d), lambda j: (0, 0)),
                  pl.BlockSpec((d, d), lambda j: (0, j)),
                  pl.BlockSpec((1, d), lambda j: (0, j))],
        out_specs=pl.BlockSpec((bsz, d), lambda j: (0, j)),
        out_shape=jax.ShapeDtypeStruct((bsz, 3 * d), f32),
        name="adaln_mod",
    )(c, w_ada, b_ada.reshape(1, 3 * d))

    assert w_in_t.shape == (W_COLS - MXU_W + M_HEADS, d)
    w_cat = pl.pallas_call(
        _relayout_kernel,
        grid=(W_COLS // MXU_W,),
        in_specs=[pl.BlockSpec((pl.Element(MXU_W), pl.Element(d)), lambda j: (_relayout_row_start(j), 0))],
        out_specs=pl.BlockSpec((d, MXU_W), lambda j: (0, j)),
        out_shape=jax.ShapeDtypeStruct((d, W_COLS), bf16),
        name="w_in_relayout",
    )(w_in_t)
    pad_heads = lambda p: jnp.pad(p.reshape(1, M_HEADS), ((0, 0), (0, LANES - M_HEADS)))
    dskip_e = jnp.repeat(d_skip, M_HEADDIM).reshape(1, M_INNER)
    cos_t, sin_t, dmat, qdec, kdec, rdec, tril = _tables(seq_len)

    tok_block = lambda w: pl.BlockSpec((TOK, w), lambda b, s: (s, 0))
    return pl.pallas_call(
        _block_kernel,
        grid=(bsz, seq_len // TOK),
        in_specs=[
            pl.BlockSpec((None, TOK, d), lambda b, s: (b, s, 0)),
            _resident((bsz, 3 * d)),
            _resident((1, d)),
            _resident((d, W_COLS)),
            _resident((M_CONV, CONV_DIM)), _resident((1, CONV_DIM)),
            _resident((1, LANES)), _resident((1, LANES)),
            _resident((1, M_INNER)), _resident((1, M_INNER)),
            _resident((M_INNER, d)), _resident((R_V, d)), _resident((d, d)),
            _resident((1, d)),
            tok_block(LANES), tok_block(LANES),
            _resident((R_HEADS // 2, CHUNK, 2 * CHUNK)),
            _resident((CHUNK, R_QK)), _resident((CHUNK, R_QK)),
            _resident((R_HEADS, 1, R_HEAD_V)),
            _resident((TOK, TOK)),
        ],
        out_specs=pl.BlockSpec((None, TOK, d), lambda b, s: (b, s, 0)),
        out_shape=jax.ShapeDtypeStruct((bsz, seq_len, d), x.dtype),
        scratch_shapes=[
            pltpu.VMEM((TOK + 2 * SUBLANES, CONV_DIM), f32),
            pltpu.VMEM((M_GROUPS, M_STATE, GROUP_W), f32),
            pltpu.VMEM((R_HEADS, R_HEAD_QK, R_HEAD_V), f32),
            pltpu.VMEM((TOK, M_INNER), f32),
            pltpu.VMEM((TOK, R_V), f32),
            pltpu.VMEM((TOK, M_INNER), f32),
            pltpu.VMEM((TOK, R_V), f32),
            pltpu.VMEM((TOK, D_MODEL), f32),
            pltpu.VMEM((TOK, D_MODEL), f32),
            pltpu.VMEM((TOK, R_QK), f32),
            pltpu.VMEM((TOK, R_QK), f32),
            pltpu.VMEM((TOK, R_V), bf16),
        ],
        compiler_params=pltpu.CompilerParams(
            dimension_semantics=("parallel", "arbitrary"),
            vmem_limit_bytes=VMEM_LIMIT_BYTES),
        name="hybrid_block",
    )(x, mod, norm_w.reshape(1, d), w_cat, conv_w, conv_b.reshape(1, CONV_DIM),
      pad_heads(dt_bias), pad_heads(a_log), dskip_e, m_norm_w.reshape(1, M_INNER),
      w_proj_m.astype(bf16), w_proj_r.astype(bf16), w_out.astype(bf16), out_norm_w.reshape(1, d),
      cos_t, sin_t, dmat, qdec, kdec, rdec, tril)


@jax.jit
def kernel(x, c, w_ada, b_ada, norm_w, w_in, conv_w, conv_b, dt_bias, a_log, d_skip, m_norm_w,
           w_proj_m, w_proj_r, w_out, final_norm_w):
    assert w_ada.shape[0] == 1
    return _layer(x, c, w_ada[0], b_ada[0], norm_w[0], jnp.swapaxes(w_in, 1, 2)[0], conv_w[0], conv_b[0], dt_bias[0],
                  a_log[0], d_skip[0], m_norm_w[0], w_proj_m[0], w_proj_r[0], w_out[0], final_norm_w)
```

```python
import functools

import jax
import jax.numpy as jnp
import numpy as np
from jax import lax
from jax.experimental import pallas as pl
from jax.experimental.pallas import tpu as pltpu

f32 = jnp.float32
bf16 = jnp.bfloat16

D_MODEL = 1024
M_HEADDIM = 64
M_HEADS = 16
M_GROUPS = 2
M_STATE = 64
M_CONV = 4
M_INNER = D_MODEL
GROUP_W = M_INNER // M_GROUPS
HEADS_PER_GROUP = M_HEADS // M_GROUPS
CONV_DIM = M_INNER + 2 * M_GROUPS * M_STATE
R_HEADS = 8
R_QK = 512
R_HEAD_QK = 64
R_V = 1024
R_HEAD_V = 128
CHUNK = 128
ROPE_BASE = 10000.0
EPS = 1e-6
LOG2_E = 1.4426950408889634

LANES = 128
SUBLANES = 8
MXU_W = 256
TOK = 512
N_CHUNKS = TOK // CHUNK
VMEM_LIMIT_BYTES = 56 * 1024 * 1024

OFF_Z = 0
OFF_XBC = OFF_Z + M_INNER
OFF_DT = OFF_XBC + CONV_DIM
OFF_Q = OFF_DT + MXU_W
OFF_K = OFF_Q + R_QK
OFF_V = OFF_K + R_QK
OFF_GR = OFF_V + R_V
OFF_GAM = OFF_GR + R_V
OFF_GAR = OFF_GAM + D_MODEL
W_COLS = OFF_GAR + D_MODEL

NT_DIMS = (((1,), (1,)), ((), ()))


def _mod_kernel(c_ref, w_ref, b_ref, o_ref):
    o_ref[...] = jnp.dot(c_ref[...], w_ref[...], preferred_element_type=f32,
                         precision=lax.Precision.HIGHEST) + b_ref[...]


def _sigmoid(v):
    return 1.0 / (1.0 + jnp.exp(-v))


def _silu(v):
    return v * _sigmoid(v)


def _softplus(v):
    return jnp.maximum(v, 0.0) + jnp.log1p(jnp.exp(-jnp.abs(v)))


def _dot(a, b):
    return jnp.dot(a, b, preferred_element_type=f32)


def _split3(v):
    hi = v.astype(bf16)
    r1 = v - hi.astype(f32)
    mid = r1.astype(bf16)
    lo = (r1 - mid.astype(f32)).astype(bf16)
    return hi, mid, lo


def _block_kernel(x_ref, mod_ref, normw_ref, win_ref, convw_ref, convb_ref,
                  dtb_ref, alog_ref, dskip_ref, mnw_ref, wpm_ref, wpr_ref, wout_ref, fnw_ref,
                  cos_ref, sin_ref, dmat_ref, qdec_ref, kdec_ref, rdec_ref, tril_ref,
                  o_ref,
                  xbc_buf, s_state, r_state, y_buf, o_buf, z_buf, gr_buf, gam_buf, gar_buf,
                  q_buf, k_buf, v_buf):
    @pl.when(pl.program_id(1) == 0)
    def _start_of_sequence():
        xbc_buf[0:SUBLANES, :] = jnp.zeros((SUBLANES, CONV_DIM), f32)
        s_state[...] = jnp.zeros_like(s_state)
        r_state[...] = jnp.zeros_like(r_state)

    x = x_ref[...]
    rs = lax.rsqrt(jnp.mean(x * x, axis=-1, keepdims=True) + EPS)
    mod = mod_ref[pl.ds(pl.program_id(0), 1), :]
    shift, scale, gate = (mod[:, i * D_MODEL:(i + 1) * D_MODEL] for i in range(3))
    g1 = normw_ref[...] * (1.0 + scale)
    hb = (x * rs * g1 + shift).astype(bf16)

    def proj(off, n):
        return _dot(hb, win_ref[:, off:off + n])

    def proj_tiles(dst, off):
        def tile(c0):
            dst[:, c0:c0 + MXU_W] = proj(off + c0, MXU_W).astype(dst.dtype)
        return [functools.partial(tile, c0) for c0 in range(0, dst.shape[1], MXU_W)]

    def emit(queue, n=1):
        for _ in range(min(n, len(queue))):
            queue.pop(0)()

    early_fill = proj_tiles(q_buf, OFF_Q) + proj_tiles(k_buf, OFF_K) + proj_tiles(v_buf, OFF_V)
    loop_fill = (proj_tiles(z_buf, OFF_Z) + proj_tiles(gr_buf, OFF_GR)
                 + proj_tiles(gam_buf, OFF_GAM) + proj_tiles(gar_buf, OFF_GAR))

    xbc_dt = proj(OFF_XBC, CONV_DIM + MXU_W)
    xbc_buf[SUBLANES:SUBLANES + TOK, :] = xbc_dt[:, :CONV_DIM]
    dt_raw = xbc_dt[:, CONV_DIM:CONV_DIM + LANES]
    emit(early_fill, 5)

    conv = convb_ref[...] + xbc_buf[SUBLANES:SUBLANES + TOK, :] * convw_ref[M_CONV - 1:M_CONV, :]
    for kk in range(M_CONV - 1):
        r0 = SUBLANES - (M_CONV - 1) + kk
        conv = conv + xbc_buf[r0:r0 + TOK, :] * convw_ref[kk:kk + 1, :]
    xbc_buf[0:SUBLANES, :] = xbc_buf[TOK:TOK + SUBLANES, :]
    xa = _silu(conv)
    xm = xa[:, :M_INNER]
    bmat = xa[:, M_INNER:M_INNER + LANES]
    cmat = xa[:, M_INNER + LANES:CONV_DIM]
    xm_b = xm.astype(bf16)

    lane = lax.broadcasted_iota(jnp.int32, (1, LANES), 1)
    a_neg = jnp.where(lane < M_HEADS, -jnp.exp(alog_ref[...]), 0.0)
    dt = _softplus(dt_raw + dtb_ref[...])
    a = dt * (a_neg * LOG2_E)
    a_split = jnp.concatenate(_split3(a), axis=1)
    a_parts = jnp.concatenate([_dot(tril_ref[...], a_split[r0:r0 + MXU_W, :]) for r0 in range(0, TOK, MXU_W)],
                              axis=0)
    a_cs = a_parts[:, :LANES] + a_parts[:, LANES:2 * LANES] + a_parts[:, 2 * LANES:]
    emit(early_fill, len(early_fill))

    ii = lax.broadcasted_iota(jnp.int32, (CHUNK, CHUNK), 0)
    jj = lax.broadcasted_iota(jnp.int32, (CHUNK, CHUNK), 1)
    causal = ii >= jj
    left_head = jj < M_HEADDIM

    def blockdiag(slab):
        zero = jnp.zeros_like(slab)
        return jnp.concatenate([jnp.where(left_head, slab, zero), jnp.where(left_head, zero, slab)], axis=0)

    def ssd_chunk(c):
        r0 = c * CHUNK
        acs_c = a_cs[r0:r0 + CHUNK, :]
        acs_t = acs_c.T[:M_HEADS, :]
        dt_t = dt[r0:r0 + CHUNK, :].T[:M_HEADS, :]
        w_t = jnp.exp2(acs_t[:, CHUNK - 1:CHUNK] - acs_t) * dt_t
        src_t = acs_t - jnp.log2(dt_t)
        b_t = bmat[r0:r0 + CHUNK, :].T
        c_c = cmat[r0:r0 + CHUNK, :].astype(bf16)
        cb2 = lax.dot_general(c_c, blockdiag(bmat[r0:r0 + CHUNK, :].astype(bf16)), NT_DIMS,
                              preferred_element_type=f32)
        s_prev = [s_state[g] for g in range(M_GROUPS)]
        y_off = [_dot(c_c[:, g * M_STATE:(g + 1) * M_STATE], s_prev[g].astype(bf16))
                 for g in range(M_GROUPS)]
        return acs_c, src_t, w_t, b_t, [cb2[:, :CHUNK], cb2[:, CHUNK:]], s_prev, y_off

    def ssd_pair(c, pair, chunk):
        acs_c, src_t, w_t, b_t, cb, s_prev, y_off = chunk
        r0 = c * CHUNK
        g = pair // (HEADS_PER_GROUP // 2)
        gslab = slice((pair % (HEADS_PER_GROUP // 2)) * LANES, (pair % (HEADS_PER_GROUP // 2) + 1) * LANES)
        pcols = slice(pair * LANES, (pair + 1) * LANES)
        heads = (2 * pair, 2 * pair + 1)
        cols = [jnp.broadcast_to(acs_c[:, h:h + 1], (CHUNK, CHUNK)) for h in heads]
        w_rows = [(cb[g] * jnp.exp2(jnp.where(causal, col - src_t[h:h + 1, :], -jnp.inf))).astype(bf16)
                  for h, col in zip(heads, cols)]
        bt_g = b_t[g * M_STATE:(g + 1) * M_STATE, :]
        st_rows = [(bt_g * w_t[h:h + 1, :]).astype(bf16) for h in heads]
        lhs = jnp.concatenate([jnp.concatenate(w_rows, axis=1), jnp.concatenate(st_rows, axis=1)], axis=0)
        res = _dot(lhs, blockdiag(xm_b[r0:r0 + CHUNK, pcols]))
        decay = jnp.exp2(jnp.where(left_head, cols[0], cols[1]))
        y_buf[r0:r0 + CHUNK, pcols] = res[:CHUNK] + y_off[g][:, gslab] * decay
        s_state[g, :, gslab] = s_prev[g][:, gslab] * decay[CHUNK - 1:CHUNK, :] + res[CHUNK:]

    cos = jnp.concatenate([cos_ref[...]] * (R_QK // LANES), axis=1)
    sin = jnp.concatenate([sin_ref[...]] * (R_QK // LANES), axis=1)
    lane_q = lax.broadcasted_iota(jnp.int32, (TOK, R_QK), 1)
    first_half = (lane_q % R_HEAD_QK) < (R_HEAD_QK // 2)

    def rotary(t):
        swapped = jnp.where(first_half, pltpu.roll(t, R_QK - R_HEAD_QK // 2, 1),
                            pltpu.roll(t, R_HEAD_QK // 2, 1))
        return t * cos + swapped * sin

    q = rotary(q_buf[...])
    k = rotary(k_buf[...]) * (R_HEAD_QK ** -0.5)
    qdec = jnp.concatenate([qdec_ref[...]] * N_CHUNKS, axis=0)
    kdec = jnp.concatenate([kdec_ref[...]] * N_CHUNKS, axis=0)
    q_b = q.astype(bf16)
    k_b = k.astype(bf16)
    qw_b = (q * qdec).astype(bf16)
    kw = k * kdec
    s_b = {}

    def score_pair(c, pair):
        r0 = c * CHUNK
        pcols = slice(pair * LANES, (pair + 1) * LANES)
        s_pair = (lax.dot_general(q_b[r0:r0 + CHUNK, pcols], blockdiag(k_b[r0:r0 + CHUNK, pcols]), NT_DIMS,
                                  preferred_element_type=f32) * dmat_ref[pair]).astype(bf16)
        s_b[c, 2 * pair], s_b[c, 2 * pair + 1] = s_pair[:, :CHUNK], s_pair[:, CHUNK:]

    def retention_head(c, h, kw_t):
        r0 = c * CHUNK
        qk_cols = slice(h * R_HEAD_QK, (h + 1) * R_HEAD_QK)
        v_cols = slice(h * R_HEAD_V, (h + 1) * R_HEAD_V)
        r_prev = r_state[h]
        lhs = jnp.concatenate(
            [jnp.concatenate([s_b[c, h], qw_b[r0:r0 + CHUNK, qk_cols]], axis=1),
             jnp.concatenate([kw_t[qk_cols, :], jnp.zeros((R_HEAD_QK, R_HEAD_QK), bf16)], axis=1)], axis=0)
        res = _dot(lhs, jnp.concatenate([v_buf[r0:r0 + CHUNK, v_cols], r_prev.astype(bf16)], axis=0))
        o_h = res[:CHUNK]
        r_state[h] = r_prev * rdec_ref[h] + res[CHUNK:]
        o_buf[r0:r0 + CHUNK, v_cols] = o_h * lax.rsqrt(jnp.mean(o_h * o_h, axis=-1, keepdims=True) + EPS)

    for pair in range(R_HEADS // 2):
        score_pair(0, pair)
        emit(loop_fill)
    n_iter, n_fill = N_CHUNKS * R_HEADS, len(loop_fill)
    for c in range(N_CHUNKS):
        chunk = ssd_chunk(c)
        kw_t = kw[c * CHUNK:(c + 1) * CHUNK, :].T.astype(bf16)
        for i in range(R_HEADS):
            ssd_pair(c, i, chunk)
            if c + 1 < N_CHUNKS and i < R_HEADS // 2:
                score_pair(c + 1, i)
            retention_head(c, i, kw_t)
            it = c * R_HEADS + i
            emit(loop_fill, (it + 1) * n_fill // n_iter - it * n_fill // n_iter)
    assert not loop_fill and M_HEADS // 2 == R_HEADS

    y_r = (_silu(gr_buf[...]) * o_buf[...]).astype(bf16)
    u_r = _dot(y_r, wpr_ref[...])

    y = y_buf[...] + dskip_ref[...] * xm
    yz = y * _silu(z_buf[...])
    y_m = []
    for g in range(M_GROUPS):
        yz_g = yz[:, g * GROUP_W:(g + 1) * GROUP_W]
        y_m.append(yz_g * lax.rsqrt(jnp.mean(yz_g * yz_g, axis=-1, keepdims=True) + EPS))
    y_m = (jnp.concatenate(y_m, axis=1) * mnw_ref[...]).astype(bf16)
    u_m = _dot(y_m, wpm_ref[...])

    merged = _sigmoid(gam_buf[...]) * u_m + _sigmoid(gar_buf[...]) * u_r
    out = _dot(merged.astype(bf16), wout_ref[...])
    xo = x + gate * out
    o_ref[...] = xo * lax.rsqrt(jnp.mean(xo * xo, axis=-1, keepdims=True) + EPS) * fnw_ref[...]


def _resident(shape):
    zeros = (0,) * len(shape)
    return pl.BlockSpec(shape, lambda b, s: zeros, pipeline_mode=pl.Buffered(1))


def _tables(seq_len):
    half = R_HEAD_QK // 2
    pos = np.arange(seq_len, dtype=np.float64)
    inv = ROPE_BASE ** (-np.arange(half, dtype=np.float64) / half)
    ang = pos[:, None] * inv[None, :]
    cos_t = np.tile(np.cos(ang), (1, LANES // half))
    sin_t = np.tile(np.concatenate([-np.sin(ang), np.sin(ang)], axis=1), (1, LANES // R_HEAD_QK))
    log_g = np.log1p(-np.exp2(-5.0 - np.arange(R_HEADS, dtype=np.float64)))
    idx = np.arange(CHUNK, dtype=np.float64)
    rel = idx[:, None] - idx[None, :]
    dmat = np.where(rel[None] >= 0, np.exp(np.minimum(rel[None], CHUNK) * log_g[:, None, None]), 0.0)
    dmat = np.concatenate([dmat[0::2], dmat[1::2]], axis=2)
    qdec = np.repeat(np.exp((idx + 1)[:, None] * log_g[None, :]), R_HEAD_QK, axis=1)
    kdec = np.repeat(np.exp((CHUNK - 1 - idx)[:, None] * log_g[None, :]), R_HEAD_QK, axis=1)
    rdec = np.broadcast_to(np.exp(CHUNK * log_g)[:, None, None], (R_HEADS, 1, R_HEAD_V))
    t = np.arange(MXU_W)
    tril = (t[:, None] >= t[None, :]) & (t[:, None] // CHUNK == t[None, :] // CHUNK)
    as_f32 = lambda a: jnp.asarray(np.ascontiguousarray(a, dtype=np.float32))
    as_bf16 = lambda a: jnp.asarray(np.ascontiguousarray(a, dtype=np.float32), dtype=bf16)
    return (as_f32(cos_t), as_f32(sin_t), as_f32(dmat), as_f32(qdec), as_f32(kdec), as_f32(rdec),
            as_bf16(tril))


def _relayout_kernel(wt_ref, o_ref):
    w = wt_ref[...].T
    col = lax.broadcasted_iota(jnp.int32, w.shape, 1)
    is_dt_tile = pl.program_id(0) == OFF_DT // MXU_W
    o_ref[...] = jnp.where(jnp.logical_and(is_dt_tile, col >= M_HEADS), 0.0, w).astype(bf16)


def _relayout_row_start(j):
    tile, skip = MXU_W // SUBLANES, (MXU_W - M_HEADS) // SUBLANES
    return jnp.where(j <= OFF_DT // MXU_W, j * tile, j * tile - skip) * SUBLANES


def _layer(x, c, w_ada, b_ada, norm_w, w_in_t, conv_w, conv_b, dt_bias, a_log, d_skip,
           m_norm_w, w_proj_m, w_proj_r, w_out, out_norm_w):
    bsz, seq_len, d = x.shape
    assert d == D_MODEL and seq_len % TOK == 0

    mod = pl.pallas_call(
        _mod_kernel,
        grid=(3,),
        in_specs=[pl.BlockSpec((bsz, d), lambda j: (0, 0)),
                  pl.BlockSpec((d, d), lambda j: (0, j)),
                  pl.BlockSpec((1, d), lambda j: (0, j))],
        out_specs=pl.BlockSpec((bsz, d), lambda j: (0, j)),
        out_shape=jax.ShapeDtypeStruct((bsz, 3 * d), f32),
        name="adaln_mod",
    )(c, w_ada, b_ada.reshape(1, 3 * d))

    assert w_in_t.shape == (W_COLS - MXU_W + M_HEADS, d)
    w_cat = pl.pallas_call(
        _relayout_kernel,
        grid=(W_COLS // MXU_W,),
        in_specs=[pl.BlockSpec((pl.Element(MXU_W), pl.Element(d)), lambda j: (_relayout_row_start(j), 0))],
        out_specs=pl.BlockSpec((d, MXU_W), lambda j: (0, j)),
        out_shape=jax.ShapeDtypeStruct((d, W_COLS), bf16),
        name="w_in_relayout",
    )(w_in_t)
    pad_heads = lambda p: jnp.pad(p.reshape(1, M_HEADS), ((0, 0), (0, LANES - M_HEADS)))
    dskip_e = jnp.repeat(d_skip, M_HEADDIM).reshape(1, M_INNER)
    cos_t, sin_t, dmat, qdec, kdec, rdec, tril = _tables(seq_len)

    tok_block = lambda w: pl.BlockSpec((TOK, w), lambda b, s: (s, 0))
    return pl.pallas_call(
        _block_kernel,
        grid=(bsz, seq_len // TOK),
        in_specs=[
            pl.BlockSpec((None, TOK, d), lambda b, s: (b, s, 0)),
            _resident((bsz, 3 * d)),
            _resident((1, d)),
            _resident((d, W_COLS)),
            _resident((M_CONV, CONV_DIM)), _resident((1, CONV_DIM)),
            _resident((1, LANES)), _resident((1, LANES)),
            _resident((1, M_INNER)), _resident((1, M_INNER)),
            _resident((M_INNER, d)), _resident((R_V, d)), _resident((d, d)),
            _resident((1, d)),
            tok_block(LANES), tok_block(LANES),
            _resident((R_HEADS // 2, CHUNK, 2 * CHUNK)),
            _resident((CHUNK, R_QK)), _resident((CHUNK, R_QK)),
            _resident((R_HEADS, 1, R_HEAD_V)),
            _resident((MXU_W, MXU_W)),
        ],
        out_specs=pl.BlockSpec((None, TOK, d), lambda b, s: (b, s, 0)),
        out_shape=jax.ShapeDtypeStruct((bsz, seq_len, d), x.dtype),
        scratch_shapes=[
            pltpu.VMEM((TOK + 2 * SUBLANES, CONV_DIM), f32),
            pltpu.VMEM((M_GROUPS, M_STATE, GROUP_W), f32),
            pltpu.VMEM((R_HEADS, R_HEAD_QK, R_HEAD_V), f32),
            pltpu.VMEM((TOK, M_INNER), f32),
            pltpu.VMEM((TOK, R_V), f32),
            pltpu.VMEM((TOK, M_INNER), f32),
            pltpu.VMEM((TOK, R_V), f32),
            pltpu.VMEM((TOK, D_MODEL), f32),
            pltpu.VMEM((TOK, D_MODEL), f32),
            pltpu.VMEM((TOK, R_QK), f32),
            pltpu.VMEM((TOK, R_QK), f32),
            pltpu.VMEM((TOK, R_V), bf16),
        ],
        compiler_params=pltpu.CompilerParams(
            dimension_semantics=("parallel", "arbitrary"),
            vmem_limit_bytes=VMEM_LIMIT_BYTES),
        name="hybrid_block",
    )(x, mod, norm_w.reshape(1, d), w_cat, conv_w, conv_b.reshape(1, CONV_DIM),
      pad_heads(dt_bias), pad_heads(a_log), dskip_e, m_norm_w.reshape(1, M_INNER),
      w_proj_m.astype(bf16), w_proj_r.astype(bf16), w_out.astype(bf16), out_norm_w.reshape(1, d),
      cos_t, sin_t, dmat, qdec, kdec, rdec, tril)


@jax.jit
def kernel(x, c, w_ada, b_ada, norm_w, w_in, conv_w, conv_b, dt_bias, a_log, d_skip, m_norm_w,
           w_proj_m, w_proj_r, w_out, final_norm_w):
    assert w_ada.shape[0] == 1
    return _layer(x, c, w_ada[0], b_ada[0], norm_w[0], jnp.swapaxes(w_in, 1, 2)[0], conv_w[0], conv_b[0], dt_bias[0],
                  a_log[0], d_skip[0], m_norm_w[0], w_proj_m[0], w_proj_r[0], w_out[0], final_norm_w)
```

```python
import functools

import jax
import jax.numpy as jnp
import numpy as np
from jax import lax
from jax.experimental import pallas as pl
from jax.experimental.pallas import tpu as pltpu

f32 = jnp.float32
bf16 = jnp.bfloat16

D_MODEL = 1024
M_HEADDIM = 64
M_HEADS = 16
M_GROUPS = 2
M_STATE = 64
M_CONV = 4
M_INNER = D_MODEL
GROUP_W = M_INNER // M_GROUPS
HEADS_PER_GROUP = M_HEADS // M_GROUPS
CONV_DIM = M_INNER + 2 * M_GROUPS * M_STATE
R_HEADS = 8
R_QK = 512
R_HEAD_QK = 64
R_V = 1024
R_HEAD_V = 128
CHUNK = 128
ROPE_BASE = 10000.0
EPS = 1e-6
LOG2_E = 1.4426950408889634

LANES = 128
SUBLANES = 8
MXU_W = 256
TOK = 512
N_CHUNKS = TOK // CHUNK
VMEM_LIMIT_BYTES = 56 * 1024 * 1024

OFF_Z = 0
OFF_XBC = OFF_Z + M_INNER
OFF_DT = OFF_XBC + CONV_DIM
OFF_Q = OFF_DT + MXU_W
OFF_K = OFF_Q + R_QK
OFF_V = OFF_K + R_QK
OFF_GR = OFF_V + R_V
OFF_GAM = OFF_GR + R_V
OFF_GAR = OFF_GAM + D_MODEL
W_COLS = OFF_GAR + D_MODEL

NT_DIMS = (((1,), (1,)), ((), ()))


def _mod_kernel(c_ref, w_ref, b_ref, o_ref):
    o_ref[...] = jnp.dot(c_ref[...], w_ref[...], preferred_element_type=f32,
                         precision=lax.Precision.HIGHEST) + b_ref[...]


def _sigmoid(v):
    return 1.0 / (1.0 + jnp.exp(-v))


def _silu(v):
    return v * _sigmoid(v)


def _softplus(v):
    return jnp.maximum(v, 0.0) + jnp.log1p(jnp.exp(-jnp.abs(v)))


def _dot(a, b):
    return jnp.dot(a, b, preferred_element_type=f32)


def _split3(v):
    hi = v.astype(bf16)
    r1 = v - hi.astype(f32)
    mid = r1.astype(bf16)
    lo = (r1 - mid.astype(f32)).astype(bf16)
    return hi, mid, lo


def _load_w_in(wt_hbm, win_ref, stage, sems):
    dt_tile = OFF_DT // MXU_W

    def slab_copy(j):
        row0 = j * MXU_W if j <= dt_tile else j * MXU_W - (MXU_W - M_HEADS)
        return pltpu.make_async_copy(wt_hbm.at[pl.ds(row0, MXU_W), :], stage.at[j % 2], sems.at[j % 2])

    n_tiles = W_COLS // MXU_W
    slab_copy(0).start()
    for j in range(n_tiles):
        if j + 1 < n_tiles:
            slab_copy(j + 1).start()
        slab_copy(j).wait()
        w = stage[j % 2].T
        if j == dt_tile:
            w = jnp.where(lax.broadcasted_iota(jnp.int32, w.shape, 1) < M_HEADS, w, 0.0)
        win_ref[:, j * MXU_W:(j + 1) * MXU_W] = w.astype(bf16)


def _block_kernel(x_ref, mod_ref, normw_ref, wt_hbm, convw_ref, convb_ref,
                  dtb_ref, alog_ref, dskip_ref, mnw_ref, wpm_ref, wpr_ref, wout_ref, fnw_ref,
                  cos_ref, sin_ref, dmat_ref, qdec_ref, kdec_ref, rdec_ref, tril_ref,
                  o_ref,
                  xbc_buf, s_state, r_state, y_buf, o_buf, z_buf, gr_buf, gam_buf, gar_buf,
                  q_buf, k_buf, v_buf, win_ref, w_stage, w_sems):
    @pl.when(jnp.logical_and(pl.program_id(0) == 0, pl.program_id(1) == 0))
    def _first_step():
        _load_w_in(wt_hbm, win_ref, w_stage, w_sems)

    @pl.when(pl.program_id(1) == 0)
    def _start_of_sequence():
        xbc_buf[0:SUBLANES, :] = jnp.zeros((SUBLANES, CONV_DIM), f32)
        s_state[...] = jnp.zeros_like(s_state)
        r_state[...] = jnp.zeros_like(r_state)

    x = x_ref[...]
    rs = lax.rsqrt(jnp.mean(x * x, axis=-1, keepdims=True) + EPS)
    mod = mod_ref[pl.ds(pl.program_id(0), 1), :]
    shift, scale, gate = (mod[:, i * D_MODEL:(i + 1) * D_MODEL] for i in range(3))
    g1 = normw_ref[...] * (1.0 + scale)
    hb = (x * rs * g1 + shift).astype(bf16)

    def proj(off, n):
        return _dot(hb, win_ref[:, off:off + n])

    def proj_tiles(dst, off):
        def tile(c0):
            dst[:, c0:c0 + MXU_W] = proj(off + c0, MXU_W).astype(dst.dtype)
        return [functools.partial(tile, c0) for c0 in range(0, dst.shape[1], MXU_W)]

    def emit(queue, n=1):
        for _ in range(min(n, len(queue))):
            queue.pop(0)()

    early_fill = proj_tiles(q_buf, OFF_Q) + proj_tiles(k_buf, OFF_K) + proj_tiles(v_buf, OFF_V)
    loop_fill = (proj_tiles(z_buf, OFF_Z) + proj_tiles(gr_buf, OFF_GR)
                 + proj_tiles(gam_buf, OFF_GAM) + proj_tiles(gar_buf, OFF_GAR))

    xbc_dt = proj(OFF_XBC, CONV_DIM + MXU_W)
    xbc_buf[SUBLANES:SUBLANES + TOK, :] = xbc_dt[:, :CONV_DIM]
    dt_raw = xbc_dt[:, CONV_DIM:CONV_DIM + LANES]
    emit(early_fill, 5)

    conv = convb_ref[...] + xbc_buf[SUBLANES:SUBLANES + TOK, :] * convw_ref[M_CONV - 1:M_CONV, :]
    for kk in range(M_CONV - 1):
        r0 = SUBLANES - (M_CONV - 1) + kk
        conv = conv + xbc_buf[r0:r0 + TOK, :] * convw_ref[kk:kk + 1, :]
    xbc_buf[0:SUBLANES, :] = xbc_buf[TOK:TOK + SUBLANES, :]
    xa = _silu(conv)
    xm = xa[:, :M_INNER]
    bmat = xa[:, M_INNER:M_INNER + LANES]
    cmat = xa[:, M_INNER + LANES:CONV_DIM]
    xm_b = xm.astype(bf16)

    lane = lax.broadcasted_iota(jnp.int32, (1, LANES), 1)
    a_neg = jnp.where(lane < M_HEADS, -jnp.exp(alog_ref[...]), 0.0)
    dt = _softplus(dt_raw + dtb_ref[...])
    a = dt * (a_neg * LOG2_E)
    a_split = jnp.concatenate(_split3(a), axis=1)
    a_parts = jnp.concatenate([_dot(tril_ref[...], a_split[r0:r0 + MXU_W, :]) for r0 in range(0, TOK, MXU_W)],
                              axis=0)
    a_cs = a_parts[:, :LANES] + a_parts[:, LANES:2 * LANES] + a_parts[:, 2 * LANES:]
    emit(early_fill, len(early_fill))

    ii = lax.broadcasted_iota(jnp.int32, (CHUNK, CHUNK), 0)
    jj = lax.broadcasted_iota(jnp.int32, (CHUNK, CHUNK), 1)
    causal = ii >= jj
    left_head = jj < M_HEADDIM

    def blockdiag(slab):
        zero = jnp.zeros_like(slab)
        return jnp.concatenate([jnp.where(left_head, slab, zero), jnp.where(left_head, zero, slab)], axis=0)

    def ssd_chunk(c):
        r0 = c * CHUNK
        acs_c = a_cs[r0:r0 + CHUNK, :]
        acs_t = acs_c.T[:M_HEADS, :]
        dt_t = dt[r0:r0 + CHUNK, :].T[:M_HEADS, :]
        w_t = jnp.exp2(acs_t[:, CHUNK - 1:CHUNK] - acs_t) * dt_t
        src_t = acs_t - jnp.log2(dt_t)
        b_t = bmat[r0:r0 + CHUNK, :].T
        c_c = cmat[r0:r0 + CHUNK, :].astype(bf16)
        cb2 = lax.dot_general(c_c, blockdiag(bmat[r0:r0 + CHUNK, :].astype(bf16)), NT_DIMS,
                              preferred_element_type=f32)
        s_prev = [s_state[g] for g in range(M_GROUPS)]
        y_off = [_dot(c_c[:, g * M_STATE:(g + 1) * M_STATE], s_prev[g].astype(bf16))
                 for g in range(M_GROUPS)]
        return acs_c, src_t, w_t, b_t, [cb2[:, :CHUNK], cb2[:, CHUNK:]], s_prev, y_off

    def ssd_pair(c, pair, chunk):
        acs_c, src_t, w_t, b_t, cb, s_prev, y_off = chunk
        r0 = c * CHUNK
        g = pair // (HEADS_PER_GROUP // 2)
        gslab = slice((pair % (HEADS_PER_GROUP // 2)) * LANES, (pair % (HEADS_PER_GROUP // 2) + 1) * LANES)
        pcols = slice(pair * LANES, (pair + 1) * LANES)
        heads = (2 * pair, 2 * pair + 1)
        cols = [jnp.broadcast_to(acs_c[:, h:h + 1], (CHUNK, CHUNK)) for h in heads]
        w_rows = [(cb[g] * jnp.exp2(jnp.where(causal, col - src_t[h:h + 1, :], -jnp.inf))).astype(bf16)
                  for h, col in zip(heads, cols)]
        bt_g = b_t[g * M_STATE:(g + 1) * M_STATE, :]
        st_rows = [(bt_g * w_t[h:h + 1, :]).astype(bf16) for h in heads]
        lhs = jnp.concatenate([jnp.concatenate(w_rows, axis=1), jnp.concatenate(st_rows, axis=1)], axis=0)
        res = _dot(lhs, blockdiag(xm_b[r0:r0 + CHUNK, pcols]))
        decay = jnp.exp2(jnp.where(left_head, cols[0], cols[1]))
        y_buf[r0:r0 + CHUNK, pcols] = res[:CHUNK] + y_off[g][:, gslab] * decay
        s_state[g, :, gslab] = s_prev[g][:, gslab] * decay[CHUNK - 1:CHUNK, :] + res[CHUNK:]

    cos = jnp.concatenate([cos_ref[...]] * (R_QK // LANES), axis=1)
    sin = jnp.concatenate([sin_ref[...]] * (R_QK // LANES), axis=1)
    lane_q = lax.broadcasted_iota(jnp.int32, (TOK, R_QK), 1)
    first_half = (lane_q % R_HEAD_QK) < (R_HEAD_QK // 2)

    def rotary(t):
        swapped = jnp.where(first_half, pltpu.roll(t, R_QK - R_HEAD_QK // 2, 1),
                            pltpu.roll(t, R_HEAD_QK // 2, 1))
        return t * cos + swapped * sin

    q = rotary(q_buf[...])
    k = rotary(k_buf[...]) * (R_HEAD_QK ** -0.5)
    qdec = jnp.concatenate([qdec_ref[...]] * N_CHUNKS, axis=0)
    kdec = jnp.concatenate([kdec_ref[...]] * N_CHUNKS, axis=0)
    q_b = q.astype(bf16)
    k_b = k.astype(bf16)
    qw_b = (q * qdec).astype(bf16)
    kw = k * kdec
    s_b = {}

    def score_pair(c, pair):
        r0 = c * CHUNK
        pcols = slice(pair * LANES, (pair + 1) * LANES)
        s_pair = (lax.dot_general(q_b[r0:r0 + CHUNK, pcols], blockdiag(k_b[r0:r0 + CHUNK, pcols]), NT_DIMS,
                                  preferred_element_type=f32) * dmat_ref[pair]).astype(bf16)
        s_b[c, 2 * pair], s_b[c, 2 * pair + 1] = s_pair[:, :CHUNK], s_pair[:, CHUNK:]

    def retention_head(c, h, kw_t):
        r0 = c * CHUNK
        qk_cols = slice(h * R_HEAD_QK, (h + 1) * R_HEAD_QK)
        v_cols = slice(h * R_HEAD_V, (h + 1) * R_HEAD_V)
        r_prev = r_state[h]
        lhs = jnp.concatenate(
            [jnp.concatenate([s_b[c, h], qw_b[r0:r0 + CHUNK, qk_cols]], axis=1),
             jnp.concatenate([kw_t[qk_cols, :], jnp.zeros((R_HEAD_QK, R_HEAD_QK), bf16)], axis=1)], axis=0)
        res = _dot(lhs, jnp.concatenate([v_buf[r0:r0 + CHUNK, v_cols], r_prev.astype(bf16)], axis=0))
        o_h = res[:CHUNK]
        r_state[h] = r_prev * rdec_ref[h] + res[CHUNK:]
        o_buf[r0:r0 + CHUNK, v_cols] = o_h * lax.rsqrt(jnp.mean(o_h * o_h, axis=-1, keepdims=True) + EPS)

    for pair in range(R_HEADS // 2):
        score_pair(0, pair)
        emit(loop_fill)
    n_iter, n_fill = N_CHUNKS * R_HEADS, len(loop_fill)
    for c in range(N_CHUNKS):
        chunk = ssd_chunk(c)
        kw_t = kw[c * CHUNK:(c + 1) * CHUNK, :].T.astype(bf16)
        for i in range(R_HEADS):
            ssd_pair(c, i, chunk)
            if c + 1 < N_CHUNKS and i < R_HEADS // 2:
                score_pair(c + 1, i)
            retention_head(c, i, kw_t)
            it = c * R_HEADS + i
            emit(loop_fill, (it + 1) * n_fill // n_iter - it * n_fill // n_iter)
    assert not loop_fill and M_HEADS // 2 == R_HEADS

    y_r = (_silu(gr_buf[...]) * o_buf[...]).astype(bf16)
    u_r = _dot(y_r, wpr_ref[...])

    y = y_buf[...] + dskip_ref[...] * xm
    yz = y * _silu(z_buf[...])
    y_m = []
    for g in range(M_GROUPS):
        yz_g = yz[:, g * GROUP_W:(g + 1) * GROUP_W]
        y_m.append(yz_g * lax.rsqrt(jnp.mean(yz_g * yz_g, axis=-1, keepdims=True) + EPS))
    y_m = (jnp.concatenate(y_m, axis=1) * mnw_ref[...]).astype(bf16)
    u_m = _dot(y_m, wpm_ref[...])

    merged = _sigmoid(gam_buf[...]) * u_m + _sigmoid(gar_buf[...]) * u_r
    out = _dot(merged.astype(bf16), wout_ref[...])
    xo = x + gate * out
    o_ref[...] = xo * lax.rsqrt(jnp.mean(xo * xo, axis=-1, keepdims=True) + EPS) * fnw_ref[...]


def _resident(shape):
    zeros = (0,) * len(shape)
    return pl.BlockSpec(shape, lambda b, s: zeros, pipeline_mode=pl.Buffered(1))


def _tables(seq_len):
    half = R_HEAD_QK // 2
    pos = np.arange(seq_len, dtype=np.float64)
    inv = ROPE_BASE ** (-np.arange(half, dtype=np.float64) / half)
    ang = pos[:, None] * inv[None, :]
    cos_t = np.tile(np.cos(ang), (1, LANES // half))
    sin_t = np.tile(np.concatenate([-np.sin(ang), np.sin(ang)], axis=1), (1, LANES // R_HEAD_QK))
    log_g = np.log1p(-np.exp2(-5.0 - np.arange(R_HEADS, dtype=np.float64)))
    idx = np.arange(CHUNK, dtype=np.float64)
    rel = idx[:, None] - idx[None, :]
    dmat = np.where(rel[None] >= 0, np.exp(np.minimum(rel[None], CHUNK) * log_g[:, None, None]), 0.0)
    dmat = np.concatenate([dmat[0::2], dmat[1::2]], axis=2)
    qdec = np.repeat(np.exp((idx + 1)[:, None] * log_g[None, :]), R_HEAD_QK, axis=1)
    kdec = np.repeat(np.exp((CHUNK - 1 - idx)[:, None] * log_g[None, :]), R_HEAD_QK, axis=1)
    rdec = np.broadcast_to(np.exp(CHUNK * log_g)[:, None, None], (R_HEADS, 1, R_HEAD_V))
    t = np.arange(MXU_W)
    tril = (t[:, None] >= t[None, :]) & (t[:, None] // CHUNK == t[None, :] // CHUNK)
    as_f32 = lambda a: jnp.asarray(np.ascontiguousarray(a, dtype=np.float32))
    as_bf16 = lambda a: jnp.asarray(np.ascontiguousarray(a, dtype=np.float32), dtype=bf16)
    return (as_f32(cos_t), as_f32(sin_t), as_f32(dmat), as_f32(qdec), as_f32(kdec), as_f32(rdec),
            as_bf16(tril))


def _layer(x, c, w_ada, b_ada, norm_w, w_in_t, conv_w, conv_b, dt_bias, a_log, d_skip,
           m_norm_w, w_proj_m, w_proj_r, w_out, out_norm_w):
    bsz, seq_len, d = x.shape
    assert d == D_MODEL and seq_len % TOK == 0

    mod = pl.pallas_call(
        _mod_kernel,
        grid=(3,),
        in_specs=[pl.BlockSpec((bsz, d), lambda j: (0, 0)),
                  pl.BlockSpec((d, d), lambda j: (0, j)),
                  pl.BlockSpec((1, d), lambda j: (0, j))],
        out_specs=pl.BlockSpec((bsz, d), lambda j: (0, j)),
        out_shape=jax.ShapeDtypeStruct((bsz, 3 * d), f32),
        name="adaln_mod",
    )(c, w_ada, b_ada.reshape(1, 3 * d))

    assert w_in_t.shape == (W_COLS - MXU_W + M_HEADS, d)
    pad_heads = lambda p: jnp.pad(p.reshape(1, M_HEADS), ((0, 0), (0, LANES - M_HEADS)))
    dskip_e = jnp.repeat(d_skip, M_HEADDIM).reshape(1, M_INNER)
    cos_t, sin_t, dmat, qdec, kdec, rdec, tril = _tables(seq_len)

    tok_block = lambda w: pl.BlockSpec((TOK, w), lambda b, s: (s, 0))
    return pl.pallas_call(
        _block_kernel,
        grid=(bsz, seq_len // TOK),
        in_specs=[
            pl.BlockSpec((None, TOK, d), lambda b, s: (b, s, 0)),
            _resident((bsz, 3 * d)),
            _resident((1, d)),
            pl.BlockSpec(memory_space=pl.ANY),
            _resident((M_CONV, CONV_DIM)), _resident((1, CONV_DIM)),
            _resident((1, LANES)), _resident((1, LANES)),
            _resident((1, M_INNER)), _resident((1, M_INNER)),
            _resident((M_INNER, d)), _resident((R_V, d)), _resident((d, d)),
            _resident((1, d)),
            tok_block(LANES), tok_block(LANES),
            _resident((R_HEADS // 2, CHUNK, 2 * CHUNK)),
            _resident((CHUNK, R_QK)), _resident((CHUNK, R_QK)),
            _resident((R_HEADS, 1, R_HEAD_V)),
            _resident((MXU_W, MXU_W)),
        ],
        out_specs=pl.BlockSpec((None, TOK, d), lambda b, s: (b, s, 0)),
        out_shape=jax.ShapeDtypeStruct((bsz, seq_len, d), x.dtype),
        scratch_shapes=[
            pltpu.VMEM((TOK + 2 * SUBLANES, CONV_DIM), f32),
            pltpu.VMEM((M_GROUPS, M_STATE, GROUP_W), f32),
            pltpu.VMEM((R_HEADS, R_HEAD_QK, R_HEAD_V), f32),
            pltpu.VMEM((TOK, M_INNER), f32),
            pltpu.VMEM((TOK, R_V), f32),
            pltpu.VMEM((TOK, M_INNER), f32),
            pltpu.VMEM((TOK, R_V), f32),
            pltpu.VMEM((TOK, D_MODEL), f32),
            pltpu.VMEM((TOK, D_MODEL), f32),
            pltpu.VMEM((TOK, R_QK), f32),
            pltpu.VMEM((TOK, R_QK), f32),
            pltpu.VMEM((TOK, R_V), bf16),
            pltpu.VMEM((d, W_COLS), bf16),
            pltpu.VMEM((2, MXU_W, d), f32),
            pltpu.SemaphoreType.DMA((2,)),
        ],
        compiler_params=pltpu.CompilerParams(
            dimension_semantics=("arbitrary", "arbitrary"),
            vmem_limit_bytes=VMEM_LIMIT_BYTES),
        name="hybrid_block",
    )(x, mod, norm_w.reshape(1, d), w_in_t, conv_w, conv_b.reshape(1, CONV_DIM),
      pad_heads(dt_bias), pad_heads(a_log), dskip_e, m_norm_w.reshape(1, M_INNER),
      w_proj_m.astype(bf16), w_proj_r.astype(bf16), w_out.astype(bf16), out_norm_w.reshape(1, d),
      cos_t, sin_t, dmat, qdec, kdec, rdec, tril)


@jax.jit
def kernel(x, c, w_ada, b_ada, norm_w, w_in, conv_w, conv_b, dt_bias, a_log, d_skip, m_norm_w,
           w_proj_m, w_proj_r, w_out, final_norm_w):
    assert w_ada.shape[0] == 1
    return _layer(x, c, w_ada[0], b_ada[0], norm_w[0], jnp.swapaxes(w_in, 1, 2)[0], conv_w[0], conv_b[0], dt_bias[0],
                  a_log[0], d_skip[0], m_norm_w[0], w_proj_m[0], w_proj_r[0], w_out[0], final_norm_w)
```

```python
import functools

import jax
import jax.numpy as jnp
import numpy as np
from jax import lax
from jax.experimental import pallas as pl
from jax.experimental.pallas import tpu as pltpu

f32 = jnp.float32
bf16 = jnp.bfloat16

D_MODEL = 1024
M_HEADDIM = 64
M_HEADS = 16
M_GROUPS = 2
M_STATE = 64
M_CONV = 4
M_INNER = D_MODEL
GROUP_W = M_INNER // M_GROUPS
HEADS_PER_GROUP = M_HEADS // M_GROUPS
CONV_DIM = M_INNER + 2 * M_GROUPS * M_STATE
R_HEADS = 8
R_QK = 512
R_HEAD_QK = 64
R_V = 1024
R_HEAD_V = 128
CHUNK = 128
ROPE_BASE = 10000.0
EPS = 1e-6
LOG2_E = 1.4426950408889634

LANES = 128
SUBLANES = 8
MXU_W = 256
TOK = 512
N_CHUNKS = TOK // CHUNK
W_SLABS_IN_FLIGHT = 4
VMEM_LIMIT_BYTES = 58 * 1024 * 1024

OFF_Z = 0
OFF_XBC = OFF_Z + M_INNER
OFF_DT = OFF_XBC + CONV_DIM
OFF_Q = OFF_DT + MXU_W
OFF_K = OFF_Q + R_QK
OFF_V = OFF_K + R_QK
OFF_GR = OFF_V + R_V
OFF_GAM = OFF_GR + R_V
OFF_GAR = OFF_GAM + D_MODEL
W_COLS = OFF_GAR + D_MODEL

NT_DIMS = (((1,), (1,)), ((), ()))


def _mod_kernel(c_ref, w_ref, b_ref, o_ref):
    o_ref[...] = jnp.dot(c_ref[...], w_ref[...], preferred_element_type=f32,
                         precision=lax.Precision.HIGHEST) + b_ref[...]


def _sigmoid(v):
    return 1.0 / (1.0 + jnp.exp(-v))


def _silu(v):
    return v * _sigmoid(v)


def _softplus(v):
    return jnp.maximum(v, 0.0) + jnp.log1p(jnp.exp(-jnp.abs(v)))


def _dot(a, b):
    return jnp.dot(a, b, preferred_element_type=f32)


def _split3(v):
    hi = v.astype(bf16)
    r1 = v - hi.astype(f32)
    mid = r1.astype(bf16)
    lo = (r1 - mid.astype(f32)).astype(bf16)
    return hi, mid, lo


def _load_weights(wt_hbm, square_hbm, win_ref, square_refs, stage, sems):
    dt_tile = OFF_DT // MXU_W
    jobs = []

    def w_in_tile(w_t, j):
        w = w_t.T
        if j == dt_tile:
            w = jnp.where(lax.broadcasted_iota(jnp.int32, w.shape, 1) < M_HEADS, w, 0.0)
        win_ref[:, j * MXU_W:(j + 1) * MXU_W] = w.astype(bf16)

    def square_rows(w, dst, r0):
        dst[r0:r0 + MXU_W, :] = w.astype(bf16)

    for j in range(W_COLS // MXU_W):
        row0 = j * MXU_W if j <= dt_tile else j * MXU_W - (MXU_W - M_HEADS)
        jobs.append((wt_hbm.at[pl.ds(row0, MXU_W), :], functools.partial(w_in_tile, j=j)))
    for src, dst in zip(square_hbm, square_refs):
        for r0 in range(0, D_MODEL, MXU_W):
            jobs.append((src.at[pl.ds(r0, MXU_W), :], functools.partial(square_rows, dst=dst, r0=r0)))

    def slab_copy(i):
        slot = i % W_SLABS_IN_FLIGHT
        return pltpu.make_async_copy(jobs[i][0], stage.at[slot], sems.at[slot])

    for i in range(W_SLABS_IN_FLIGHT - 1):
        slab_copy(i).start()
    for i in range(len(jobs)):
        if i + W_SLABS_IN_FLIGHT - 1 < len(jobs):
            slab_copy(i + W_SLABS_IN_FLIGHT - 1).start()
        slab_copy(i).wait()
        jobs[i][1](stage[i % W_SLABS_IN_FLIGHT])


def _block_kernel(x_ref, mod_ref, normw_ref, wt_hbm, convw_ref, convb_ref,
                  dtb_ref, alog_ref, dskip_ref, mnw_ref, wpm_hbm, wpr_hbm, wout_hbm, fnw_ref,
                  cos_ref, sin_ref, dmat_ref, qdec_ref, kdec_ref, rdec_ref, tril_ref,
                  o_ref,
                  xbc_buf, s_state, r_state, y_buf, o_buf, z_buf, gr_buf, gam_buf, gar_buf,
                  q_buf, k_buf, v_buf, win_ref, wpm_ref, wpr_ref, wout_ref, w_stage, w_sems):
    @pl.when(jnp.logical_and(pl.program_id(0) == 0, pl.program_id(1) == 0))
    def _first_step():
        _load_weights(wt_hbm, (wpm_hbm, wpr_hbm, wout_hbm), win_ref, (wpm_ref, wpr_ref, wout_ref), w_stage, w_sems)

    @pl.when(pl.program_id(1) == 0)
    def _start_of_sequence():
        xbc_buf[0:SUBLANES, :] = jnp.zeros((SUBLANES, CONV_DIM), f32)
        s_state[...] = jnp.zeros_like(s_state)
        r_state[...] = jnp.zeros_like(r_state)

    x = x_ref[...]
    rs = lax.rsqrt(jnp.mean(x * x, axis=-1, keepdims=True) + EPS)
    mod = mod_ref[pl.ds(pl.program_id(0), 1), :]
    shift, scale, gate = (mod[:, i * D_MODEL:(i + 1) * D_MODEL] for i in range(3))
    g1 = normw_ref[...] * (1.0 + scale)
    hb = (x * rs * g1 + shift).astype(bf16)

    def proj(off, n):
        return _dot(hb, win_ref[:, off:off + n])

    def proj_tiles(dst, off):
        def tile(c0):
            dst[:, c0:c0 + MXU_W] = proj(off + c0, MXU_W).astype(dst.dtype)
        return [functools.partial(tile, c0) for c0 in range(0, dst.shape[1], MXU_W)]

    def emit(queue, n=1):
        for _ in range(min(n, len(queue))):
            queue.pop(0)()

    early_fill = proj_tiles(q_buf, OFF_Q) + proj_tiles(k_buf, OFF_K) + proj_tiles(v_buf, OFF_V)
    loop_fill = (proj_tiles(z_buf, OFF_Z) + proj_tiles(gr_buf, OFF_GR)
                 + proj_tiles(gam_buf, OFF_GAM) + proj_tiles(gar_buf, OFF_GAR))

    xbc_dt = proj(OFF_XBC, CONV_DIM + MXU_W)
    xbc_buf[SUBLANES:SUBLANES + TOK, :] = xbc_dt[:, :CONV_DIM]
    dt_raw = xbc_dt[:, CONV_DIM:CONV_DIM + LANES]
    emit(early_fill, 5)

    conv = convb_ref[...] + xbc_buf[SUBLANES:SUBLANES + TOK, :] * convw_ref[M_CONV - 1:M_CONV, :]
    for kk in range(M_CONV - 1):
        r0 = SUBLANES - (M_CONV - 1) + kk
        conv = conv + xbc_buf[r0:r0 + TOK, :] * convw_ref[kk:kk + 1, :]
    xbc_buf[0:SUBLANES, :] = xbc_buf[TOK:TOK + SUBLANES, :]
    xa = _silu(conv)
    xm = xa[:, :M_INNER]
    bmat = xa[:, M_INNER:M_INNER + LANES]
    cmat = xa[:, M_INNER + LANES:CONV_DIM]
    xm_b = xm.astype(bf16)

    lane = lax.broadcasted_iota(jnp.int32, (1, LANES), 1)
    a_neg = jnp.where(lane < M_HEADS, -jnp.exp(alog_ref[...]), 0.0)
    dt = _softplus(dt_raw + dtb_ref[...])
    a = dt * (a_neg * LOG2_E)
    a_split = jnp.concatenate(_split3(a), axis=1)
    a_parts = jnp.concatenate([_dot(tril_ref[...], a_split[r0:r0 + MXU_W, :]) for r0 in range(0, TOK, MXU_W)],
                              axis=0)
    a_cs = a_parts[:, :LANES] + a_parts[:, LANES:2 * LANES] + a_parts[:, 2 * LANES:]
    emit(early_fill, len(early_fill))

    ii = lax.broadcasted_iota(jnp.int32, (CHUNK, CHUNK), 0)
    jj = lax.broadcasted_iota(jnp.int32, (CHUNK, CHUNK), 1)
    causal = ii >= jj
    left_head = jj < M_HEADDIM

    def blockdiag(slab):
        zero = jnp.zeros_like(slab)
        return jnp.concatenate([jnp.where(left_head, slab, zero), jnp.where(left_head, zero, slab)], axis=0)

    def ssd_chunk(c):
        r0 = c * CHUNK
        acs_c = a_cs[r0:r0 + CHUNK, :]
        acs_t = acs_c.T[:M_HEADS, :]
        dt_t = dt[r0:r0 + CHUNK, :].T[:M_HEADS, :]
        w_t = jnp.exp2(acs_t[:, CHUNK - 1:CHUNK] - acs_t) * dt_t
        src_t = acs_t - jnp.log2(dt_t)
        b_t = bmat[r0:r0 + CHUNK, :].T
        c_c = cmat[r0:r0 + CHUNK, :].astype(bf16)
        cb2 = lax.dot_general(c_c, blockdiag(bmat[r0:r0 + CHUNK, :].astype(bf16)), NT_DIMS,
                              preferred_element_type=f32)
        s_prev = [s_state[g] for g in range(M_GROUPS)]
        y_off = [_dot(c_c[:, g * M_STATE:(g + 1) * M_STATE], s_prev[g].astype(bf16))
                 for g in range(M_GROUPS)]
        return acs_c, src_t, w_t, b_t, [cb2[:, :CHUNK], cb2[:, CHUNK:]], s_prev, y_off

    def ssd_pair(c, pair, chunk):
        acs_c, src_t, w_t, b_t, cb, s_prev, y_off = chunk
        r0 = c * CHUNK
        g = pair // (HEADS_PER_GROUP // 2)
        gslab = slice((pair % (HEADS_PER_GROUP // 2)) * LANES, (pair % (HEADS_PER_GROUP // 2) + 1) * LANES)
        pcols = slice(pair * LANES, (pair + 1) * LANES)
        heads = (2 * pair, 2 * pair + 1)
        cols = [jnp.broadcast_to(acs_c[:, h:h + 1], (CHUNK, CHUNK)) for h in heads]
        w_rows = [(cb[g] * jnp.exp2(jnp.where(causal, col - src_t[h:h + 1, :], -jnp.inf))).astype(bf16)
                  for h, col in zip(heads, cols)]
        bt_g = b_t[g * M_STATE:(g + 1) * M_STATE, :]
        st_rows = [(bt_g * w_t[h:h + 1, :]).astype(bf16) for h in heads]
        lhs = jnp.concatenate([jnp.concatenate(w_rows, axis=1), jnp.concatenate(st_rows, axis=1)], axis=0)
        res = _dot(lhs, blockdiag(xm_b[r0:r0 + CHUNK, pcols]))
        decay = jnp.exp2(jnp.where(left_head, cols[0], cols[1]))
        y_buf[r0:r0 + CHUNK, pcols] = res[:CHUNK] + y_off[g][:, gslab] * decay
        s_state[g, :, gslab] = s_prev[g][:, gslab] * decay[CHUNK - 1:CHUNK, :] + res[CHUNK:]

    cos = jnp.concatenate([cos_ref[...]] * (R_QK // LANES), axis=1)
    sin = jnp.concatenate([sin_ref[...]] * (R_QK // LANES), axis=1)
    lane_q = lax.broadcasted_iota(jnp.int32, (TOK, R_QK), 1)
    first_half = (lane_q % R_HEAD_QK) < (R_HEAD_QK // 2)

    def rotary(t):
        swapped = jnp.where(first_half, pltpu.roll(t, R_QK - R_HEAD_QK // 2, 1),
                            pltpu.roll(t, R_HEAD_QK // 2, 1))
        return t * cos + swapped * sin

    q = rotary(q_buf[...])
    k = rotary(k_buf[...]) * (R_HEAD_QK ** -0.5)
    qdec = jnp.concatenate([qdec_ref[...]] * N_CHUNKS, axis=0)
    kdec = jnp.concatenate([kdec_ref[...]] * N_CHUNKS, axis=0)
    q_b = q.astype(bf16)
    k_b = k.astype(bf16)
    qw_b = (q * qdec).astype(bf16)
    kw = k * kdec
    s_b = {}

    def score_pair(c, pair):
        r0 = c * CHUNK
        pcols = slice(pair * LANES, (pair + 1) * LANES)
        s_pair = (lax.dot_general(q_b[r0:r0 + CHUNK, pcols], blockdiag(k_b[r0:r0 + CHUNK, pcols]), NT_DIMS,
                                  preferred_element_type=f32) * dmat_ref[pair]).astype(bf16)
        s_b[c, 2 * pair], s_b[c, 2 * pair + 1] = s_pair[:, :CHUNK], s_pair[:, CHUNK:]

    def retention_head(c, h, kw_t):
        r0 = c * CHUNK
        qk_cols = slice(h * R_HEAD_QK, (h + 1) * R_HEAD_QK)
        v_cols = slice(h * R_HEAD_V, (h + 1) * R_HEAD_V)
        r_prev = r_state[h]
        lhs = jnp.concatenate(
            [jnp.concatenate([s_b[c, h], qw_b[r0:r0 + CHUNK, qk_cols]], axis=1),
             jnp.concatenate([kw_t[qk_cols, :], jnp.zeros((R_HEAD_QK, R_HEAD_QK), bf16)], axis=1)], axis=0)
        res = _dot(lhs, jnp.concatenate([v_buf[r0:r0 + CHUNK, v_cols], r_prev.astype(bf16)], axis=0))
        o_h = res[:CHUNK]
        r_state[h] = r_prev * rdec_ref[h] + res[CHUNK:]
        o_buf[r0:r0 + CHUNK, v_cols] = o_h * lax.rsqrt(jnp.mean(o_h * o_h, axis=-1, keepdims=True) + EPS)

    for pair in range(R_HEADS // 2):
        score_pair(0, pair)
        emit(loop_fill)
    n_iter, n_fill = N_CHUNKS * R_HEADS, len(loop_fill)
    for c in range(N_CHUNKS):
        chunk = ssd_chunk(c)
        kw_t = kw[c * CHUNK:(c + 1) * CHUNK, :].T.astype(bf16)
        for i in range(R_HEADS):
            ssd_pair(c, i, chunk)
            if c + 1 < N_CHUNKS and i < R_HEADS // 2:
                score_pair(c + 1, i)
            retention_head(c, i, kw_t)
            it = c * R_HEADS + i
            emit(loop_fill, (it + 1) * n_fill // n_iter - it * n_fill // n_iter)
    assert not loop_fill and M_HEADS // 2 == R_HEADS

    y_r = (_silu(gr_buf[...]) * o_buf[...]).astype(bf16)
    u_r = _dot(y_r, wpr_ref[...])

    y = y_buf[...] + dskip_ref[...] * xm
    yz = y * _silu(z_buf[...])
    y_m = []
    for g in range(M_GROUPS):
        yz_g = yz[:, g * GROUP_W:(g + 1) * GROUP_W]
        y_m.append(yz_g * lax.rsqrt(jnp.mean(yz_g * yz_g, axis=-1, keepdims=True) + EPS))
    y_m = (jnp.concatenate(y_m, axis=1) * mnw_ref[...]).astype(bf16)
    u_m = _dot(y_m, wpm_ref[...])

    merged = _sigmoid(gam_buf[...]) * u_m + _sigmoid(gar_buf[...]) * u_r
    out = _dot(merged.astype(bf16), wout_ref[...])
    xo = x + gate * out
    o_ref[...] = xo * lax.rsqrt(jnp.mean(xo * xo, axis=-1, keepdims=True) + EPS) * fnw_ref[...]


def _resident(shape):
    zeros = (0,) * len(shape)
    return pl.BlockSpec(shape, lambda b, s: zeros, pipeline_mode=pl.Buffered(1))


def _tables(seq_len):
    half = R_HEAD_QK // 2
    pos = np.arange(seq_len, dtype=np.float64)
    inv = ROPE_BASE ** (-np.arange(half, dtype=np.float64) / half)
    ang = pos[:, None] * inv[None, :]
    cos_t = np.tile(np.cos(ang), (1, LANES // half))
    sin_t = np.tile(np.concatenate([-np.sin(ang), np.sin(ang)], axis=1), (1, LANES // R_HEAD_QK))
    log_g = np.log1p(-np.exp2(-5.0 - np.arange(R_HEADS, dtype=np.float64)))
    idx = np.arange(CHUNK, dtype=np.float64)
    rel = idx[:, None] - idx[None, :]
    dmat = np.where(rel[None] >= 0, np.exp(np.minimum(rel[None], CHUNK) * log_g[:, None, None]), 0.0)
    dmat = np.concatenate([dmat[0::2], dmat[1::2]], axis=2)
    qdec = np.repeat(np.exp((idx + 1)[:, None] * log_g[None, :]), R_HEAD_QK, axis=1)
    kdec = np.repeat(np.exp((CHUNK - 1 - idx)[:, None] * log_g[None, :]), R_HEAD_QK, axis=1)
    rdec = np.broadcast_to(np.exp(CHUNK * log_g)[:, None, None], (R_HEADS, 1, R_HEAD_V))
    t = np.arange(MXU_W)
    tril = (t[:, None] >= t[None, :]) & (t[:, None] // CHUNK == t[None, :] // CHUNK)
    as_f32 = lambda a: jnp.asarray(np.ascontiguousarray(a, dtype=np.float32))
    as_bf16 = lambda a: jnp.asarray(np.ascontiguousarray(a, dtype=np.float32), dtype=bf16)
    return (as_f32(cos_t), as_f32(sin_t), as_f32(dmat), as_f32(qdec), as_f32(kdec), as_f32(rdec),
            as_bf16(tril))


def _layer(x, c, w_ada, b_ada, norm_w, w_in_t, conv_w, conv_b, dt_bias, a_log, d_skip,
           m_norm_w, w_proj_m, w_proj_r, w_out, out_norm_w):
    bsz, seq_len, d = x.shape
    assert d == D_MODEL and seq_len % TOK == 0

    mod = pl.pallas_call(
        _mod_kernel,
        grid=(3,),
        in_specs=[pl.BlockSpec((bsz, d), lambda j: (0, 0)),
                  pl.BlockSpec((d, d), lambda j: (0, j)),
                  pl.BlockSpec((1, d), lambda j: (0, j))],
        out_specs=pl.BlockSpec((bsz, d), lambda j: (0, j)),
        out_shape=jax.ShapeDtypeStruct((bsz, 3 * d), f32),
        name="adaln_mod",
    )(c, w_ada, b_ada.reshape(1, 3 * d))

    assert w_in_t.shape == (W_COLS - MXU_W + M_HEADS, d)
    pad_heads = lambda p: jnp.pad(p.reshape(1, M_HEADS), ((0, 0), (0, LANES - M_HEADS)))
    dskip_e = jnp.repeat(d_skip, M_HEADDIM).reshape(1, M_INNER)
    cos_t, sin_t, dmat, qdec, kdec, rdec, tril = _tables(seq_len)

    tok_block = lambda w: pl.BlockSpec((TOK, w), lambda b, s: (s, 0))
    return pl.pallas_call(
        _block_kernel,
        grid=(bsz, seq_len // TOK),
        in_specs=[
            pl.BlockSpec((None, TOK, d), lambda b, s: (b, s, 0)),
            _resident((bsz, 3 * d)),
            _resident((1, d)),
            pl.BlockSpec(memory_space=pl.ANY),
            _resident((M_CONV, CONV_DIM)), _resident((1, CONV_DIM)),
            _resident((1, LANES)), _resident((1, LANES)),
            _resident((1, M_INNER)), _resident((1, M_INNER)),
            pl.BlockSpec(memory_space=pl.ANY), pl.BlockSpec(memory_space=pl.ANY),
            pl.BlockSpec(memory_space=pl.ANY),
            _resident((1, d)),
            tok_block(LANES), tok_block(LANES),
            _resident((R_HEADS // 2, CHUNK, 2 * CHUNK)),
            _resident((CHUNK, R_QK)), _resident((CHUNK, R_QK)),
            _resident((R_HEADS, 1, R_HEAD_V)),
            _resident((MXU_W, MXU_W)),
        ],
        out_specs=pl.BlockSpec((None, TOK, d), lambda b, s: (b, s, 0)),
        out_shape=jax.ShapeDtypeStruct((bsz, seq_len, d), x.dtype),
        scratch_shapes=[
            pltpu.VMEM((TOK + 2 * SUBLANES, CONV_DIM), f32),
            pltpu.VMEM((M_GROUPS, M_STATE, GROUP_W), f32),
            pltpu.VMEM((R_HEADS, R_HEAD_QK, R_HEAD_V), f32),
            pltpu.VMEM((TOK, M_INNER), f32),
            pltpu.VMEM((TOK, R_V), f32),
            pltpu.VMEM((TOK, M_INNER), f32),
            pltpu.VMEM((TOK, R_V), f32),
            pltpu.VMEM((TOK, D_MODEL), f32),
            pltpu.VMEM((TOK, D_MODEL), f32),
            pltpu.VMEM((TOK, R_QK), f32),
            pltpu.VMEM((TOK, R_QK), f32),
            pltpu.VMEM((TOK, R_V), bf16),
            pltpu.VMEM((d, W_COLS), bf16),
            pltpu.VMEM((M_INNER, d), bf16), pltpu.VMEM((R_V, d), bf16), pltpu.VMEM((d, d), bf16),
            pltpu.VMEM((W_SLABS_IN_FLIGHT, MXU_W, d), f32),
            pltpu.SemaphoreType.DMA((W_SLABS_IN_FLIGHT,)),
        ],
        compiler_params=pltpu.CompilerParams(
            dimension_semantics=("arbitrary", "arbitrary"),
            vmem_limit_bytes=VMEM_LIMIT_BYTES),
        name="hybrid_block",
    )(x, mod, norm_w.reshape(1, d), w_in_t, conv_w, conv_b.reshape(1, CONV_DIM),
      pad_heads(dt_bias), pad_heads(a_log), dskip_e, m_norm_w.reshape(1, M_INNER),
      w_proj_m, w_proj_r, w_out, out_norm_w.reshape(1, d),
      cos_t, sin_t, dmat, qdec, kdec, rdec, tril)


@jax.jit
def kernel(x, c, w_ada, b_ada, norm_w, w_in, conv_w, conv_b, dt_bias, a_log, d_skip, m_norm_w,
           w_proj_m, w_proj_r, w_out, final_norm_w):
    assert w_ada.shape[0] == 1
    return _layer(x, c, w_ada[0], b_ada[0], norm_w[0], jnp.swapaxes(w_in, 1, 2)[0], conv_w[0], conv_b[0], dt_bias[0],
                  a_log[0], d_skip[0], m_norm_w[0], w_proj_m[0], w_proj_r[0], w_out[0], final_norm_w)
```

```python
import functools

import jax
import jax.numpy as jnp
import numpy as np
from jax import lax
from jax.experimental import pallas as pl
from jax.experimental.pallas import tpu as pltpu

f32 = jnp.float32
bf16 = jnp.bfloat16

D_MODEL = 1024
M_HEADDIM = 64
M_HEADS = 16
M_GROUPS = 2
M_STATE = 64
M_CONV = 4
M_INNER = D_MODEL
GROUP_W = M_INNER // M_GROUPS
HEADS_PER_GROUP = M_HEADS // M_GROUPS
CONV_DIM = M_INNER + 2 * M_GROUPS * M_STATE
R_HEADS = 8
R_QK = 512
R_HEAD_QK = 64
R_V = 1024
R_HEAD_V = 128
CHUNK = 128
ROPE_BASE = 10000.0
EPS = 1e-6
LOG2_E = 1.4426950408889634

LANES = 128
SUBLANES = 8
MXU_W = 256
TOK = 512
N_CHUNKS = TOK // CHUNK
EDGE_ROWS = 256
W_SLABS_IN_FLIGHT = 4
VMEM_LIMIT_BYTES = 58 * 1024 * 1024

OFF_Z = 0
OFF_XBC = OFF_Z + M_INNER
OFF_DT = OFF_XBC + CONV_DIM
OFF_Q = OFF_DT + MXU_W
OFF_K = OFF_Q + R_QK
OFF_V = OFF_K + R_QK
OFF_GR = OFF_V + R_V
OFF_GAM = OFF_GR + R_V
OFF_GAR = OFF_GAM + D_MODEL
W_COLS = OFF_GAR + D_MODEL

NT_DIMS = (((1,), (1,)), ((), ()))


def _mod_kernel(c_ref, w_ref, b_ref, o_ref):
    o_ref[...] = jnp.dot(c_ref[...], w_ref[...], preferred_element_type=f32,
                         precision=lax.Precision.HIGHEST) + b_ref[...]


def _sigmoid(v):
    return 1.0 / (1.0 + jnp.exp(-v))


def _silu(v):
    return v * _sigmoid(v)


def _softplus(v):
    return jnp.maximum(v, 0.0) + jnp.log1p(jnp.exp(-jnp.abs(v)))


def _dot(a, b):
    return jnp.dot(a, b, preferred_element_type=f32)


def _split3(v):
    hi = v.astype(bf16)
    r1 = v - hi.astype(f32)
    mid = r1.astype(bf16)
    lo = (r1 - mid.astype(f32)).astype(bf16)
    return hi, mid, lo


def _load_weights(wt_hbm, square_hbm, win_ref, square_refs, stage, sems):
    dt_tile = OFF_DT // MXU_W
    jobs = []

    def w_in_tile(w_t, j):
        w = w_t.T
        if j == dt_tile:
            w = jnp.where(lax.broadcasted_iota(jnp.int32, w.shape, 1) < M_HEADS, w, 0.0)
        win_ref[:, j * MXU_W:(j + 1) * MXU_W] = w.astype(bf16)

    def square_rows(w, dst, r0):
        dst[r0:r0 + MXU_W, :] = w.astype(bf16)

    for j in range(W_COLS // MXU_W):
        row0 = j * MXU_W if j <= dt_tile else j * MXU_W - (MXU_W - M_HEADS)
        jobs.append((wt_hbm.at[pl.ds(row0, MXU_W), :], functools.partial(w_in_tile, j=j)))
    for src, dst in zip(square_hbm, square_refs):
        for r0 in range(0, D_MODEL, MXU_W):
            jobs.append((src.at[pl.ds(r0, MXU_W), :], functools.partial(square_rows, dst=dst, r0=r0)))

    def slab_copy(i):
        slot = i % W_SLABS_IN_FLIGHT
        return pltpu.make_async_copy(jobs[i][0], stage.at[slot], sems.at[slot])

    for i in range(W_SLABS_IN_FLIGHT - 1):
        slab_copy(i).start()
    for i in range(len(jobs)):
        if i + W_SLABS_IN_FLIGHT - 1 < len(jobs):
            slab_copy(i + W_SLABS_IN_FLIGHT - 1).start()
        slab_copy(i).wait()
        jobs[i][1](stage[i % W_SLABS_IN_FLIGHT])


def _block_kernel(x_ref, mod_ref, normw_ref, wt_hbm, convw_ref, convb_ref,
                  dtb_ref, alog_ref, dskip_ref, mnw_ref, wpm_hbm, wpr_hbm, wout_hbm, fnw_ref,
                  cos_ref, sin_ref, dmat_ref, qdec_ref, kdec_ref, rdec_ref, tril_ref,
                  o_ref,
                  xbc_buf, s_state, r_state, y_buf, o_buf, z_buf, gr_buf, gam_buf, gar_buf,
                  q_buf, k_buf, v_buf, win_ref, wpm_ref, wpr_ref, wout_ref, w_stage, w_sems):
    @pl.when(jnp.logical_and(pl.program_id(0) == 0, pl.program_id(1) == 0))
    def _first_step():
        _load_weights(wt_hbm, (wpm_hbm, wpr_hbm, wout_hbm), win_ref, (wpm_ref, wpr_ref, wout_ref), w_stage, w_sems)

    @pl.when(pl.program_id(1) == 0)
    def _start_of_sequence():
        xbc_buf[0:SUBLANES, :] = jnp.zeros((SUBLANES, CONV_DIM), f32)
        s_state[...] = jnp.zeros_like(s_state)
        r_state[...] = jnp.zeros_like(r_state)

    mod = mod_ref[pl.ds(pl.program_id(0), 1), :]
    shift, scale, gate = (mod[:, i * D_MODEL:(i + 1) * D_MODEL] for i in range(3))
    g1 = normw_ref[...] * (1.0 + scale)
    slabs = [slice(r0, r0 + EDGE_ROWS) for r0 in range(0, TOK, EDGE_ROWS)]
    hb_slabs = []
    for rows in slabs:
        x = x_ref[rows, :]
        rs = lax.rsqrt(jnp.mean(x * x, axis=-1, keepdims=True) + EPS)
        hb_slabs.append((x * rs * g1 + shift).astype(bf16))
    hb = jnp.concatenate(hb_slabs, axis=0)

    def proj(off, n):
        return _dot(hb, win_ref[:, off:off + n])

    def proj_tiles(dst, off):
        def tile(c0):
            dst[:, c0:c0 + MXU_W] = proj(off + c0, MXU_W).astype(dst.dtype)
        return [functools.partial(tile, c0) for c0 in range(0, dst.shape[1], MXU_W)]

    def emit(queue, n=1):
        for _ in range(min(n, len(queue))):
            queue.pop(0)()

    qkv_fill = proj_tiles(q_buf, OFF_Q) + proj_tiles(k_buf, OFF_K) + proj_tiles(v_buf, OFF_V)
    gate_fill = (proj_tiles(z_buf, OFF_Z) + proj_tiles(gr_buf, OFF_GR)
                 + proj_tiles(gam_buf, OFF_GAM) + proj_tiles(gar_buf, OFF_GAR))

    xbc_dt = jnp.concatenate([_dot(h, win_ref[:, OFF_XBC:OFF_XBC + CONV_DIM + MXU_W]) for h in hb_slabs],
                             axis=0)
    xbc_buf[SUBLANES:SUBLANES + TOK, :] = xbc_dt[:, :CONV_DIM]
    dt_raw = xbc_dt[:, CONV_DIM:CONV_DIM + LANES]
    emit(gate_fill, 2)

    conv = convb_ref[...] + xbc_buf[SUBLANES:SUBLANES + TOK, :] * convw_ref[M_CONV - 1:M_CONV, :]
    for kk in range(M_CONV - 1):
        r0 = SUBLANES - (M_CONV - 1) + kk
        conv = conv + xbc_buf[r0:r0 + TOK, :] * convw_ref[kk:kk + 1, :]
    xbc_buf[0:SUBLANES, :] = xbc_buf[TOK:TOK + SUBLANES, :]
    xa = _silu(conv)
    xm = xa[:, :M_INNER]
    bmat = xa[:, M_INNER:M_INNER + LANES]
    cmat = xa[:, M_INNER + LANES:CONV_DIM]
    xm_b = xm.astype(bf16)

    lane = lax.broadcasted_iota(jnp.int32, (1, LANES), 1)
    a_neg = jnp.where(lane < M_HEADS, -jnp.exp(alog_ref[...]), 0.0)
    dt = _softplus(dt_raw + dtb_ref[...])
    a = dt * (a_neg * LOG2_E)
    a_split = jnp.concatenate(_split3(a), axis=1)
    a_parts = jnp.concatenate([_dot(tril_ref[...], a_split[r0:r0 + MXU_W, :]) for r0 in range(0, TOK, MXU_W)],
                              axis=0)
    a_cs = a_parts[:, :LANES] + a_parts[:, LANES:2 * LANES] + a_parts[:, 2 * LANES:]
    emit(gate_fill, 2)

    ii = lax.broadcasted_iota(jnp.int32, (CHUNK, CHUNK), 0)
    jj = lax.broadcasted_iota(jnp.int32, (CHUNK, CHUNK), 1)
    causal = ii >= jj
    left_head = jj < M_HEADDIM

    def blockdiag(slab):
        zero = jnp.zeros_like(slab)
        return jnp.concatenate([jnp.where(left_head, slab, zero), jnp.where(left_head, zero, slab)], axis=0)

    def ssd_chunk(c):
        r0 = c * CHUNK
        acs_c = a_cs[r0:r0 + CHUNK, :]
        acs_t = acs_c.T[:M_HEADS, :]
        dt_t = dt[r0:r0 + CHUNK, :].T[:M_HEADS, :]
        w_t = jnp.exp2(acs_t[:, CHUNK - 1:CHUNK] - acs_t) * dt_t
        src_t = acs_t - jnp.log2(dt_t)
        b_t = bmat[r0:r0 + CHUNK, :].T
        c_c = cmat[r0:r0 + CHUNK, :].astype(bf16)
        cb2 = lax.dot_general(c_c, blockdiag(bmat[r0:r0 + CHUNK, :].astype(bf16)), NT_DIMS,
                              preferred_element_type=f32)
        s_prev = [s_state[g] for g in range(M_GROUPS)]
        y_off = [_dot(c_c[:, g * M_STATE:(g + 1) * M_STATE], s_prev[g].astype(bf16))
                 for g in range(M_GROUPS)]
        return acs_c, src_t, w_t, b_t, [cb2[:, :CHUNK], cb2[:, CHUNK:]], s_prev, y_off

    def ssd_pair(c, pair, chunk):
        acs_c, src_t, w_t, b_t, cb, s_prev, y_off = chunk
        r0 = c * CHUNK
        g = pair // (HEADS_PER_GROUP // 2)
        gslab = slice((pair % (HEADS_PER_GROUP // 2)) * LANES, (pair % (HEADS_PER_GROUP // 2) + 1) * LANES)
        pcols = slice(pair * LANES, (pair + 1) * LANES)
        heads = (2 * pair, 2 * pair + 1)
        cols = [jnp.broadcast_to(acs_c[:, h:h + 1], (CHUNK, CHUNK)) for h in heads]
        w_rows = [(cb[g] * jnp.exp2(jnp.where(causal, col - src_t[h:h + 1, :], -jnp.inf))).astype(bf16)
                  for h, col in zip(heads, cols)]
        bt_g = b_t[g * M_STATE:(g + 1) * M_STATE, :]
        st_rows = [(bt_g * w_t[h:h + 1, :]).astype(bf16) for h in heads]
        lhs = jnp.concatenate([jnp.concatenate(w_rows, axis=1), jnp.concatenate(st_rows, axis=1)], axis=0)
        res = _dot(lhs, blockdiag(xm_b[r0:r0 + CHUNK, pcols]))
        decay = jnp.exp2(jnp.where(left_head, cols[0], cols[1]))
        y_buf[r0:r0 + CHUNK, pcols] = res[:CHUNK] + y_off[g][:, gslab] * decay
        s_state[g, :, gslab] = s_prev[g][:, gslab] * decay[CHUNK - 1:CHUNK, :] + res[CHUNK:]

    ret = {}

    def retention_prep():
        cos = jnp.concatenate([cos_ref[...]] * (R_QK // LANES), axis=1)
        sin = jnp.concatenate([sin_ref[...]] * (R_QK // LANES), axis=1)
        lane_q = lax.broadcasted_iota(jnp.int32, (TOK, R_QK), 1)
        first_half = (lane_q % R_HEAD_QK) < (R_HEAD_QK // 2)

        def rotary(t):
            swapped = jnp.where(first_half, pltpu.roll(t, R_QK - R_HEAD_QK // 2, 1),
                                pltpu.roll(t, R_HEAD_QK // 2, 1))
            return t * cos + swapped * sin

        q = rotary(q_buf[...])
        k = rotary(k_buf[...]) * (R_HEAD_QK ** -0.5)
        qdec = jnp.concatenate([qdec_ref[...]] * N_CHUNKS, axis=0)
        kdec = jnp.concatenate([kdec_ref[...]] * N_CHUNKS, axis=0)
        ret.update(q_b=q.astype(bf16), k_b=k.astype(bf16), qw_b=(q * qdec).astype(bf16), kw=k * kdec)

    s_b = {}

    def score_pair(c, pair):
        r0 = c * CHUNK
        pcols = slice(pair * LANES, (pair + 1) * LANES)
        s_pair = (lax.dot_general(ret["q_b"][r0:r0 + CHUNK, pcols], blockdiag(ret["k_b"][r0:r0 + CHUNK, pcols]),
                                  NT_DIMS, preferred_element_type=f32) * dmat_ref[pair]).astype(bf16)
        s_b[c, 2 * pair], s_b[c, 2 * pair + 1] = s_pair[:, :CHUNK], s_pair[:, CHUNK:]

    def retention_head(c, h, kw_t):
        r0 = c * CHUNK
        qk_cols = slice(h * R_HEAD_QK, (h + 1) * R_HEAD_QK)
        v_cols = slice(h * R_HEAD_V, (h + 1) * R_HEAD_V)
        r_prev = r_state[h]
        lhs = jnp.concatenate(
            [jnp.concatenate([s_b[c, h], ret["qw_b"][r0:r0 + CHUNK, qk_cols]], axis=1),
             jnp.concatenate([kw_t[qk_cols, :], jnp.zeros((R_HEAD_QK, R_HEAD_QK), bf16)], axis=1)], axis=0)
        res = _dot(lhs, jnp.concatenate([v_buf[r0:r0 + CHUNK, v_cols], r_prev.astype(bf16)], axis=0))
        o_h = res[:CHUNK]
        r_state[h] = r_prev * rdec_ref[h] + res[CHUNK:]
        o_buf[r0:r0 + CHUNK, v_cols] = o_h * lax.rsqrt(jnp.mean(o_h * o_h, axis=-1, keepdims=True) + EPS)

    def ssd_output():
        y = y_buf[...] + dskip_ref[...] * xm
        yz = y * _silu(z_buf[...])
        y_m = []
        for g in range(M_GROUPS):
            yz_g = yz[:, g * GROUP_W:(g + 1) * GROUP_W]
            y_m.append(yz_g * lax.rsqrt(jnp.mean(yz_g * yz_g, axis=-1, keepdims=True) + EPS))
        return (jnp.concatenate(y_m, axis=1) * mnw_ref[...]).astype(bf16)

    half = R_HEADS // 2
    n_mid, n_gate = (N_CHUNKS - 1) * R_HEADS, len(gate_fill) - 2
    u_m_tiles = []
    for r in range(N_CHUNKS + 1):
        if r < N_CHUNKS:
            chunk = ssd_chunk(r)
        if r >= 1:
            kw_t = ret["kw"][(r - 1) * CHUNK:r * CHUNK, :].T.astype(bf16)
        if r == N_CHUNKS:
            y_m = ssd_output()
        for i in range(R_HEADS):
            if r < N_CHUNKS:
                ssd_pair(r, i, chunk)
                if r == 0 and i == half:
                    retention_prep()
                if i >= half:
                    score_pair(r, i - half)
            if r >= 1:
                retention_head(r - 1, i, kw_t)
            if r == 0 and i < R_HEADS - 2:
                emit(qkv_fill, 2 if i < 2 else 1)
            elif r == 0:
                emit(gate_fill)
            elif r < N_CHUNKS:
                it = (r - 1) * R_HEADS + i
                emit(gate_fill, (it + 1) * n_gate // n_mid - it * n_gate // n_mid)
            elif i < D_MODEL // MXU_W:
                u_m_tiles.append(_dot(y_m, wpm_ref[:, i * MXU_W:(i + 1) * MXU_W]))
    assert not qkv_fill and not gate_fill and M_HEADS // 2 == R_HEADS
    u_m = jnp.concatenate(u_m_tiles, axis=1)

    u_r = [_dot((_silu(gr_buf[rows, :]) * o_buf[rows, :]).astype(bf16), wpr_ref[...]) for rows in slabs]
    for rows, u_r_slab in zip(slabs, u_r):
        merged = _sigmoid(gam_buf[rows, :]) * u_m[rows, :] + _sigmoid(gar_buf[rows, :]) * u_r_slab
        xo = x_ref[rows, :] + gate * _dot(merged.astype(bf16), wout_ref[...])
        o_ref[rows, :] = xo * lax.rsqrt(jnp.mean(xo * xo, axis=-1, keepdims=True) + EPS) * fnw_ref[...]


def _resident(shape):
    zeros = (0,) * len(shape)
    return pl.BlockSpec(shape, lambda b, s: zeros, pipeline_mode=pl.Buffered(1))


def _tables(seq_len):
    half = R_HEAD_QK // 2
    pos = np.arange(seq_len, dtype=np.float64)
    inv = ROPE_BASE ** (-np.arange(half, dtype=np.float64) / half)
    ang = pos[:, None] * inv[None, :]
    cos_t = np.tile(np.cos(ang), (1, LANES // half))
    sin_t = np.tile(np.concatenate([-np.sin(ang), np.sin(ang)], axis=1), (1, LANES // R_HEAD_QK))
    log_g = np.log1p(-np.exp2(-5.0 - np.arange(R_HEADS, dtype=np.float64)))
    idx = np.arange(CHUNK, dtype=np.float64)
    rel = idx[:, None] - idx[None, :]
    dmat = np.where(rel[None] >= 0, np.exp(np.minimum(rel[None], CHUNK) * log_g[:, None, None]), 0.0)
    dmat = np.concatenate([dmat[0::2], dmat[1::2]], axis=2)
    qdec = np.repeat(np.exp((idx + 1)[:, None] * log_g[None, :]), R_HEAD_QK, axis=1)
    kdec = np.repeat(np.exp((CHUNK - 1 - idx)[:, None] * log_g[None, :]), R_HEAD_QK, axis=1)
    rdec = np.broadcast_to(np.exp(CHUNK * log_g)[:, None, None], (R_HEADS, 1, R_HEAD_V))
    t = np.arange(MXU_W)
    tril = (t[:, None] >= t[None, :]) & (t[:, None] // CHUNK == t[None, :] // CHUNK)
    as_f32 = lambda a: jnp.asarray(np.ascontiguousarray(a, dtype=np.float32))
    as_bf16 = lambda a: jnp.asarray(np.ascontiguousarray(a, dtype=np.float32), dtype=bf16)
    return (as_f32(cos_t), as_f32(sin_t), as_f32(dmat), as_f32(qdec), as_f32(kdec), as_f32(rdec),
            as_bf16(tril))


def _layer(x, c, w_ada, b_ada, norm_w, w_in_t, conv_w, conv_b, dt_bias, a_log, d_skip,
           m_norm_w, w_proj_m, w_proj_r, w_out, out_norm_w):
    bsz, seq_len, d = x.shape
    assert d == D_MODEL and seq_len % TOK == 0

    mod = pl.pallas_call(
        _mod_kernel,
        grid=(3,),
        in_specs=[pl.BlockSpec((bsz, d), lambda j: (0, 0)),
                  pl.BlockSpec((d, d), lambda j: (0, j)),
                  pl.BlockSpec((1, d), lambda j: (0, j))],
        out_specs=pl.BlockSpec((bsz, d), lambda j: (0, j)),
        out_shape=jax.ShapeDtypeStruct((bsz, 3 * d), f32),
        name="adaln_mod",
    )(c, w_ada, b_ada.reshape(1, 3 * d))

    assert w_in_t.shape == (W_COLS - MXU_W + M_HEADS, d)
    pad_heads = lambda p: jnp.pad(p.reshape(1, M_HEADS), ((0, 0), (0, LANES - M_HEADS)))
    dskip_e = jnp.repeat(d_skip, M_HEADDIM).reshape(1, M_INNER)
    cos_t, sin_t, dmat, qdec, kdec, rdec, tril = _tables(seq_len)

    tok_block = lambda w: pl.BlockSpec((TOK, w), lambda b, s: (s, 0))
    return pl.pallas_call(
        _block_kernel,
        grid=(bsz, seq_len // TOK),
        in_specs=[
            pl.BlockSpec((None, TOK, d), lambda b, s: (b, s, 0)),
            _resident((bsz, 3 * d)),
            _resident((1, d)),
            pl.BlockSpec(memory_space=pl.ANY),
            _resident((M_CONV, CONV_DIM)), _resident((1, CONV_DIM)),
            _resident((1, LANES)), _resident((1, LANES)),
            _resident((1, M_INNER)), _resident((1, M_INNER)),
            pl.BlockSpec(memory_space=pl.ANY), pl.BlockSpec(memory_space=pl.ANY),
            pl.BlockSpec(memory_space=pl.ANY),
            _resident((1, d)),
            tok_block(LANES), tok_block(LANES),
            _resident((R_HEADS // 2, CHUNK, 2 * CHUNK)),
            _resident((CHUNK, R_QK)), _resident((CHUNK, R_QK)),
            _resident((R_HEADS, 1, R_HEAD_V)),
            _resident((MXU_W, MXU_W)),
        ],
        out_specs=pl.BlockSpec((None, TOK, d), lambda b, s: (b, s, 0)),
        out_shape=jax.ShapeDtypeStruct((bsz, seq_len, d), x.dtype),
        scratch_shapes=[
            pltpu.VMEM((TOK + 2 * SUBLANES, CONV_DIM), f32),
            pltpu.VMEM((M_GROUPS, M_STATE, GROUP_W), f32),
            pltpu.VMEM((R_HEADS, R_HEAD_QK, R_HEAD_V), f32),
            pltpu.VMEM((TOK, M_INNER), f32),
            pltpu.VMEM((TOK, R_V), f32),
            pltpu.VMEM((TOK, M_INNER), f32),
            pltpu.VMEM((TOK, R_V), f32),
            pltpu.VMEM((TOK, D_MODEL), f32),
            pltpu.VMEM((TOK, D_MODEL), f32),
            pltpu.VMEM((TOK, R_QK), f32),
            pltpu.VMEM((TOK, R_QK), f32),
            pltpu.VMEM((TOK, R_V), bf16),
            pltpu.VMEM((d, W_COLS), bf16),
            pltpu.VMEM((M_INNER, d), bf16), pltpu.VMEM((R_V, d), bf16), pltpu.VMEM((d, d), bf16),
            pltpu.VMEM((W_SLABS_IN_FLIGHT, MXU_W, d), f32),
            pltpu.SemaphoreType.DMA((W_SLABS_IN_FLIGHT,)),
        ],
        compiler_params=pltpu.CompilerParams(
            dimension_semantics=("arbitrary", "arbitrary"),
            vmem_limit_bytes=VMEM_LIMIT_BYTES),
        name="hybrid_block",
    )(x, mod, norm_w.reshape(1, d), w_in_t, conv_w, conv_b.reshape(1, CONV_DIM),
      pad_heads(dt_bias), pad_heads(a_log), dskip_e, m_norm_w.reshape(1, M_INNER),
      w_proj_m, w_proj_r, w_out, out_norm_w.reshape(1, d),
      cos_t, sin_t, dmat, qdec, kdec, rdec, tril)


@jax.jit
def kernel(x, c, w_ada, b_ada, norm_w, w_in, conv_w, conv_b, dt_bias, a_log, d_skip, m_norm_w,
           w_proj_m, w_proj_r, w_out, final_norm_w):
    assert w_ada.shape[0] == 1
    return _layer(x, c, w_ada[0], b_ada[0], norm_w[0], jnp.swapaxes(w_in, 1, 2)[0], conv_w[0], conv_b[0], dt_bias[0],
                  a_log[0], d_skip[0], m_norm_w[0], w_proj_m[0], w_proj_r[0], w_out[0], final_norm_w)
```

```python
import functools

import jax
import jax.numpy as jnp
import numpy as np
from jax import lax
from jax.experimental import pallas as pl
from jax.experimental.pallas import tpu as pltpu

f32 = jnp.float32
bf16 = jnp.bfloat16

D_MODEL = 1024
M_HEADDIM = 64
M_HEADS = 16
M_GROUPS = 2
M_STATE = 64
M_CONV = 4
M_INNER = D_MODEL
GROUP_W = M_INNER // M_GROUPS
HEADS_PER_GROUP = M_HEADS // M_GROUPS
CONV_DIM = M_INNER + 2 * M_GROUPS * M_STATE
R_HEADS = 8
R_QK = 512
R_HEAD_QK = 64
R_V = 1024
R_HEAD_V = 128
CHUNK = 128
ROPE_BASE = 10000.0
EPS = 1e-6
LOG2_E = 1.4426950408889634

LANES = 128
SUBLANES = 8
MXU_W = 256
TOK = 512
N_CHUNKS = TOK // CHUNK
EDGE_ROWS = 256
W_SLABS_IN_FLIGHT = 4
VMEM_LIMIT_BYTES = 58 * 1024 * 1024

OFF_Z = 0
OFF_XBC = OFF_Z + M_INNER
OFF_DT = OFF_XBC + CONV_DIM
OFF_Q = OFF_DT + MXU_W
OFF_K = OFF_Q + R_QK
OFF_V = OFF_K + R_QK
OFF_GR = OFF_V + R_V
OFF_GAM = OFF_GR + R_V
OFF_GAR = OFF_GAM + D_MODEL
W_COLS = OFF_GAR + D_MODEL

NT_DIMS = (((1,), (1,)), ((), ()))


def _mod_kernel(c_ref, w_ref, b_ref, o_ref):
    o_ref[...] = jnp.dot(c_ref[...], w_ref[...], preferred_element_type=f32,
                         precision=lax.Precision.HIGHEST) + b_ref[...]


def _sigmoid(v):
    return 1.0 / (1.0 + jnp.exp(-v))


def _silu(v):
    return v * _sigmoid(v)


def _softplus(v):
    return jnp.maximum(v, 0.0) + jnp.log1p(jnp.exp(-jnp.abs(v)))


def _dot(a, b):
    return jnp.dot(a, b, preferred_element_type=f32)


def _split3(v):
    hi = v.astype(bf16)
    r1 = v - hi.astype(f32)
    mid = r1.astype(bf16)
    lo = (r1 - mid.astype(f32)).astype(bf16)
    return hi, mid, lo


def _load_weights(wt_hbm, square_hbm, win_ref, square_refs, stage, sems):
    dt_tile = OFF_DT // MXU_W
    jobs = []

    def w_in_tile(w_t, j):
        w = w_t.T
        if j == dt_tile:
            w = jnp.where(lax.broadcasted_iota(jnp.int32, w.shape, 1) < M_HEADS, w, 0.0)
        win_ref[:, j * MXU_W:(j + 1) * MXU_W] = w.astype(bf16)

    def square_rows(w, dst, r0):
        dst[r0:r0 + MXU_W, :] = w.astype(bf16)

    for j in range(W_COLS // MXU_W):
        row0 = j * MXU_W if j <= dt_tile else j * MXU_W - (MXU_W - M_HEADS)
        jobs.append((wt_hbm.at[pl.ds(row0, MXU_W), :], functools.partial(w_in_tile, j=j)))
    for src, dst in zip(square_hbm, square_refs):
        for r0 in range(0, D_MODEL, MXU_W):
            jobs.append((src.at[pl.ds(r0, MXU_W), :], functools.partial(square_rows, dst=dst, r0=r0)))

    def slab_copy(i):
        slot = i % W_SLABS_IN_FLIGHT
        return pltpu.make_async_copy(jobs[i][0], stage.at[slot], sems.at[slot])

    for i in range(W_SLABS_IN_FLIGHT - 1):
        slab_copy(i).start()
    for i in range(len(jobs)):
        if i + W_SLABS_IN_FLIGHT - 1 < len(jobs):
            slab_copy(i + W_SLABS_IN_FLIGHT - 1).start()
        slab_copy(i).wait()
        jobs[i][1](stage[i % W_SLABS_IN_FLIGHT])


def _block_kernel(x_ref, mod_ref, normw_ref, wt_hbm, convw_ref, convb_ref,
                  dtb_ref, alog_ref, dskip_ref, mnw_ref, wpm_hbm, wpr_hbm, wout_hbm, fnw_ref,
                  cos_ref, sin_ref, dmat_ref, qdec_ref, kdec_ref, rdec_ref, tril_ref,
                  o_ref,
                  xbc_buf, s_state, r_state, y_buf, o_buf, z_buf, gr_buf, gam_buf, gar_buf,
                  q_buf, k_buf, v_buf, shiftw_buf, win_ref, wpm_ref, wpr_ref, wout_ref, w_stage, w_sems):
    w_xbc_dt = win_ref.at[:, OFF_XBC:OFF_XBC + CONV_DIM + MXU_W]

    @pl.when(jnp.logical_and(pl.program_id(0) == 0, pl.program_id(1) == 0))
    def _first_step():
        _load_weights(wt_hbm, (wpm_hbm, wpr_hbm, wout_hbm), win_ref, (wpm_ref, wpr_ref, wout_ref), w_stage, w_sems)
        shiftw_buf[...] = _dot(mod_ref[:, :D_MODEL].astype(bf16), w_xbc_dt[...])

    @pl.when(pl.program_id(1) == 0)
    def _start_of_sequence():
        xbc_buf[0:SUBLANES, :] = jnp.zeros((SUBLANES, CONV_DIM), f32)
        s_state[...] = jnp.zeros_like(s_state)
        r_state[...] = jnp.zeros_like(r_state)

    mod = mod_ref[pl.ds(pl.program_id(0), 1), :]
    shift, scale, gate = (mod[:, i * D_MODEL:(i + 1) * D_MODEL] for i in range(3))
    shift_w = shiftw_buf[pl.ds(pl.program_id(0), 1), :]

    g1 = normw_ref[...] * (1.0 + scale)
    slabs = [slice(r0, r0 + EDGE_ROWS) for r0 in range(0, TOK, EDGE_ROWS)]
    hb_slabs, xbc_dt_slabs = [], []
    for rows in slabs:
        x = x_ref[rows, :]
        xg = x * g1
        rs = lax.rsqrt(jnp.mean(x * x, axis=-1, keepdims=True) + EPS)
        xbc_dt_slabs.append(_dot(xg.astype(bf16), w_xbc_dt[...]) * rs + shift_w)
        hb_slabs.append((xg * rs + shift).astype(bf16))
    hb = jnp.concatenate(hb_slabs, axis=0)

    def proj(off, n):
        return _dot(hb, win_ref[:, off:off + n])

    def proj_tiles(dst, off):
        def tile(c0):
            dst[:, c0:c0 + MXU_W] = proj(off + c0, MXU_W).astype(dst.dtype)
        return [functools.partial(tile, c0) for c0 in range(0, dst.shape[1], MXU_W)]

    def emit(queue, n=1):
        for _ in range(min(n, len(queue))):
            queue.pop(0)()

    qkv_fill = proj_tiles(q_buf, OFF_Q) + proj_tiles(k_buf, OFF_K) + proj_tiles(v_buf, OFF_V)
    gate_fill = (proj_tiles(z_buf, OFF_Z) + proj_tiles(gr_buf, OFF_GR)
                 + proj_tiles(gam_buf, OFF_GAM) + proj_tiles(gar_buf, OFF_GAR))

    xbc_dt = jnp.concatenate(xbc_dt_slabs, axis=0)
    xbc_buf[SUBLANES:SUBLANES + TOK, :] = xbc_dt[:, :CONV_DIM]
    dt_raw = xbc_dt[:, CONV_DIM:CONV_DIM + LANES]
    emit(gate_fill, 2)

    conv = convb_ref[...] + xbc_buf[SUBLANES:SUBLANES + TOK, :] * convw_ref[M_CONV - 1:M_CONV, :]
    for kk in range(M_CONV - 1):
        r0 = SUBLANES - (M_CONV - 1) + kk
        conv = conv + xbc_buf[r0:r0 + TOK, :] * convw_ref[kk:kk + 1, :]
    xbc_buf[0:SUBLANES, :] = xbc_buf[TOK:TOK + SUBLANES, :]
    xa = _silu(conv)
    xm = xa[:, :M_INNER]
    bmat = xa[:, M_INNER:M_INNER + LANES]
    cmat = xa[:, M_INNER + LANES:CONV_DIM]
    xm_b = xm.astype(bf16)

    lane = lax.broadcasted_iota(jnp.int32, (1, LANES), 1)
    a_neg = jnp.where(lane < M_HEADS, -jnp.exp(alog_ref[...]), 0.0)
    dt = _softplus(dt_raw + dtb_ref[...])
    a = dt * (a_neg * LOG2_E)
    a_split = jnp.concatenate(_split3(a), axis=1)
    a_parts = jnp.concatenate([_dot(tril_ref[...], a_split[r0:r0 + MXU_W, :]) for r0 in range(0, TOK, MXU_W)],
                              axis=0)
    a_cs = a_parts[:, :LANES] + a_parts[:, LANES:2 * LANES] + a_parts[:, 2 * LANES:]
    emit(gate_fill, 2)

    ii = lax.broadcasted_iota(jnp.int32, (CHUNK, CHUNK), 0)
    jj = lax.broadcasted_iota(jnp.int32, (CHUNK, CHUNK), 1)
    causal = ii >= jj
    left_head = jj < M_HEADDIM

    def blockdiag(slab):
        zero = jnp.zeros_like(slab)
        return jnp.concatenate([jnp.where(left_head, slab, zero), jnp.where(left_head, zero, slab)], axis=0)

    def ssd_chunk(c):
        r0 = c * CHUNK
        acs_c = a_cs[r0:r0 + CHUNK, :]
        acs_t = acs_c.T[:M_HEADS, :]
        dt_t = dt[r0:r0 + CHUNK, :].T[:M_HEADS, :]
        w_t = jnp.exp2(acs_t[:, CHUNK - 1:CHUNK] - acs_t) * dt_t
        src_t = acs_t - jnp.log2(dt_t)
        b_t = bmat[r0:r0 + CHUNK, :].T
        c_c = cmat[r0:r0 + CHUNK, :].astype(bf16)
        cb2 = lax.dot_general(c_c, blockdiag(bmat[r0:r0 + CHUNK, :].astype(bf16)), NT_DIMS,
                              preferred_element_type=f32)
        s_prev = [s_state[g] for g in range(M_GROUPS)]
        y_off = [_dot(c_c[:, g * M_STATE:(g + 1) * M_STATE], s_prev[g].astype(bf16))
                 for g in range(M_GROUPS)]
        return acs_c, src_t, w_t, b_t, [cb2[:, :CHUNK], cb2[:, CHUNK:]], s_prev, y_off

    def ssd_pair(c, pair, chunk):
        acs_c, src_t, w_t, b_t, cb, s_prev, y_off = chunk
        r0 = c * CHUNK
        g = pair // (HEADS_PER_GROUP // 2)
        gslab = slice((pair % (HEADS_PER_GROUP // 2)) * LANES, (pair % (HEADS_PER_GROUP // 2) + 1) * LANES)
        pcols = slice(pair * LANES, (pair + 1) * LANES)
        heads = (2 * pair, 2 * pair + 1)
        cols = [jnp.broadcast_to(acs_c[:, h:h + 1], (CHUNK, CHUNK)) for h in heads]
        w_rows = [(cb[g] * jnp.exp2(jnp.where(causal, col - src_t[h:h + 1, :], -jnp.inf))).astype(bf16)
                  for h, col in zip(heads, cols)]
        bt_g = b_t[g * M_STATE:(g + 1) * M_STATE, :]
        st_rows = [(bt_g * w_t[h:h + 1, :]).astype(bf16) for h in heads]
        lhs = jnp.concatenate([jnp.concatenate(w_rows, axis=1), jnp.concatenate(st_rows, axis=1)], axis=0)
        res = _dot(lhs, blockdiag(xm_b[r0:r0 + CHUNK, pcols]))
        decay = jnp.exp2(jnp.where(left_head, cols[0], cols[1]))
        y_buf[r0:r0 + CHUNK, pcols] = res[:CHUNK] + y_off[g][:, gslab] * decay
        s_state[g, :, gslab] = s_prev[g][:, gslab] * decay[CHUNK - 1:CHUNK, :] + res[CHUNK:]

    ret = {}

    def retention_prep():
        cos = jnp.concatenate([cos_ref[...]] * (R_QK // LANES), axis=1)
        sin = jnp.concatenate([sin_ref[...]] * (R_QK // LANES), axis=1)
        lane_q = lax.broadcasted_iota(jnp.int32, (TOK, R_QK), 1)
        first_half = (lane_q % R_HEAD_QK) < (R_HEAD_QK // 2)

        def rotary(t):
            swapped = jnp.where(first_half, pltpu.roll(t, R_QK - R_HEAD_QK // 2, 1),
                                pltpu.roll(t, R_HEAD_QK // 2, 1))
            return t * cos + swapped * sin

        q = rotary(q_buf[...])
        k = rotary(k_buf[...]) * (R_HEAD_QK ** -0.5)
        qdec = jnp.concatenate([qdec_ref[...]] * N_CHUNKS, axis=0)
        kdec = jnp.concatenate([kdec_ref[...]] * N_CHUNKS, axis=0)
        ret.update(q_b=q.astype(bf16), k_b=k.astype(bf16), qw_b=(q * qdec).astype(bf16), kw=k * kdec)

    s_b = {}

    def score_pair(c, pair):
        r0 = c * CHUNK
        pcols = slice(pair * LANES, (pair + 1) * LANES)
        s_pair = (lax.dot_general(ret["q_b"][r0:r0 + CHUNK, pcols], blockdiag(ret["k_b"][r0:r0 + CHUNK, pcols]),
                                  NT_DIMS, preferred_element_type=f32) * dmat_ref[pair]).astype(bf16)
        s_b[c, 2 * pair], s_b[c, 2 * pair + 1] = s_pair[:, :CHUNK], s_pair[:, CHUNK:]

    def retention_head(c, h, kw_t):
        r0 = c * CHUNK
        qk_cols = slice(h * R_HEAD_QK, (h + 1) * R_HEAD_QK)
        v_cols = slice(h * R_HEAD_V, (h + 1) * R_HEAD_V)
        r_prev = r_state[h]
        lhs = jnp.concatenate(
            [jnp.concatenate([s_b[c, h], ret["qw_b"][r0:r0 + CHUNK, qk_cols]], axis=1),
             jnp.concatenate([kw_t[qk_cols, :], jnp.zeros((R_HEAD_QK, R_HEAD_QK), bf16)], axis=1)], axis=0)
        res = _dot(lhs, jnp.concatenate([v_buf[r0:r0 + CHUNK, v_cols], r_prev.astype(bf16)], axis=0))
        o_h = res[:CHUNK]
        r_state[h] = r_prev * rdec_ref[h] + res[CHUNK:]
        o_buf[r0:r0 + CHUNK, v_cols] = o_h * lax.rsqrt(jnp.mean(o_h * o_h, axis=-1, keepdims=True) + EPS)

    def ssd_output():
        y = y_buf[...] + dskip_ref[...] * xm
        yz = y * _silu(z_buf[...])
        y_m = []
        for g in range(M_GROUPS):
            yz_g = yz[:, g * GROUP_W:(g + 1) * GROUP_W]
            y_m.append(yz_g * lax.rsqrt(jnp.mean(yz_g * yz_g, axis=-1, keepdims=True) + EPS))
        return (jnp.concatenate(y_m, axis=1) * mnw_ref[...]).astype(bf16)

    half = R_HEADS // 2
    n_mid, n_gate = (N_CHUNKS - 1) * R_HEADS, len(gate_fill) - 2
    u_m_tiles = []
    for r in range(N_CHUNKS + 1):
        if r < N_CHUNKS:
            chunk = ssd_chunk(r)
        if r >= 1:
            kw_t = ret["kw"][(r - 1) * CHUNK:r * CHUNK, :].T.astype(bf16)
        if r == N_CHUNKS:
            y_m = ssd_output()
        for i in range(R_HEADS):
            if r < N_CHUNKS:
                ssd_pair(r, i, chunk)
                if r == 0 and i == half:
                    retention_prep()
                if i >= half:
                    score_pair(r, i - half)
            if r >= 1:
                retention_head(r - 1, i, kw_t)
            if r == 0 and i < R_HEADS - 2:
                emit(qkv_fill, 2 if i < 2 else 1)
            elif r == 0:
                emit(gate_fill)
            elif r < N_CHUNKS:
                it = (r - 1) * R_HEADS + i
                emit(gate_fill, (it + 1) * n_gate // n_mid - it * n_gate // n_mid)
            elif i < D_MODEL // MXU_W:
                u_m_tiles.append(_dot(y_m, wpm_ref[:, i * MXU_W:(i + 1) * MXU_W]))
    assert not qkv_fill and not gate_fill and M_HEADS // 2 == R_HEADS
    u_m = jnp.concatenate(u_m_tiles, axis=1)

    u_r = [_dot((_silu(gr_buf[rows, :]) * o_buf[rows, :]).astype(bf16), wpr_ref[...]) for rows in slabs]
    for rows, u_r_slab in zip(slabs, u_r):
        merged = _sigmoid(gam_buf[rows, :]) * u_m[rows, :] + _sigmoid(gar_buf[rows, :]) * u_r_slab
        xo = x_ref[rows, :] + gate * _dot(merged.astype(bf16), wout_ref[...])
        o_ref[rows, :] = xo * lax.rsqrt(jnp.mean(xo * xo, axis=-1, keepdims=True) + EPS) * fnw_ref[...]


def _resident(shape):
    zeros = (0,) * len(shape)
    return pl.BlockSpec(shape, lambda b, s: zeros, pipeline_mode=pl.Buffered(1))


def _tables(seq_len):
    half = R_HEAD_QK // 2
    pos = np.arange(seq_len, dtype=np.float64)
    inv = ROPE_BASE ** (-np.arange(half, dtype=np.float64) / half)
    ang = pos[:, None] * inv[None, :]
    cos_t = np.tile(np.cos(ang), (1, LANES // half))
    sin_t = np.tile(np.concatenate([-np.sin(ang), np.sin(ang)], axis=1), (1, LANES // R_HEAD_QK))
    log_g = np.log1p(-np.exp2(-5.0 - np.arange(R_HEADS, dtype=np.float64)))
    idx = np.arange(CHUNK, dtype=np.float64)
    rel = idx[:, None] - idx[None, :]
    dmat = np.where(rel[None] >= 0, np.exp(np.minimum(rel[None], CHUNK) * log_g[:, None, None]), 0.0)
    dmat = np.concatenate([dmat[0::2], dmat[1::2]], axis=2)
    qdec = np.repeat(np.exp((idx + 1)[:, None] * log_g[None, :]), R_HEAD_QK, axis=1)
    kdec = np.repeat(np.exp((CHUNK - 1 - idx)[:, None] * log_g[None, :]), R_HEAD_QK, axis=1)
    rdec = np.broadcast_to(np.exp(CHUNK * log_g)[:, None, None], (R_HEADS, 1, R_HEAD_V))
    t = np.arange(MXU_W)
    tril = (t[:, None] >= t[None, :]) & (t[:, None] // CHUNK == t[None, :] // CHUNK)
    as_f32 = lambda a: jnp.asarray(np.ascontiguousarray(a, dtype=np.float32))
    as_bf16 = lambda a: jnp.asarray(np.ascontiguousarray(a, dtype=np.float32), dtype=bf16)
    return (as_f32(cos_t), as_f32(sin_t), as_f32(dmat), as_f32(qdec), as_f32(kdec), as_f32(rdec),
            as_bf16(tril))


def _layer(x, c, w_ada, b_ada, norm_w, w_in_t, conv_w, conv_b, dt_bias, a_log, d_skip,
           m_norm_w, w_proj_m, w_proj_r, w_out, out_norm_w):
    bsz, seq_len, d = x.shape
    assert d == D_MODEL and seq_len % TOK == 0

    mod = pl.pallas_call(
        _mod_kernel,
        grid=(3,),
        in_specs=[pl.BlockSpec((bsz, d), lambda j: (0, 0)),
                  pl.BlockSpec((d, d), lambda j: (0, j)),
                  pl.BlockSpec((1, d), lambda j: (0, j))],
        out_specs=pl.BlockSpec((bsz, d), lambda j: (0, j)),
        out_shape=jax.ShapeDtypeStruct((bsz, 3 * d), f32),
        name="adaln_mod",
    )(c, w_ada, b_ada.reshape(1, 3 * d))

    assert w_in_t.shape == (W_COLS - MXU_W + M_HEADS, d)
    pad_heads = lambda p: jnp.pad(p.reshape(1, M_HEADS), ((0, 0), (0, LANES - M_HEADS)))
    dskip_e = jnp.repeat(d_skip, M_HEADDIM).reshape(1, M_INNER)
    cos_t, sin_t, dmat, qdec, kdec, rdec, tril = _tables(seq_len)

    tok_block = lambda w: pl.BlockSpec((TOK, w), lambda b, s: (s, 0))
    return pl.pallas_call(
        _block_kernel,
        grid=(bsz, seq_len // TOK),
        in_specs=[
            pl.BlockSpec((None, TOK, d), lambda b, s: (b, s, 0)),
            _resident((bsz, 3 * d)),
            _resident((1, d)),
            pl.BlockSpec(memory_space=pl.ANY),
            _resident((M_CONV, CONV_DIM)), _resident((1, CONV_DIM)),
            _resident((1, LANES)), _resident((1, LANES)),
            _resident((1, M_INNER)), _resident((1, M_INNER)),
            pl.BlockSpec(memory_space=pl.ANY), pl.BlockSpec(memory_space=pl.ANY),
            pl.BlockSpec(memory_space=pl.ANY),
            _resident((1, d)),
            tok_block(LANES), tok_block(LANES),
            _resident((R_HEADS // 2, CHUNK, 2 * CHUNK)),
            _resident((CHUNK, R_QK)), _resident((CHUNK, R_QK)),
            _resident((R_HEADS, 1, R_HEAD_V)),
            _resident((MXU_W, MXU_W)),
        ],
        out_specs=pl.BlockSpec((None, TOK, d), lambda b, s: (b, s, 0)),
        out_shape=jax.ShapeDtypeStruct((bsz, seq_len, d), x.dtype),
        scratch_shapes=[
            pltpu.VMEM((TOK + 2 * SUBLANES, CONV_DIM), f32),
            pltpu.VMEM((M_GROUPS, M_STATE, GROUP_W), f32),
            pltpu.VMEM((R_HEADS, R_HEAD_QK, R_HEAD_V), f32),
            pltpu.VMEM((TOK, M_INNER), f32),
            pltpu.VMEM((TOK, R_V), f32),
            pltpu.VMEM((TOK, M_INNER), f32),
            pltpu.VMEM((TOK, R_V), f32),
            pltpu.VMEM((TOK, D_MODEL), f32),
            pltpu.VMEM((TOK, D_MODEL), f32),
            pltpu.VMEM((TOK, R_QK), f32),
            pltpu.VMEM((TOK, R_QK), f32),
            pltpu.VMEM((TOK, R_V), bf16),
            pltpu.VMEM((bsz, CONV_DIM + MXU_W), f32),
            pltpu.VMEM((d, W_COLS), bf16),
            pltpu.VMEM((M_INNER, d), bf16), pltpu.VMEM((R_V, d), bf16), pltpu.VMEM((d, d), bf16),
            pltpu.VMEM((W_SLABS_IN_FLIGHT, MXU_W, d), f32),
            pltpu.SemaphoreType.DMA((W_SLABS_IN_FLIGHT,)),
        ],
        compiler_params=pltpu.CompilerParams(
            dimension_semantics=("arbitrary", "arbitrary"),
            vmem_limit_bytes=VMEM_LIMIT_BYTES),
        name="hybrid_block",
    )(x, mod, norm_w.reshape(1, d), w_in_t, conv_w, conv_b.reshape(1, CONV_DIM),
      pad_heads(dt_bias), pad_heads(a_log), dskip_e, m_norm_w.reshape(1, M_INNER),
      w_proj_m, w_proj_r, w_out, out_norm_w.reshape(1, d),
      cos_t, sin_t, dmat, qdec, kdec, rdec, tril)


@jax.jit
def kernel(x, c, w_ada, b_ada, norm_w, w_in, conv_w, conv_b, dt_bias, a_log, d_skip, m_norm_w,
           w_proj_m, w_proj_r, w_out, final_norm_w):
    assert w_ada.shape[0] == 1
    return _layer(x, c, w_ada[0], b_ada[0], norm_w[0], jnp.swapaxes(w_in, 1, 2)[0], conv_w[0], conv_b[0], dt_bias[0],
                  a_log[0], d_skip[0], m_norm_w[0], w_proj_m[0], w_proj_r[0], w_out[0], final_norm_w)
```

```python
import functools

import jax
import jax.numpy as jnp
import numpy as np
from jax import lax
from jax.experimental import pallas as pl
from jax.experimental.pallas import tpu as pltpu

f32 = jnp.float32
bf16 = jnp.bfloat16

D_MODEL = 1024
M_HEADDIM = 64
M_HEADS = 16
M_GROUPS = 2
M_STATE = 64
M_CONV = 4
M_INNER = D_MODEL
GROUP_W = M_INNER // M_GROUPS
HEADS_PER_GROUP = M_HEADS // M_GROUPS
CONV_DIM = M_INNER + 2 * M_GROUPS * M_STATE
R_HEADS = 8
R_QK = 512
R_HEAD_QK = 64
R_V = 1024
R_HEAD_V = 128
CHUNK = 128
ROPE_BASE = 10000.0
EPS = 1e-6
LOG2_E = 1.4426950408889634

LANES = 128
SUBLANES = 8
MXU_W = 256
TOK = 512
N_CHUNKS = TOK // CHUNK
EDGE_ROWS = 256
MOD_COLS = 512
W_SLABS_IN_FLIGHT = 4
VMEM_LIMIT_BYTES = 58 * 1024 * 1024

OFF_Z = 0
OFF_XBC = OFF_Z + M_INNER
OFF_DT = OFF_XBC + CONV_DIM
OFF_Q = OFF_DT + MXU_W
OFF_K = OFF_Q + R_QK
OFF_V = OFF_K + R_QK
OFF_GR = OFF_V + R_V
OFF_GAM = OFF_GR + R_V
OFF_GAR = OFF_GAM + D_MODEL
W_COLS = OFF_GAR + D_MODEL

NT_DIMS = (((1,), (1,)), ((), ()))


def _mod_kernel(c_ref, w_ref, b_ref, o_ref):
    c_hi, c_lo = _split2(c_ref[...])
    w_hi, w_lo = _split2(w_ref[...])
    n = c_hi.shape[0]
    with_w_hi = _dot(jnp.concatenate([c_hi, c_lo], axis=0), w_hi)
    o_ref[...] = with_w_hi[:n] + with_w_hi[n:] + _dot(c_hi, w_lo) + b_ref[...]


def _sigmoid(v):
    return 1.0 / (1.0 + jnp.exp(-v))


def _silu(v):
    return v * _sigmoid(v)


def _softplus(v):
    return jnp.maximum(v, 0.0) + jnp.log1p(jnp.exp(-jnp.abs(v)))


def _dot(a, b):
    return jnp.dot(a, b, preferred_element_type=f32)


def _split2(v):
    hi = v.astype(bf16)
    return hi, (v - hi.astype(f32)).astype(bf16)


def _split3(v):
    hi = v.astype(bf16)
    r1 = v - hi.astype(f32)
    mid = r1.astype(bf16)
    lo = (r1 - mid.astype(f32)).astype(bf16)
    return hi, mid, lo


def _load_weights(wt_hbm, square_hbm, win_ref, square_refs, stage, sems):
    dt_tile = OFF_DT // MXU_W
    jobs = []

    def w_in_tile(w_t, j):
        w = w_t.T
        if j == dt_tile:
            w = jnp.where(lax.broadcasted_iota(jnp.int32, w.shape, 1) < M_HEADS, w, 0.0)
        win_ref[:, j * MXU_W:(j + 1) * MXU_W] = w.astype(bf16)

    def square_rows(w, dst, r0):
        dst[r0:r0 + MXU_W, :] = w.astype(bf16)

    for j in range(W_COLS // MXU_W):
        row0 = j * MXU_W if j <= dt_tile else j * MXU_W - (MXU_W - M_HEADS)
        jobs.append((wt_hbm.at[pl.ds(row0, MXU_W), :], functools.partial(w_in_tile, j=j)))
    for src, dst in zip(square_hbm, square_refs):
        for r0 in range(0, D_MODEL, MXU_W):
            jobs.append((src.at[pl.ds(r0, MXU_W), :], functools.partial(square_rows, dst=dst, r0=r0)))

    def slab_copy(i):
        slot = i % W_SLABS_IN_FLIGHT
        return pltpu.make_async_copy(jobs[i][0], stage.at[slot], sems.at[slot])

    for i in range(W_SLABS_IN_FLIGHT - 1):
        slab_copy(i).start()
    for i in range(len(jobs)):
        if i + W_SLABS_IN_FLIGHT - 1 < len(jobs):
            slab_copy(i + W_SLABS_IN_FLIGHT - 1).start()
        slab_copy(i).wait()
        jobs[i][1](stage[i % W_SLABS_IN_FLIGHT])


def _block_kernel(x_ref, mod_ref, normw_ref, wt_hbm, convw_ref, convb_ref,
                  heads_ref, mnw_ref, wpm_hbm, wpr_hbm, wout_hbm, fnw_ref,
                  cos_ref, sin_ref, dmat_ref, qdec_ref, kdec_ref, rdec_ref, tril_ref,
                  o_ref,
                  xbc_buf, s_state, r_state, y_buf, o_buf, z_buf, gr_buf, gam_buf, gar_buf,
                  q_buf, k_buf, v_buf, win_ref, wpm_ref, wpr_ref, wout_ref, w_stage, w_sems):
    @pl.when(jnp.logical_and(pl.program_id(0) == 0, pl.program_id(1) == 0))
    def _first_step():
        _load_weights(wt_hbm, (wpm_hbm, wpr_hbm, wout_hbm), win_ref, (wpm_ref, wpr_ref, wout_ref), w_stage, w_sems)

    @pl.when(pl.program_id(1) == 0)
    def _start_of_sequence():
        xbc_buf[0:SUBLANES, :] = jnp.zeros((SUBLANES, CONV_DIM), f32)
        s_state[...] = jnp.zeros_like(s_state)
        r_state[...] = jnp.zeros_like(r_state)

    mod = mod_ref[pl.ds(pl.program_id(0), 1), :]
    shift, scale, gate = (mod[:, i * D_MODEL:(i + 1) * D_MODEL] for i in range(3))
    g1 = normw_ref[...] * (1.0 + scale)
    slabs = [slice(r0, r0 + EDGE_ROWS) for r0 in range(0, TOK, EDGE_ROWS)]
    hb_slabs = []
    for rows in slabs:
        x = x_ref[rows, :]
        rs = lax.rsqrt(jnp.mean(x * x, axis=-1, keepdims=True) + EPS)
        hb_slabs.append((x * rs * g1 + shift).astype(bf16))
    hb = jnp.concatenate(hb_slabs, axis=0)

    def proj(off, n):
        return _dot(hb, win_ref[:, off:off + n])

    def proj_tiles(dst, off):
        def tile(c0):
            dst[:, c0:c0 + MXU_W] = proj(off + c0, MXU_W).astype(dst.dtype)
        return [functools.partial(tile, c0) for c0 in range(0, dst.shape[1], MXU_W)]

    def emit(queue, n=1):
        for _ in range(min(n, len(queue))):
            queue.pop(0)()

    qkv_fill = proj_tiles(q_buf, OFF_Q) + proj_tiles(k_buf, OFF_K) + proj_tiles(v_buf, OFF_V)
    gate_fill = (proj_tiles(z_buf, OFF_Z) + proj_tiles(gr_buf, OFF_GR)
                 + proj_tiles(gam_buf, OFF_GAM) + proj_tiles(gar_buf, OFF_GAR))

    xbc_dt = jnp.concatenate([_dot(h, win_ref[:, OFF_XBC:OFF_XBC + CONV_DIM + MXU_W]) for h in hb_slabs],
                             axis=0)
    xbc_buf[SUBLANES:SUBLANES + TOK, :] = xbc_dt[:, :CONV_DIM]
    dt_raw = xbc_dt[:, CONV_DIM:CONV_DIM + LANES]
    emit(gate_fill, 2)

    conv = convb_ref[...] + xbc_buf[SUBLANES:SUBLANES + TOK, :] * convw_ref[M_CONV - 1:M_CONV, :]
    for kk in range(M_CONV - 1):
        r0 = SUBLANES - (M_CONV - 1) + kk
        conv = conv + xbc_buf[r0:r0 + TOK, :] * convw_ref[kk:kk + 1, :]
    xbc_buf[0:SUBLANES, :] = xbc_buf[TOK:TOK + SUBLANES, :]
    xa = _silu(conv)
    xm = xa[:, :M_INNER]
    bmat = xa[:, M_INNER:M_INNER + LANES]
    cmat = xa[:, M_INNER + LANES:CONV_DIM]
    xm_b = xm.astype(bf16)

    lane = lax.broadcasted_iota(jnp.int32, (1, LANES), 1)
    a_neg = jnp.where(lane < M_HEADS, -jnp.exp(heads_ref[1:2, :LANES]), 0.0)
    dt = _softplus(dt_raw + heads_ref[0:1, :LANES])
    a = dt * (a_neg * LOG2_E)
    a_split = jnp.concatenate(_split3(a), axis=1)
    a_parts = jnp.concatenate([_dot(tril_ref[...], a_split[r0:r0 + MXU_W, :]) for r0 in range(0, TOK, MXU_W)],
                              axis=0)
    a_cs = a_parts[:, :LANES] + a_parts[:, LANES:2 * LANES] + a_parts[:, 2 * LANES:]
    emit(gate_fill, 2)

    ii = lax.broadcasted_iota(jnp.int32, (CHUNK, CHUNK), 0)
    jj = lax.broadcasted_iota(jnp.int32, (CHUNK, CHUNK), 1)
    causal = ii >= jj
    left_head = jj < M_HEADDIM

    def blockdiag(slab):
        zero = jnp.zeros_like(slab)
        return jnp.concatenate([jnp.where(left_head, slab, zero), jnp.where(left_head, zero, slab)], axis=0)

    def ssd_chunk(c):
        r0 = c * CHUNK
        acs_c = a_cs[r0:r0 + CHUNK, :]
        acs_t = acs_c.T[:M_HEADS, :]
        dt_t = dt[r0:r0 + CHUNK, :].T[:M_HEADS, :]
        w_t = jnp.exp2(acs_t[:, CHUNK - 1:CHUNK] - acs_t) * dt_t
        src_t = acs_t - jnp.log2(dt_t)
        b_t = bmat[r0:r0 + CHUNK, :].T
        c_c = cmat[r0:r0 + CHUNK, :].astype(bf16)
        cb2 = lax.dot_general(c_c, blockdiag(bmat[r0:r0 + CHUNK, :].astype(bf16)), NT_DIMS,
                              preferred_element_type=f32)
        s_prev = [s_state[g] for g in range(M_GROUPS)]
        y_off = [_dot(c_c[:, g * M_STATE:(g + 1) * M_STATE], s_prev[g].astype(bf16))
                 for g in range(M_GROUPS)]
        return acs_c, src_t, w_t, b_t, [cb2[:, :CHUNK], cb2[:, CHUNK:]], s_prev, y_off

    def ssd_pair(c, pair, chunk):
        acs_c, src_t, w_t, b_t, cb, s_prev, y_off = chunk
        r0 = c * CHUNK
        g = pair // (HEADS_PER_GROUP // 2)
        gslab = slice((pair % (HEADS_PER_GROUP // 2)) * LANES, (pair % (HEADS_PER_GROUP // 2) + 1) * LANES)
        pcols = slice(pair * LANES, (pair + 1) * LANES)
        heads = (2 * pair, 2 * pair + 1)
        cols = [jnp.broadcast_to(acs_c[:, h:h + 1], (CHUNK, CHUNK)) for h in heads]
        w_rows = [(cb[g] * jnp.exp2(jnp.where(causal, col - src_t[h:h + 1, :], -jnp.inf))).astype(bf16)
                  for h, col in zip(heads, cols)]
        bt_g = b_t[g * M_STATE:(g + 1) * M_STATE, :]
        st_rows = [(bt_g * w_t[h:h + 1, :]).astype(bf16) for h in heads]
        lhs = jnp.concatenate([jnp.concatenate(w_rows, axis=1), jnp.concatenate(st_rows, axis=1)], axis=0)
        res = _dot(lhs, blockdiag(xm_b[r0:r0 + CHUNK, pcols]))
        decay = jnp.exp2(jnp.where(left_head, cols[0], cols[1]))
        y_buf[r0:r0 + CHUNK, pcols] = res[:CHUNK] + y_off[g][:, gslab] * decay
        s_state[g, :, gslab] = s_prev[g][:, gslab] * decay[CHUNK - 1:CHUNK, :] + res[CHUNK:]

    ret = {}

    def retention_prep():
        cos = jnp.concatenate([cos_ref[...]] * (R_QK // LANES), axis=1)
        sin = jnp.concatenate([sin_ref[...]] * (R_QK // LANES), axis=1)
        lane_q = lax.broadcasted_iota(jnp.int32, (TOK, R_QK), 1)
        first_half = (lane_q % R_HEAD_QK) < (R_HEAD_QK // 2)

        def rotary(t):
            swapped = jnp.where(first_half, pltpu.roll(t, R_QK - R_HEAD_QK // 2, 1),
                                pltpu.roll(t, R_HEAD_QK // 2, 1))
            return t * cos + swapped * sin

        q = rotary(q_buf[...])
        k = rotary(k_buf[...]) * (R_HEAD_QK ** -0.5)
        qdec = jnp.concatenate([qdec_ref[...]] * N_CHUNKS, axis=0)
        kdec = jnp.concatenate([kdec_ref[...]] * N_CHUNKS, axis=0)
        ret.update(q_b=q.astype(bf16), k_b=k.astype(bf16), qw_b=(q * qdec).astype(bf16), kw=k * kdec)

    s_b = {}

    def score_pair(c, pair):
        r0 = c * CHUNK
        pcols = slice(pair * LANES, (pair + 1) * LANES)
        s_pair = (lax.dot_general(ret["q_b"][r0:r0 + CHUNK, pcols], blockdiag(ret["k_b"][r0:r0 + CHUNK, pcols]),
                                  NT_DIMS, preferred_element_type=f32) * dmat_ref[pair]).astype(bf16)
        s_b[c, 2 * pair], s_b[c, 2 * pair + 1] = s_pair[:, :CHUNK], s_pair[:, CHUNK:]

    def retention_head(c, h, kw_t):
        r0 = c * CHUNK
        qk_cols = slice(h * R_HEAD_QK, (h + 1) * R_HEAD_QK)
        v_cols = slice(h * R_HEAD_V, (h + 1) * R_HEAD_V)
        r_prev = r_state[h]
        lhs = jnp.concatenate(
            [jnp.concatenate([s_b[c, h], ret["qw_b"][r0:r0 + CHUNK, qk_cols]], axis=1),
             jnp.concatenate([kw_t[qk_cols, :], jnp.zeros((R_HEAD_QK, R_HEAD_QK), bf16)], axis=1)], axis=0)
        res = _dot(lhs, jnp.concatenate([v_buf[r0:r0 + CHUNK, v_cols], r_prev.astype(bf16)], axis=0))
        o_h = res[:CHUNK]
        r_state[h] = r_prev * rdec_ref[h] + res[CHUNK:]
        o_buf[r0:r0 + CHUNK, v_cols] = o_h * lax.rsqrt(jnp.mean(o_h * o_h, axis=-1, keepdims=True) + EPS)

    def ssd_output():
        y = y_buf[...] + heads_ref[2:3, :] * xm
        yz = y * _silu(z_buf[...])
        y_m = []
        for g in range(M_GROUPS):
            yz_g = yz[:, g * GROUP_W:(g + 1) * GROUP_W]
            y_m.append(yz_g * lax.rsqrt(jnp.mean(yz_g * yz_g, axis=-1, keepdims=True) + EPS))
        return (jnp.concatenate(y_m, axis=1) * mnw_ref[...]).astype(bf16)

    half = R_HEADS // 2
    n_mid, n_gate = (N_CHUNKS - 1) * R_HEADS, len(gate_fill) - 2
    u_m_tiles = []
    for r in range(N_CHUNKS + 1):
        if r < N_CHUNKS:
            chunk = ssd_chunk(r)
        if r >= 1:
            kw_t = ret["kw"][(r - 1) * CHUNK:r * CHUNK, :].T.astype(bf16)
        if r == N_CHUNKS:
            y_m = ssd_output()
        for i in range(R_HEADS):
            if r < N_CHUNKS:
                ssd_pair(r, i, chunk)
                if r == 0 and i == half:
                    retention_prep()
                if i >= half:
                    score_pair(r, i - half)
            if r >= 1:
                retention_head(r - 1, i, kw_t)
            if r == 0 and i < R_HEADS - 2:
                emit(qkv_fill, 2 if i < 2 else 1)
            elif r == 0:
                emit(gate_fill)
            elif r < N_CHUNKS:
                it = (r - 1) * R_HEADS + i
                emit(gate_fill, (it + 1) * n_gate // n_mid - it * n_gate // n_mid)
            elif i < D_MODEL // MXU_W:
                u_m_tiles.append(_dot(y_m, wpm_ref[:, i * MXU_W:(i + 1) * MXU_W]))
    assert not qkv_fill and not gate_fill and M_HEADS // 2 == R_HEADS
    u_m = jnp.concatenate(u_m_tiles, axis=1)

    u_r = [_dot((_silu(gr_buf[rows, :]) * o_buf[rows, :]).astype(bf16), wpr_ref[...]) for rows in slabs]
    for rows, u_r_slab in zip(slabs, u_r):
        merged = _sigmoid(gam_buf[rows, :]) * u_m[rows, :] + _sigmoid(gar_buf[rows, :]) * u_r_slab
        xo = x_ref[rows, :] + gate * _dot(merged.astype(bf16), wout_ref[...])
        o_ref[rows, :] = xo * lax.rsqrt(jnp.mean(xo * xo, axis=-1, keepdims=True) + EPS) * fnw_ref[...]


def _resident(shape):
    zeros = (0,) * len(shape)
    return pl.BlockSpec(shape, lambda b, s: zeros, pipeline_mode=pl.Buffered(1))


def _tables(seq_len):
    half = R_HEAD_QK // 2
    pos = np.arange(seq_len, dtype=np.float64)
    inv = ROPE_BASE ** (-np.arange(half, dtype=np.float64) / half)
    ang = pos[:, None] * inv[None, :]
    cos_t = np.tile(np.cos(ang), (1, LANES // half))
    sin_t = np.tile(np.concatenate([-np.sin(ang), np.sin(ang)], axis=1), (1, LANES // R_HEAD_QK))
    log_g = np.log1p(-np.exp2(-5.0 - np.arange(R_HEADS, dtype=np.float64)))
    idx = np.arange(CHUNK, dtype=np.float64)
    rel = idx[:, None] - idx[None, :]
    dmat = np.where(rel[None] >= 0, np.exp(np.minimum(rel[None], CHUNK) * log_g[:, None, None]), 0.0)
    dmat = np.concatenate([dmat[0::2], dmat[1::2]], axis=2)
    qdec = np.repeat(np.exp((idx + 1)[:, None] * log_g[None, :]), R_HEAD_QK, axis=1)
    kdec = np.repeat(np.exp((CHUNK - 1 - idx)[:, None] * log_g[None, :]), R_HEAD_QK, axis=1)
    rdec = np.broadcast_to(np.exp(CHUNK * log_g)[:, None, None], (R_HEADS, 1, R_HEAD_V))
    t = np.arange(MXU_W)
    tril = (t[:, None] >= t[None, :]) & (t[:, None] // CHUNK == t[None, :] // CHUNK)
    as_f32 = lambda a: jnp.asarray(np.ascontiguousarray(a, dtype=np.float32))
    as_bf16 = lambda a: jnp.asarray(np.ascontiguousarray(a, dtype=np.float32), dtype=bf16)
    return (as_f32(cos_t), as_f32(sin_t), as_f32(dmat), as_f32(qdec), as_f32(kdec), as_f32(rdec),
            as_bf16(tril))


def _layer(x, c, w_ada, b_ada, norm_w, w_in_t, conv_w, conv_b, dt_bias, a_log, d_skip,
           m_norm_w, w_proj_m, w_proj_r, w_out, out_norm_w):
    bsz, seq_len, d = x.shape
    assert d == D_MODEL and seq_len % TOK == 0

    mod = pl.pallas_call(
        _mod_kernel,
        grid=(3 * d // MOD_COLS,),
        in_specs=[pl.BlockSpec((bsz, d), lambda j: (0, 0)),
                  pl.BlockSpec((d, MOD_COLS), lambda j: (0, j)),
                  pl.BlockSpec((1, MOD_COLS), lambda j: (0, j))],
        out_specs=pl.BlockSpec((bsz, MOD_COLS), lambda j: (0, j)),
        out_shape=jax.ShapeDtypeStruct((bsz, 3 * d), f32),
        name="adaln_mod",
    )(c, w_ada, b_ada.reshape(1, 3 * d))

    assert w_in_t.shape == (W_COLS - MXU_W + M_HEADS, d)
    pad_heads = lambda p: jnp.pad(p, (0, M_INNER - M_HEADS))
    per_head = jnp.stack([pad_heads(dt_bias), pad_heads(a_log), jnp.repeat(d_skip, M_HEADDIM)])
    cos_t, sin_t, dmat, qdec, kdec, rdec, tril = _tables(seq_len)

    tok_block = lambda w: pl.BlockSpec((TOK, w), lambda b, s: (s, 0))
    return pl.pallas_call(
        _block_kernel,
        grid=(bsz, seq_len // TOK),
        in_specs=[
            pl.BlockSpec((None, TOK, d), lambda b, s: (b, s, 0)),
            _resident((bsz, 3 * d)),
            _resident((1, d)),
            pl.BlockSpec(memory_space=pl.ANY),
            _resident((M_CONV, CONV_DIM)), _resident((1, CONV_DIM)),
            _resident((3, M_INNER)), _resident((1, M_INNER)),
            pl.BlockSpec(memory_space=pl.ANY), pl.BlockSpec(memory_space=pl.ANY),
            pl.BlockSpec(memory_space=pl.ANY),
            _resident((1, d)),
            tok_block(LANES), tok_block(LANES),
            _resident((R_HEADS // 2, CHUNK, 2 * CHUNK)),
            _resident((CHUNK, R_QK)), _resident((CHUNK, R_QK)),
            _resident((R_HEADS, 1, R_HEAD_V)),
            _resident((MXU_W, MXU_W)),
        ],
        out_specs=pl.BlockSpec((None, TOK, d), lambda b, s: (b, s, 0)),
        out_shape=jax.ShapeDtypeStruct((bsz, seq_len, d), x.dtype),
        scratch_shapes=[
            pltpu.VMEM((TOK + 2 * SUBLANES, CONV_DIM), f32),
            pltpu.VMEM((M_GROUPS, M_STATE, GROUP_W), f32),
            pltpu.VMEM((R_HEADS, R_HEAD_QK, R_HEAD_V), f32),
            pltpu.VMEM((TOK, M_INNER), f32),
            pltpu.VMEM((TOK, R_V), f32),
            pltpu.VMEM((TOK, M_INNER), f32),
            pltpu.VMEM((TOK, R_V), f32),
            pltpu.VMEM((TOK, D_MODEL), f32),
            pltpu.VMEM((TOK, D_MODEL), f32),
            pltpu.VMEM((TOK, R_QK), f32),
            pltpu.VMEM((TOK, R_QK), f32),
            pltpu.VMEM((TOK, R_V), bf16),
            pltpu.VMEM((d, W_COLS), bf16),
            pltpu.VMEM((M_INNER, d), bf16), pltpu.VMEM((R_V, d), bf16), pltpu.VMEM((d, d), bf16),
            pltpu.VMEM((W_SLABS_IN_FLIGHT, MXU_W, d), f32),
            pltpu.SemaphoreType.DMA((W_SLABS_IN_FLIGHT,)),
        ],
        compiler_params=pltpu.CompilerParams(
            dimension_semantics=("arbitrary", "arbitrary"),
            vmem_limit_bytes=VMEM_LIMIT_BYTES),
        name="hybrid_block",
    )(x, mod, norm_w.reshape(1, d), w_in_t, conv_w, conv_b.reshape(1, CONV_DIM),
      per_head, m_norm_w.reshape(1, M_INNER),
      w_proj_m, w_proj_r, w_out, out_norm_w.reshape(1, d),
      cos_t, sin_t, dmat, qdec, kdec, rdec, tril)


@jax.jit
def kernel(x, c, w_ada, b_ada, norm_w, w_in, conv_w, conv_b, dt_bias, a_log, d_skip, m_norm_w,
           w_proj_m, w_proj_r, w_out, final_norm_w):
    assert w_ada.shape[0] == 1
    return _layer(x, c, w_ada[0], b_ada[0], norm_w[0], jnp.swapaxes(w_in, 1, 2)[0], conv_w[0], conv_b[0], dt_bias[0],
                  a_log[0], d_skip[0], m_norm_w[0], w_proj_m[0], w_proj_r[0], w_out[0], final_norm_w)
```

```python
import functools

import jax
import jax.numpy as jnp
import numpy as np
from jax import lax
from jax.experimental import pallas as pl
from jax.experimental.pallas import tpu as pltpu

f32 = jnp.float32
bf16 = jnp.bfloat16

D_MODEL = 1024
M_HEADDIM = 64
M_HEADS = 16
M_GROUPS = 2
M_STATE = 64
M_CONV = 4
M_INNER = D_MODEL
GROUP_W = M_INNER // M_GROUPS
HEADS_PER_GROUP = M_HEADS // M_GROUPS
CONV_DIM = M_INNER + 2 * M_GROUPS * M_STATE
R_HEADS = 8
R_QK = 512
R_HEAD_QK = 64
R_V = 1024
R_HEAD_V = 128
CHUNK = 128
ROPE_BASE = 10000.0
EPS = 1e-6
LOG2_E = 1.4426950408889634

LANES = 128
SUBLANES = 8
MXU_W = 256
TOK = 512
N_CHUNKS = TOK // CHUNK
EDGE_ROWS = 256
MOD_ROWS = 256
W_SLABS_IN_FLIGHT = 4
VMEM_LIMIT_BYTES = 58 * 1024 * 1024

OFF_Z = 0
OFF_XBC = OFF_Z + M_INNER
OFF_DT = OFF_XBC + CONV_DIM
OFF_Q = OFF_DT + MXU_W
OFF_K = OFF_Q + R_QK
OFF_V = OFF_K + R_QK
OFF_GR = OFF_V + R_V
OFF_GAM = OFF_GR + R_V
OFF_GAR = OFF_GAM + D_MODEL
W_COLS = OFF_GAR + D_MODEL

NT_DIMS = (((1,), (1,)), ((), ()))


def _mod_kernel(c_ref, w_ref, b_ref, o_ref):
    @pl.when(pl.program_id(0) == 0)
    def _start():
        o_ref[...] = jnp.broadcast_to(b_ref[...], o_ref.shape)

    c_hi, c_lo = _split2(c_ref[...])
    w_hi, w_lo = _split2(w_ref[...])
    n = c_hi.shape[0]
    with_w_hi = _dot(jnp.concatenate([c_hi, c_lo], axis=0), w_hi)
    o_ref[...] += with_w_hi[:n] + with_w_hi[n:] + _dot(c_hi, w_lo)


def _sigmoid(v):
    return 1.0 / (1.0 + jnp.exp(-v))


def _silu(v):
    return v * _sigmoid(v)


def _softplus(v):
    return jnp.maximum(v, 0.0) + jnp.log1p(jnp.exp(-jnp.abs(v)))


def _dot(a, b):
    return jnp.dot(a, b, preferred_element_type=f32)


def _split2(v):
    hi = v.astype(bf16)
    return hi, (v - hi.astype(f32)).astype(bf16)


def _split3(v):
    hi = v.astype(bf16)
    r1 = v - hi.astype(f32)
    mid = r1.astype(bf16)
    lo = (r1 - mid.astype(f32)).astype(bf16)
    return hi, mid, lo


def _load_weights(wt_hbm, square_hbm, win_ref, square_refs, stage, sems):
    dt_tile = OFF_DT // MXU_W
    jobs = []

    def w_in_tile(w_t, j):
        w = w_t.T
        if j == dt_tile:
            w = jnp.where(lax.broadcasted_iota(jnp.int32, w.shape, 1) < M_HEADS, w, 0.0)
        win_ref[:, j * MXU_W:(j + 1) * MXU_W] = w.astype(bf16)

    def square_rows(w, dst, r0):
        dst[r0:r0 + MXU_W, :] = w.astype(bf16)

    for j in range(W_COLS // MXU_W):
        row0 = j * MXU_W if j <= dt_tile else j * MXU_W - (MXU_W - M_HEADS)
        jobs.append((wt_hbm.at[pl.ds(row0, MXU_W), :], functools.partial(w_in_tile, j=j)))
    for src, dst in zip(square_hbm, square_refs):
        for r0 in range(0, D_MODEL, MXU_W):
            jobs.append((src.at[pl.ds(r0, MXU_W), :], functools.partial(square_rows, dst=dst, r0=r0)))

    def slab_copy(i):
        slot = i % W_SLABS_IN_FLIGHT
        return pltpu.make_async_copy(jobs[i][0], stage.at[slot], sems.at[slot])

    for i in range(W_SLABS_IN_FLIGHT - 1):
        slab_copy(i).start()
    for i in range(len(jobs)):
        if i + W_SLABS_IN_FLIGHT - 1 < len(jobs):
            slab_copy(i + W_SLABS_IN_FLIGHT - 1).start()
        slab_copy(i).wait()
        jobs[i][1](stage[i % W_SLABS_IN_FLIGHT])


def _block_kernel(x_ref, mod_ref, normw_ref, wt_hbm, convw_ref, convb_ref,
                  heads_ref, mnw_ref, wpm_hbm, wpr_hbm, wout_hbm, fnw_ref,
                  cos_ref, sin_ref, dmat_ref, qdec_ref, kdec_ref, rdec_ref, tril_ref,
                  o_ref,
                  xbc_buf, s_state, r_state, y_buf, o_buf, z_buf, gr_buf, gam_buf, gar_buf,
                  q_buf, k_buf, v_buf, win_ref, wpm_ref, wpr_ref, wout_ref, w_stage, w_sems):
    @pl.when(jnp.logical_and(pl.program_id(0) == 0, pl.program_id(1) == 0))
    def _first_step():
        _load_weights(wt_hbm, (wpm_hbm, wpr_hbm, wout_hbm), win_ref, (wpm_ref, wpr_ref, wout_ref), w_stage, w_sems)

    @pl.when(pl.program_id(1) == 0)
    def _start_of_sequence():
        xbc_buf[0:SUBLANES, :] = jnp.zeros((SUBLANES, CONV_DIM), f32)
        s_state[...] = jnp.zeros_like(s_state)
        r_state[...] = jnp.zeros_like(r_state)

    mod = mod_ref[pl.ds(pl.program_id(0), 1), :]
    shift, scale, gate = (mod[:, i * D_MODEL:(i + 1) * D_MODEL] for i in range(3))
    g1 = normw_ref[...] * (1.0 + scale)
    slabs = [slice(r0, r0 + EDGE_ROWS) for r0 in range(0, TOK, EDGE_ROWS)]
    hb_slabs = []
    for rows in slabs:
        x = x_ref[rows, :]
        rs = lax.rsqrt(jnp.mean(x * x, axis=-1, keepdims=True) + EPS)
        hb_slabs.append((x * rs * g1 + shift).astype(bf16))
    hb = jnp.concatenate(hb_slabs, axis=0)

    def proj(off, n):
        return _dot(hb, win_ref[:, off:off + n])

    def proj_tiles(dst, off):
        def tile(c0):
            dst[:, c0:c0 + MXU_W] = proj(off + c0, MXU_W).astype(dst.dtype)
        return [functools.partial(tile, c0) for c0 in range(0, dst.shape[1], MXU_W)]

    def emit(queue, n=1):
        for _ in range(min(n, len(queue))):
            queue.pop(0)()

    qkv_fill = proj_tiles(q_buf, OFF_Q) + proj_tiles(k_buf, OFF_K) + proj_tiles(v_buf, OFF_V)
    gate_fill = (proj_tiles(z_buf, OFF_Z) + proj_tiles(gr_buf, OFF_GR)
                 + proj_tiles(gam_buf, OFF_GAM) + proj_tiles(gar_buf, OFF_GAR))

    xbc_dt = jnp.concatenate([_dot(h, win_ref[:, OFF_XBC:OFF_XBC + CONV_DIM + MXU_W]) for h in hb_slabs],
                             axis=0)
    xbc_buf[SUBLANES:SUBLANES + TOK, :] = xbc_dt[:, :CONV_DIM]
    dt_raw = xbc_dt[:, CONV_DIM:CONV_DIM + LANES]
    emit(gate_fill, 2)

    conv = convb_ref[...] + xbc_buf[SUBLANES:SUBLANES + TOK, :] * convw_ref[M_CONV - 1:M_CONV, :]
    for kk in range(M_CONV - 1):
        r0 = SUBLANES - (M_CONV - 1) + kk
        conv = conv + xbc_buf[r0:r0 + TOK, :] * convw_ref[kk:kk + 1, :]
    xbc_buf[0:SUBLANES, :] = xbc_buf[TOK:TOK + SUBLANES, :]
    xa = _silu(conv)
    xm = xa[:, :M_INNER]
    bmat = xa[:, M_INNER:M_INNER + LANES]
    cmat = xa[:, M_INNER + LANES:CONV_DIM]
    xm_b = xm.astype(bf16)

    lane = lax.broadcasted_iota(jnp.int32, (1, LANES), 1)
    a_neg = jnp.where(lane < M_HEADS, -jnp.exp(heads_ref[1:2, :LANES]), 0.0)
    dt = _softplus(dt_raw + heads_ref[0:1, :LANES])
    a = dt * (a_neg * LOG2_E)
    a_split = jnp.concatenate(_split3(a), axis=1)
    a_parts = jnp.concatenate([_dot(tril_ref[...], a_split[r0:r0 + MXU_W, :]) for r0 in range(0, TOK, MXU_W)],
                              axis=0)
    a_cs = a_parts[:, :LANES] + a_parts[:, LANES:2 * LANES] + a_parts[:, 2 * LANES:]
    emit(gate_fill, 2)

    ii = lax.broadcasted_iota(jnp.int32, (CHUNK, CHUNK), 0)
    jj = lax.broadcasted_iota(jnp.int32, (CHUNK, CHUNK), 1)
    causal = ii >= jj
    left_head = jj < M_HEADDIM

    def blockdiag(slab):
        zero = jnp.zeros_like(slab)
        return jnp.concatenate([jnp.where(left_head, slab, zero), jnp.where(left_head, zero, slab)], axis=0)

    def ssd_chunk(c):
        r0 = c * CHUNK
        acs_c = a_cs[r0:r0 + CHUNK, :]
        acs_t = acs_c.T[:M_HEADS, :]
        dt_t = dt[r0:r0 + CHUNK, :].T[:M_HEADS, :]
        w_t = jnp.exp2(acs_t[:, CHUNK - 1:CHUNK] - acs_t) * dt_t
        src_t = acs_t - jnp.log2(dt_t)
        b_t = bmat[r0:r0 + CHUNK, :].T
        c_c = cmat[r0:r0 + CHUNK, :].astype(bf16)
        cb2 = lax.dot_general(c_c, blockdiag(bmat[r0:r0 + CHUNK, :].astype(bf16)), NT_DIMS,
                              preferred_element_type=f32)
        s_prev = [s_state[g] for g in range(M_GROUPS)]
        y_off = [_dot(c_c[:, g * M_STATE:(g + 1) * M_STATE], s_prev[g].astype(bf16))
                 for g in range(M_GROUPS)]
        return acs_c, src_t, w_t, b_t, [cb2[:, :CHUNK], cb2[:, CHUNK:]], s_prev, y_off

    def ssd_pair(c, pair, chunk):
        acs_c, src_t, w_t, b_t, cb, s_prev, y_off = chunk
        r0 = c * CHUNK
        g = pair // (HEADS_PER_GROUP // 2)
        gslab = slice((pair % (HEADS_PER_GROUP // 2)) * LANES, (pair % (HEADS_PER_GROUP // 2) + 1) * LANES)
        pcols = slice(pair * LANES, (pair + 1) * LANES)
        heads = (2 * pair, 2 * pair + 1)
        cols = [jnp.broadcast_to(acs_c[:, h:h + 1], (CHUNK, CHUNK)) for h in heads]
        w_rows = [(cb[g] * jnp.exp2(jnp.where(causal, col - src_t[h:h + 1, :], -jnp.inf))).astype(bf16)
                  for h, col in zip(heads, cols)]
        bt_g = b_t[g * M_STATE:(g + 1) * M_STATE, :]
        st_rows = [(bt_g * w_t[h:h + 1, :]).astype(bf16) for h in heads]
        lhs = jnp.concatenate([jnp.concatenate(w_rows, axis=1), jnp.concatenate(st_rows, axis=1)], axis=0)
        res = _dot(lhs, blockdiag(xm_b[r0:r0 + CHUNK, pcols]))
        decay = jnp.exp2(jnp.where(left_head, cols[0], cols[1]))
        y_buf[r0:r0 + CHUNK, pcols] = res[:CHUNK] + y_off[g][:, gslab] * decay
        s_state[g, :, gslab] = s_prev[g][:, gslab] * decay[CHUNK - 1:CHUNK, :] + res[CHUNK:]

    ret = {}

    def retention_prep():
        cos = jnp.concatenate([cos_ref[...]] * (R_QK // LANES), axis=1)
        sin = jnp.concatenate([sin_ref[...]] * (R_QK // LANES), axis=1)
        lane_q = lax.broadcasted_iota(jnp.int32, (TOK, R_QK), 1)
        first_half = (lane_q % R_HEAD_QK) < (R_HEAD_QK // 2)

        def rotary(t):
            swapped = jnp.where(first_half, pltpu.roll(t, R_QK - R_HEAD_QK // 2, 1),
                                pltpu.roll(t, R_HEAD_QK // 2, 1))
            return t * cos + swapped * sin

        q = rotary(q_buf[...])
        k = rotary(k_buf[...]) * (R_HEAD_QK ** -0.5)
        qdec = jnp.concatenate([qdec_ref[...]] * N_CHUNKS, axis=0)
        kdec = jnp.concatenate([kdec_ref[...]] * N_CHUNKS, axis=0)
        ret.update(q_b=q.astype(bf16), k_b=k.astype(bf16), qw_b=(q * qdec).astype(bf16), kw=k * kdec)

    s_b = {}

    def score_pair(c, pair):
        r0 = c * CHUNK
        pcols = slice(pair * LANES, (pair + 1) * LANES)
        s_pair = (lax.dot_general(ret["q_b"][r0:r0 + CHUNK, pcols], blockdiag(ret["k_b"][r0:r0 + CHUNK, pcols]),
                                  NT_DIMS, preferred_element_type=f32) * dmat_ref[pair]).astype(bf16)
        s_b[c, 2 * pair], s_b[c, 2 * pair + 1] = s_pair[:, :CHUNK], s_pair[:, CHUNK:]

    def retention_head(c, h, kw_t):
        r0 = c * CHUNK
        qk_cols = slice(h * R_HEAD_QK, (h + 1) * R_HEAD_QK)
        v_cols = slice(h * R_HEAD_V, (h + 1) * R_HEAD_V)
        r_prev = r_state[h]
        lhs = jnp.concatenate(
            [jnp.concatenate([s_b[c, h], ret["qw_b"][r0:r0 + CHUNK, qk_cols]], axis=1),
             jnp.concatenate([kw_t[qk_cols, :], jnp.zeros((R_HEAD_QK, R_HEAD_QK), bf16)], axis=1)], axis=0)
        res = _dot(lhs, jnp.concatenate([v_buf[r0:r0 + CHUNK, v_cols], r_prev.astype(bf16)], axis=0))
        o_h = res[:CHUNK]
        r_state[h] = r_prev * rdec_ref[h] + res[CHUNK:]
        o_buf[r0:r0 + CHUNK, v_cols] = o_h * lax.rsqrt(jnp.mean(o_h * o_h, axis=-1, keepdims=True) + EPS)

    def ssd_output():
        y = y_buf[...] + heads_ref[2:3, :] * xm
        yz = y * _silu(z_buf[...])
        y_m = []
        for g in range(M_GROUPS):
            yz_g = yz[:, g * GROUP_W:(g + 1) * GROUP_W]
            y_m.append(yz_g * lax.rsqrt(jnp.mean(yz_g * yz_g, axis=-1, keepdims=True) + EPS))
        return (jnp.concatenate(y_m, axis=1) * mnw_ref[...]).astype(bf16)

    half = R_HEADS // 2
    n_mid, n_gate = (N_CHUNKS - 1) * R_HEADS, len(gate_fill) - 2
    u_m_tiles = []
    for r in range(N_CHUNKS + 1):
        if r < N_CHUNKS:
            chunk = ssd_chunk(r)
        if r >= 1:
            kw_t = ret["kw"][(r - 1) * CHUNK:r * CHUNK, :].T.astype(bf16)
        if r == N_CHUNKS:
            y_m = ssd_output()
        for i in range(R_HEADS):
            if r < N_CHUNKS:
                ssd_pair(r, i, chunk)
                if r == 0 and i == half:
                    retention_prep()
                if i >= half:
                    score_pair(r, i - half)
            if r >= 1:
                retention_head(r - 1, i, kw_t)
            if r == 0 and i < R_HEADS - 2:
                emit(qkv_fill, 2 if i < 2 else 1)
            elif r == 0:
                emit(gate_fill)
            elif r < N_CHUNKS:
                it = (r - 1) * R_HEADS + i
                emit(gate_fill, (it + 1) * n_gate // n_mid - it * n_gate // n_mid)
            elif i < D_MODEL // MXU_W:
                u_m_tiles.append(_dot(y_m, wpm_ref[:, i * MXU_W:(i + 1) * MXU_W]))
    assert not qkv_fill and not gate_fill and M_HEADS // 2 == R_HEADS
    u_m = jnp.concatenate(u_m_tiles, axis=1)

    u_r = [_dot((_silu(gr_buf[rows, :]) * o_buf[rows, :]).astype(bf16), wpr_ref[...]) for rows in slabs]
    for rows, u_r_slab in zip(slabs, u_r):
        merged = _sigmoid(gam_buf[rows, :]) * u_m[rows, :] + _sigmoid(gar_buf[rows, :]) * u_r_slab
        xo = x_ref[rows, :] + gate * _dot(merged.astype(bf16), wout_ref[...])
        o_ref[rows, :] = xo * lax.rsqrt(jnp.mean(xo * xo, axis=-1, keepdims=True) + EPS) * fnw_ref[...]


def _resident(shape):
    zeros = (0,) * len(shape)
    return pl.BlockSpec(shape, lambda b, s: zeros, pipeline_mode=pl.Buffered(1))


def _tables(seq_len):
    half = R_HEAD_QK // 2
    pos = np.arange(seq_len, dtype=np.float64)
    inv = ROPE_BASE ** (-np.arange(half, dtype=np.float64) / half)
    ang = pos[:, None] * inv[None, :]
    cos_t = np.tile(np.cos(ang), (1, LANES // half))
    sin_t = np.tile(np.concatenate([-np.sin(ang), np.sin(ang)], axis=1), (1, LANES // R_HEAD_QK))
    log_g = np.log1p(-np.exp2(-5.0 - np.arange(R_HEADS, dtype=np.float64)))
    idx = np.arange(CHUNK, dtype=np.float64)
    rel = idx[:, None] - idx[None, :]
    dmat = np.where(rel[None] >= 0, np.exp(np.minimum(rel[None], CHUNK) * log_g[:, None, None]), 0.0)
    dmat = np.concatenate([dmat[0::2], dmat[1::2]], axis=2)
    qdec = np.repeat(np.exp((idx + 1)[:, None] * log_g[None, :]), R_HEAD_QK, axis=1)
    kdec = np.repeat(np.exp((CHUNK - 1 - idx)[:, None] * log_g[None, :]), R_HEAD_QK, axis=1)
    rdec = np.broadcast_to(np.exp(CHUNK * log_g)[:, None, None], (R_HEADS, 1, R_HEAD_V))
    t = np.arange(MXU_W)
    tril = (t[:, None] >= t[None, :]) & (t[:, None] // CHUNK == t[None, :] // CHUNK)
    as_f32 = lambda a: jnp.asarray(np.ascontiguousarray(a, dtype=np.float32))
    as_bf16 = lambda a: jnp.asarray(np.ascontiguousarray(a, dtype=np.float32), dtype=bf16)
    return (as_f32(cos_t), as_f32(sin_t), as_f32(dmat), as_f32(qdec), as_f32(kdec), as_f32(rdec),
            as_bf16(tril))


def _layer(x, c, w_ada, b_ada, norm_w, w_in_t, conv_w, conv_b, dt_bias, a_log, d_skip,
           m_norm_w, w_proj_m, w_proj_r, w_out, out_norm_w):
    bsz, seq_len, d = x.shape
    assert d == D_MODEL and seq_len % TOK == 0

    mod = pl.pallas_call(
        _mod_kernel,
        grid=(d // MOD_ROWS,),
        in_specs=[pl.BlockSpec((bsz, MOD_ROWS), lambda k: (0, k)),
                  pl.BlockSpec((MOD_ROWS, 3 * d), lambda k: (k, 0)),
                  pl.BlockSpec((1, 3 * d), lambda k: (0, 0))],
        out_specs=pl.BlockSpec((bsz, 3 * d), lambda k: (0, 0)),
        out_shape=jax.ShapeDtypeStruct((bsz, 3 * d), f32),
        name="adaln_mod",
    )(c, w_ada, b_ada.reshape(1, 3 * d))

    assert w_in_t.shape == (W_COLS - MXU_W + M_HEADS, d)
    pad_heads = lambda p: jnp.pad(p, (0, M_INNER - M_HEADS))
    per_head = jnp.stack([pad_heads(dt_bias), pad_heads(a_log), jnp.repeat(d_skip, M_HEADDIM)])
    cos_t, sin_t, dmat, qdec, kdec, rdec, tril = _tables(seq_len)

    tok_block = lambda w: pl.BlockSpec((TOK, w), lambda b, s: (s, 0))
    return pl.pallas_call(
        _block_kernel,
        grid=(bsz, seq_len // TOK),
        in_specs=[
            pl.BlockSpec((None, TOK, d), lambda b, s: (b, s, 0)),
            _resident((bsz, 3 * d)),
            _resident((1, d)),
            pl.BlockSpec(memory_space=pl.ANY),
            _resident((M_CONV, CONV_DIM)), _resident((1, CONV_DIM)),
            _resident((3, M_INNER)), _resident((1, M_INNER)),
            pl.BlockSpec(memory_space=pl.ANY), pl.BlockSpec(memory_space=pl.ANY),
            pl.BlockSpec(memory_space=pl.ANY),
            _resident((1, d)),
            tok_block(LANES), tok_block(LANES),
            _resident((R_HEADS // 2, CHUNK, 2 * CHUNK)),
            _resident((CHUNK, R_QK)), _resident((CHUNK, R_QK)),
            _resident((R_HEADS, 1, R_HEAD_V)),
            _resident((MXU_W, MXU_W)),
        ],
        out_specs=pl.BlockSpec((None, TOK, d), lambda b, s: (b, s, 0)),
        out_shape=jax.ShapeDtypeStruct((bsz, seq_len, d), x.dtype),
        scratch_shapes=[
            pltpu.VMEM((TOK + 2 * SUBLANES, CONV_DIM), f32),
            pltpu.VMEM((M_GROUPS, M_STATE, GROUP_W), f32),
            pltpu.VMEM((R_HEADS, R_HEAD_QK, R_HEAD_V), f32),
            pltpu.VMEM((TOK, M_INNER), f32),
            pltpu.VMEM((TOK, R_V), f32),
            pltpu.VMEM((TOK, M_INNER), f32),
            pltpu.VMEM((TOK, R_V), f32),
            pltpu.VMEM((TOK, D_MODEL), f32),
            pltpu.VMEM((TOK, D_MODEL), f32),
            pltpu.VMEM((TOK, R_QK), f32),
            pltpu.VMEM((TOK, R_QK), f32),
            pltpu.VMEM((TOK, R_V), bf16),
            pltpu.VMEM((d, W_COLS), bf16),
            pltpu.VMEM((M_INNER, d), bf16), pltpu.VMEM((R_V, d), bf16), pltpu.VMEM((d, d), bf16),
            pltpu.VMEM((W_SLABS_IN_FLIGHT, MXU_W, d), f32),
            pltpu.SemaphoreType.DMA((W_SLABS_IN_FLIGHT,)),
        ],
        compiler_params=pltpu.CompilerParams(
            dimension_semantics=("arbitrary", "arbitrary"),
            vmem_limit_bytes=VMEM_LIMIT_BYTES),
        name="hybrid_block",
    )(x, mod, norm_w.reshape(1, d), w_in_t, conv_w, conv_b.reshape(1, CONV_DIM),
      per_head, m_norm_w.reshape(1, M_INNER),
      w_proj_m, w_proj_r, w_out, out_norm_w.reshape(1, d),
      cos_t, sin_t, dmat, qdec, kdec, rdec, tril)


@jax.jit
def kernel(x, c, w_ada, b_ada, norm_w, w_in, conv_w, conv_b, dt_bias, a_log, d_skip, m_norm_w,
           w_proj_m, w_proj_r, w_out, final_norm_w):
    assert w_ada.shape[0] == 1
    return _layer(x, c, w_ada[0], b_ada[0], norm_w[0], jnp.swapaxes(w_in, 1, 2)[0], conv_w[0], conv_b[0], dt_bias[0],
                  a_log[0], d_skip[0], m_norm_w[0], w_proj_m[0], w_proj_r[0], w_out[0], final_norm_w)
```

```python
import functools

import jax
import jax.numpy as jnp
import numpy as np
from jax import lax
from jax.experimental import pallas as pl
from jax.experimental.pallas import tpu as pltpu

f32 = jnp.float32
bf16 = jnp.bfloat16

D_MODEL = 1024
M_HEADDIM = 64
M_HEADS = 16
M_GROUPS = 2
M_STATE = 64
M_CONV = 4
M_INNER = D_MODEL
GROUP_W = M_INNER // M_GROUPS
HEADS_PER_GROUP = M_HEADS // M_GROUPS
CONV_DIM = M_INNER + 2 * M_GROUPS * M_STATE
R_HEADS = 8
R_QK = 512
R_HEAD_QK = 64
R_V = 1024
R_HEAD_V = 128
CHUNK = 128
ROPE_BASE = 10000.0
EPS = 1e-6
LOG2_E = 1.4426950408889634

LANES = 128
SUBLANES = 8
MXU_W = 256
TOK = 512
N_CHUNKS = TOK // CHUNK
EDGE_ROWS = 128
MOD_ROWS = 256
W_SLABS_IN_FLIGHT = 4
VMEM_LIMIT_BYTES = 58 * 1024 * 1024

OFF_Z = 0
OFF_XBC = OFF_Z + M_INNER
OFF_DT = OFF_XBC + CONV_DIM
OFF_Q = OFF_DT + MXU_W
OFF_K = OFF_Q + R_QK
OFF_V = OFF_K + R_QK
OFF_GR = OFF_V + R_V
OFF_GAM = OFF_GR + R_V
OFF_GAR = OFF_GAM + D_MODEL
W_COLS = OFF_GAR + D_MODEL

NT_DIMS = (((1,), (1,)), ((), ()))


def _mod_kernel(c_ref, w_ref, b_ref, o_ref):
    @pl.when(pl.program_id(0) == 0)
    def _start():
        o_ref[...] = jnp.broadcast_to(b_ref[...], o_ref.shape)

    c_hi, c_lo = _split2(c_ref[...])
    w_hi, w_lo = _split2(w_ref[...])
    n = c_hi.shape[0]
    with_w_hi = _dot(jnp.concatenate([c_hi, c_lo], axis=0), w_hi)
    o_ref[...] += with_w_hi[:n] + with_w_hi[n:] + _dot(c_hi, w_lo)


def _sigmoid(v):
    return 1.0 / (1.0 + jnp.exp(-v))


def _silu(v):
    return v * _sigmoid(v)


def _softplus(v):
    return jnp.maximum(v, 0.0) + jnp.log1p(jnp.exp(-jnp.abs(v)))


def _dot(a, b):
    return jnp.dot(a, b, preferred_element_type=f32)


def _split2(v):
    hi = v.astype(bf16)
    return hi, (v - hi.astype(f32)).astype(bf16)


def _split3(v):
    hi = v.astype(bf16)
    r1 = v - hi.astype(f32)
    mid = r1.astype(bf16)
    lo = (r1 - mid.astype(f32)).astype(bf16)
    return hi, mid, lo


def _load_weights(wt_hbm, square_hbm, win_ref, square_refs, stage, sems):
    dt_tile = OFF_DT // MXU_W
    jobs = []

    def w_in_tile(w_t, j):
        w = w_t.T
        if j == dt_tile:
            w = jnp.where(lax.broadcasted_iota(jnp.int32, w.shape, 1) < M_HEADS, w, 0.0)
        win_ref[:, j * MXU_W:(j + 1) * MXU_W] = w.astype(bf16)

    def square_rows(w, dst, r0):
        dst[r0:r0 + MXU_W, :] = w.astype(bf16)

    for j in range(W_COLS // MXU_W):
        row0 = j * MXU_W if j <= dt_tile else j * MXU_W - (MXU_W - M_HEADS)
        jobs.append((wt_hbm.at[pl.ds(row0, MXU_W), :], functools.partial(w_in_tile, j=j)))
    for src, dst in zip(square_hbm, square_refs):
        for r0 in range(0, D_MODEL, MXU_W):
            jobs.append((src.at[pl.ds(r0, MXU_W), :], functools.partial(square_rows, dst=dst, r0=r0)))

    def slab_copy(i):
        slot = i % W_SLABS_IN_FLIGHT
        return pltpu.make_async_copy(jobs[i][0], stage.at[slot], sems.at[slot])

    for i in range(W_SLABS_IN_FLIGHT - 1):
        slab_copy(i).start()
    for i in range(len(jobs)):
        if i + W_SLABS_IN_FLIGHT - 1 < len(jobs):
            slab_copy(i + W_SLABS_IN_FLIGHT - 1).start()
        slab_copy(i).wait()
        jobs[i][1](stage[i % W_SLABS_IN_FLIGHT])


def _block_kernel(x_ref, mod_ref, normw_ref, wt_hbm, convw_ref, convb_ref,
                  heads_ref, mnw_ref, wpm_hbm, wpr_hbm, wout_hbm, fnw_ref,
                  cos_ref, sin_ref, dmat_ref, qdec_ref, kdec_ref, rdec_ref, tril_ref,
                  o_ref,
                  xbc_buf, s_state, r_state, y_buf, o_buf, z_buf, gr_buf, gam_buf, gar_buf,
                  q_buf, k_buf, v_buf, win_ref, wpm_ref, wpr_ref, wout_ref, w_stage, w_sems):
    @pl.when(jnp.logical_and(pl.program_id(0) == 0, pl.program_id(1) == 0))
    def _first_step():
        _load_weights(wt_hbm, (wpm_hbm, wpr_hbm, wout_hbm), win_ref, (wpm_ref, wpr_ref, wout_ref), w_stage, w_sems)

    @pl.when(pl.program_id(1) == 0)
    def _start_of_sequence():
        xbc_buf[0:SUBLANES, :] = jnp.zeros((SUBLANES, CONV_DIM), f32)
        s_state[...] = jnp.zeros_like(s_state)
        r_state[...] = jnp.zeros_like(r_state)

    mod = mod_ref[pl.ds(pl.program_id(0), 1), :]
    shift, scale, gate = (mod[:, i * D_MODEL:(i + 1) * D_MODEL] for i in range(3))
    g1 = normw_ref[...] * (1.0 + scale)
    slabs = [slice(r0, r0 + EDGE_ROWS) for r0 in range(0, TOK, EDGE_ROWS)]
    hb_slabs = []
    for rows in slabs:
        x = x_ref[rows, :]
        rs = lax.rsqrt(jnp.mean(x * x, axis=-1, keepdims=True) + EPS)
        hb_slabs.append((x * rs * g1 + shift).astype(bf16))
    hb = jnp.concatenate(hb_slabs, axis=0)

    def proj(off, n):
        return _dot(hb, win_ref[:, off:off + n])

    def proj_tiles(dst, off):
        def tile(c0):
            dst[:, c0:c0 + MXU_W] = proj(off + c0, MXU_W).astype(dst.dtype)
        return [functools.partial(tile, c0) for c0 in range(0, dst.shape[1], MXU_W)]

    def emit(queue, n=1):
        for _ in range(min(n, len(queue))):
            queue.pop(0)()

    qkv_fill = proj_tiles(q_buf, OFF_Q) + proj_tiles(k_buf, OFF_K) + proj_tiles(v_buf, OFF_V)
    gate_fill = (proj_tiles(z_buf, OFF_Z) + proj_tiles(gr_buf, OFF_GR)
                 + proj_tiles(gam_buf, OFF_GAM) + proj_tiles(gar_buf, OFF_GAR))

    xbc_dt = jnp.concatenate([_dot(h, win_ref[:, OFF_XBC:OFF_XBC + CONV_DIM + MXU_W]) for h in hb_slabs],
                             axis=0)
    xbc_buf[SUBLANES:SUBLANES + TOK, :] = xbc_dt[:, :CONV_DIM]
    dt_raw = xbc_dt[:, CONV_DIM:CONV_DIM + LANES]
    emit(gate_fill, 2)

    conv = convb_ref[...] + xbc_buf[SUBLANES:SUBLANES + TOK, :] * convw_ref[M_CONV - 1:M_CONV, :]
    for kk in range(M_CONV - 1):
        r0 = SUBLANES - (M_CONV - 1) + kk
        conv = conv + xbc_buf[r0:r0 + TOK, :] * convw_ref[kk:kk + 1, :]
    xbc_buf[0:SUBLANES, :] = xbc_buf[TOK:TOK + SUBLANES, :]
    xa = _silu(conv)
    xm = xa[:, :M_INNER]
    bmat = xa[:, M_INNER:M_INNER + LANES]
    cmat = xa[:, M_INNER + LANES:CONV_DIM]
    xm_b = xm.astype(bf16)

    lane = lax.broadcasted_iota(jnp.int32, (1, LANES), 1)
    a_neg = jnp.where(lane < M_HEADS, -jnp.exp(heads_ref[1:2, :LANES]), 0.0)
    dt = _softplus(dt_raw + heads_ref[0:1, :LANES])
    a = dt * (a_neg * LOG2_E)
    a_split = jnp.concatenate(_split3(a), axis=1)
    a_parts = jnp.concatenate([_dot(tril_ref[...], a_split[r0:r0 + MXU_W, :]) for r0 in range(0, TOK, MXU_W)],
                              axis=0)
    a_cs = a_parts[:, :LANES] + a_parts[:, LANES:2 * LANES] + a_parts[:, 2 * LANES:]
    emit(gate_fill, 2)

    ii = lax.broadcasted_iota(jnp.int32, (CHUNK, CHUNK), 0)
    jj = lax.broadcasted_iota(jnp.int32, (CHUNK, CHUNK), 1)
    causal = ii >= jj
    left_head = jj < M_HEADDIM

    def blockdiag(slab):
        zero = jnp.zeros_like(slab)
        return jnp.concatenate([jnp.where(left_head, slab, zero), jnp.where(left_head, zero, slab)], axis=0)

    def ssd_chunk(c):
        r0 = c * CHUNK
        acs_c = a_cs[r0:r0 + CHUNK, :]
        acs_t = acs_c.T[:M_HEADS, :]
        dt_t = dt[r0:r0 + CHUNK, :].T[:M_HEADS, :]
        w_t = jnp.exp2(acs_t[:, CHUNK - 1:CHUNK] - acs_t) * dt_t
        src_t = acs_t - jnp.log2(dt_t)
        b_t = bmat[r0:r0 + CHUNK, :].T
        c_c = cmat[r0:r0 + CHUNK, :].astype(bf16)
        cb2 = lax.dot_general(c_c, blockdiag(bmat[r0:r0 + CHUNK, :].astype(bf16)), NT_DIMS,
                              preferred_element_type=f32)
        s_prev = [s_state[g] for g in range(M_GROUPS)]
        y_off = [_dot(c_c[:, g * M_STATE:(g + 1) * M_STATE], s_prev[g].astype(bf16))
                 for g in range(M_GROUPS)]
        return acs_c, src_t, w_t, b_t, [cb2[:, :CHUNK], cb2[:, CHUNK:]], s_prev, y_off

    def ssd_pair(c, pair, chunk):
        acs_c, src_t, w_t, b_t, cb, s_prev, y_off = chunk
        r0 = c * CHUNK
        g = pair // (HEADS_PER_GROUP // 2)
        gslab = slice((pair % (HEADS_PER_GROUP // 2)) * LANES, (pair % (HEADS_PER_GROUP // 2) + 1) * LANES)
        pcols = slice(pair * LANES, (pair + 1) * LANES)
        heads = (2 * pair, 2 * pair + 1)
        cols = [jnp.broadcast_to(acs_c[:, h:h + 1], (CHUNK, CHUNK)) for h in heads]
        w_rows = [(cb[g] * jnp.exp2(jnp.where(causal, col - src_t[h:h + 1, :], -jnp.inf))).astype(bf16)
                  for h, col in zip(heads, cols)]
        bt_g = b_t[g * M_STATE:(g + 1) * M_STATE, :]
        st_rows = [(bt_g * w_t[h:h + 1, :]).astype(bf16) for h in heads]
        lhs = jnp.concatenate([jnp.concatenate(w_rows, axis=1), jnp.concatenate(st_rows, axis=1)], axis=0)
        res = _dot(lhs, blockdiag(xm_b[r0:r0 + CHUNK, pcols]))
        decay = jnp.exp2(jnp.where(left_head, cols[0], cols[1]))
        y_buf[r0:r0 + CHUNK, pcols] = res[:CHUNK] + y_off[g][:, gslab] * decay
        s_state[g, :, gslab] = s_prev[g][:, gslab] * decay[CHUNK - 1:CHUNK, :] + res[CHUNK:]

    ret = {}

    def retention_prep():
        cos = jnp.concatenate([cos_ref[...]] * (R_QK // LANES), axis=1)
        sin = jnp.concatenate([sin_ref[...]] * (R_QK // LANES), axis=1)
        lane_q = lax.broadcasted_iota(jnp.int32, (TOK, R_QK), 1)
        first_half = (lane_q % R_HEAD_QK) < (R_HEAD_QK // 2)

        def rotary(t):
            swapped = jnp.where(first_half, pltpu.roll(t, R_QK - R_HEAD_QK // 2, 1),
                                pltpu.roll(t, R_HEAD_QK // 2, 1))
            return t * cos + swapped * sin

        q = rotary(q_buf[...])
        k = rotary(k_buf[...])
        qdec = jnp.concatenate([qdec_ref[...]] * N_CHUNKS, axis=0)
        kdec = jnp.concatenate([kdec_ref[...]] * N_CHUNKS, axis=0)
        ret.update(q_b=q.astype(bf16), k_b=k.astype(bf16), qw_b=(q * qdec).astype(bf16), kw=k * kdec)

    s_b = {}

    def score_pair(c, pair):
        r0 = c * CHUNK
        pcols = slice(pair * LANES, (pair + 1) * LANES)
        s_pair = (lax.dot_general(ret["q_b"][r0:r0 + CHUNK, pcols], blockdiag(ret["k_b"][r0:r0 + CHUNK, pcols]),
                                  NT_DIMS, preferred_element_type=f32) * dmat_ref[pair]).astype(bf16)
        s_b[c, 2 * pair], s_b[c, 2 * pair + 1] = s_pair[:, :CHUNK], s_pair[:, CHUNK:]

    def retention_head(c, h, kw_t):
        r0 = c * CHUNK
        qk_cols = slice(h * R_HEAD_QK, (h + 1) * R_HEAD_QK)
        v_cols = slice(h * R_HEAD_V, (h + 1) * R_HEAD_V)
        r_prev = r_state[h]
        lhs = jnp.concatenate(
            [jnp.concatenate([s_b[c, h], ret["qw_b"][r0:r0 + CHUNK, qk_cols]], axis=1),
             jnp.concatenate([kw_t[qk_cols, :], jnp.zeros((R_HEAD_QK, R_HEAD_QK), bf16)], axis=1)], axis=0)
        res = _dot(lhs, jnp.concatenate([v_buf[r0:r0 + CHUNK, v_cols], r_prev.astype(bf16)], axis=0))
        o_h = res[:CHUNK]
        r_state[h] = r_prev * rdec_ref[h] + res[CHUNK:]
        o_buf[r0:r0 + CHUNK, v_cols] = o_h * lax.rsqrt(jnp.mean(o_h * o_h, axis=-1, keepdims=True) + EPS)

    def ssd_output():
        y = y_buf[...] + heads_ref[2:3, :] * xm
        yz = y * _silu(z_buf[...])
        y_m = []
        for g in range(M_GROUPS):
            yz_g = yz[:, g * GROUP_W:(g + 1) * GROUP_W]
            y_m.append(yz_g * lax.rsqrt(jnp.mean(yz_g * yz_g, axis=-1, keepdims=True) + EPS))
        return (jnp.concatenate(y_m, axis=1) * mnw_ref[...]).astype(bf16)

    half = R_HEADS // 2
    n_mid, n_gate = (N_CHUNKS - 1) * R_HEADS, len(gate_fill) - 2
    u_m_tiles = []
    for r in range(N_CHUNKS + 1):
        if r < N_CHUNKS:
            chunk = ssd_chunk(r)
        if r >= 1:
            kw_t = ret["kw"][(r - 1) * CHUNK:r * CHUNK, :].T.astype(bf16)
        if r == N_CHUNKS:
            y_m = ssd_output()
        for i in range(R_HEADS):
            if r < N_CHUNKS:
                ssd_pair(r, i, chunk)
                if r == 0 and i == half:
                    retention_prep()
                if i >= half:
                    score_pair(r, i - half)
            if r >= 1:
                retention_head(r - 1, i, kw_t)
            if r == 0 and i < R_HEADS - 2:
                emit(qkv_fill, 2 if i < 2 else 1)
            elif r == 0:
                emit(gate_fill)
            elif r < N_CHUNKS:
                it = (r - 1) * R_HEADS + i
                emit(gate_fill, (it + 1) * n_gate // n_mid - it * n_gate // n_mid)
            elif i < D_MODEL // MXU_W:
                u_m_tiles.append(_dot(y_m, wpm_ref[:, i * MXU_W:(i + 1) * MXU_W]))
    assert not qkv_fill and not gate_fill and M_HEADS // 2 == R_HEADS
    u_m = jnp.concatenate(u_m_tiles, axis=1)

    u_r = [_dot((_silu(gr_buf[rows, :]) * o_buf[rows, :]).astype(bf16), wpr_ref[...]) for rows in slabs]
    for rows, u_r_slab in zip(slabs, u_r):
        merged = _sigmoid(gam_buf[rows, :]) * u_m[rows, :] + _sigmoid(gar_buf[rows, :]) * u_r_slab
        xo = x_ref[rows, :] + gate * _dot(merged.astype(bf16), wout_ref[...])
        o_ref[rows, :] = xo * lax.rsqrt(jnp.mean(xo * xo, axis=-1, keepdims=True) + EPS) * fnw_ref[...]


def _resident(shape):
    zeros = (0,) * len(shape)
    return pl.BlockSpec(shape, lambda b, s: zeros, pipeline_mode=pl.Buffered(1))


def _tables(seq_len):
    half = R_HEAD_QK // 2
    pos = np.arange(seq_len, dtype=np.float64)
    inv = ROPE_BASE ** (-np.arange(half, dtype=np.float64) / half)
    ang = pos[:, None] * inv[None, :]
    cos_t = np.tile(np.cos(ang), (1, LANES // half))
    sin_t = np.tile(np.concatenate([-np.sin(ang), np.sin(ang)], axis=1), (1, LANES // R_HEAD_QK))
    log_g = np.log1p(-np.exp2(-5.0 - np.arange(R_HEADS, dtype=np.float64)))
    idx = np.arange(CHUNK, dtype=np.float64)
    rel = idx[:, None] - idx[None, :]
    dmat = np.where(rel[None] >= 0, np.exp(np.minimum(rel[None], CHUNK) * log_g[:, None, None]), 0.0)
    dmat = np.concatenate([dmat[0::2], dmat[1::2]], axis=2) * R_HEAD_QK ** -0.5
    qdec = np.repeat(np.exp((idx + 1)[:, None] * log_g[None, :]), R_HEAD_QK, axis=1)
    kdec = np.repeat(np.exp((CHUNK - 1 - idx)[:, None] * log_g[None, :]), R_HEAD_QK, axis=1) * R_HEAD_QK ** -0.5
    rdec = np.broadcast_to(np.exp(CHUNK * log_g)[:, None, None], (R_HEADS, 1, R_HEAD_V))
    t = np.arange(MXU_W)
    tril = (t[:, None] >= t[None, :]) & (t[:, None] // CHUNK == t[None, :] // CHUNK)
    as_f32 = lambda a: jnp.asarray(np.ascontiguousarray(a, dtype=np.float32))
    as_bf16 = lambda a: jnp.asarray(np.ascontiguousarray(a, dtype=np.float32), dtype=bf16)
    return (as_f32(cos_t), as_f32(sin_t), as_f32(dmat), as_f32(qdec), as_f32(kdec), as_f32(rdec),
            as_bf16(tril))


def _layer(x, c, w_ada, b_ada, norm_w, w_in_t, conv_w, conv_b, dt_bias, a_log, d_skip,
           m_norm_w, w_proj_m, w_proj_r, w_out, out_norm_w):
    bsz, seq_len, d = x.shape
    assert d == D_MODEL and seq_len % TOK == 0

    mod = pl.pallas_call(
        _mod_kernel,
        grid=(d // MOD_ROWS,),
        in_specs=[pl.BlockSpec((bsz, MOD_ROWS), lambda k: (0, k)),
                  pl.BlockSpec((MOD_ROWS, 3 * d), lambda k: (k, 0)),
                  pl.BlockSpec((1, 3 * d), lambda k: (0, 0))],
        out_specs=pl.BlockSpec((bsz, 3 * d), lambda k: (0, 0)),
        out_shape=jax.ShapeDtypeStruct((bsz, 3 * d), f32),
        name="adaln_mod",
    )(c, w_ada, b_ada.reshape(1, 3 * d))

    assert w_in_t.shape == (W_COLS - MXU_W + M_HEADS, d)
    pad_heads = lambda p: jnp.pad(p, (0, M_INNER - M_HEADS))
    per_head = jnp.stack([pad_heads(dt_bias), pad_heads(a_log), jnp.repeat(d_skip, M_HEADDIM)])
    cos_t, sin_t, dmat, qdec, kdec, rdec, tril = _tables(seq_len)

    tok_block = lambda w: pl.BlockSpec((TOK, w), lambda b, s: (s, 0))
    return pl.pallas_call(
        _block_kernel,
        grid=(bsz, seq_len // TOK),
        in_specs=[
            pl.BlockSpec((None, TOK, d), lambda b, s: (b, s, 0)),
            _resident((bsz, 3 * d)),
            _resident((1, d)),
            pl.BlockSpec(memory_space=pl.ANY),
            _resident((M_CONV, CONV_DIM)), _resident((1, CONV_DIM)),
            _resident((3, M_INNER)), _resident((1, M_INNER)),
            pl.BlockSpec(memory_space=pl.ANY), pl.BlockSpec(memory_space=pl.ANY),
            pl.BlockSpec(memory_space=pl.ANY),
            _resident((1, d)),
            tok_block(LANES), tok_block(LANES),
            _resident((R_HEADS // 2, CHUNK, 2 * CHUNK)),
            _resident((CHUNK, R_QK)), _resident((CHUNK, R_QK)),
            _resident((R_HEADS, 1, R_HEAD_V)),
            _resident((MXU_W, MXU_W)),
        ],
        out_specs=pl.BlockSpec((None, TOK, d), lambda b, s: (b, s, 0)),
        out_shape=jax.ShapeDtypeStruct((bsz, seq_len, d), x.dtype),
        scratch_shapes=[
            pltpu.VMEM((TOK + 2 * SUBLANES, CONV_DIM), f32),
            pltpu.VMEM((M_GROUPS, M_STATE, GROUP_W), f32),
            pltpu.VMEM((R_HEADS, R_HEAD_QK, R_HEAD_V), f32),
            pltpu.VMEM((TOK, M_INNER), f32),
            pltpu.VMEM((TOK, R_V), f32),
            pltpu.VMEM((TOK, M_INNER), f32),
            pltpu.VMEM((TOK, R_V), f32),
            pltpu.VMEM((TOK, D_MODEL), f32),
            pltpu.VMEM((TOK, D_MODEL), f32),
            pltpu.VMEM((TOK, R_QK), f32),
            pltpu.VMEM((TOK, R_QK), f32),
            pltpu.VMEM((TOK, R_V), bf16),
            pltpu.VMEM((d, W_COLS), bf16),
            pltpu.VMEM((M_INNER, d), bf16), pltpu.VMEM((R_V, d), bf16), pltpu.VMEM((d, d), bf16),
            pltpu.VMEM((W_SLABS_IN_FLIGHT, MXU_W, d), f32),
            pltpu.SemaphoreType.DMA((W_SLABS_IN_FLIGHT,)),
        ],
        compiler_params=pltpu.CompilerParams(
            dimension_semantics=("arbitrary", "arbitrary"),
            vmem_limit_bytes=VMEM_LIMIT_BYTES),
        name="hybrid_block",
    )(x, mod, norm_w.reshape(1, d), w_in_t, conv_w, conv_b.reshape(1, CONV_DIM),
      per_head, m_norm_w.reshape(1, M_INNER),
      w_proj_m, w_proj_r, w_out, out_norm_w.reshape(1, d),
      cos_t, sin_t, dmat, qdec, kdec, rdec, tril)


@jax.jit
def kernel(x, c, w_ada, b_ada, norm_w, w_in, conv_w, conv_b, dt_bias, a_log, d_skip, m_norm_w,
           w_proj_m, w_proj_r, w_out, final_norm_w):
    assert w_ada.shape[0] == 1
    return _layer(x, c, w_ada[0], b_ada[0], norm_w[0], jnp.swapaxes(w_in, 1, 2)[0], conv_w[0], conv_b[0], dt_bias[0],
                  a_log[0], d_skip[0], m_norm_w[0], w_proj_m[0], w_proj_r[0], w_out[0], final_norm_w)
```

```python
import functools

import jax
import jax.numpy as jnp
import numpy as np
from jax import lax
from jax.experimental import pallas as pl
from jax.experimental.pallas import tpu as pltpu

f32 = jnp.float32
bf16 = jnp.bfloat16

D_MODEL = 1024
M_HEADDIM = 64
M_HEADS = 16
M_GROUPS = 2
M_STATE = 64
M_CONV = 4
M_INNER = D_MODEL
GROUP_W = M_INNER // M_GROUPS
HEADS_PER_GROUP = M_HEADS // M_GROUPS
CONV_DIM = M_INNER + 2 * M_GROUPS * M_STATE
R_HEADS = 8
R_QK = 512
R_HEAD_QK = 64
R_V = 1024
R_HEAD_V = 128
CHUNK = 128
ROPE_BASE = 10000.0
EPS = 1e-6
LOG2_E = 1.4426950408889634

LANES = 128
SUBLANES = 8
MXU_W = 256
TOK = 512
N_CHUNKS = TOK // CHUNK
EDGE_ROWS = 512
MOD_ROWS = 256
W_SLABS_IN_FLIGHT = 4
VMEM_LIMIT_BYTES = 58 * 1024 * 1024

OFF_Z = 0
OFF_XBC = OFF_Z + M_INNER
OFF_DT = OFF_XBC + CONV_DIM
OFF_Q = OFF_DT + MXU_W
OFF_K = OFF_Q + R_QK
OFF_V = OFF_K + R_QK
OFF_GR = OFF_V + R_V
OFF_GAM = OFF_GR + R_V
OFF_GAR = OFF_GAM + D_MODEL
W_COLS = OFF_GAR + D_MODEL

NT_DIMS = (((1,), (1,)), ((), ()))


def _mod_kernel(c_ref, w_ref, b_ref, o_ref):
    @pl.when(pl.program_id(0) == 0)
    def _start():
        o_ref[...] = jnp.broadcast_to(b_ref[...], o_ref.shape)

    c_hi, c_lo = _split2(c_ref[...])
    w_hi, w_lo = _split2(w_ref[...])
    n = c_hi.shape[0]
    with_w_hi = _dot(jnp.concatenate([c_hi, c_lo], axis=0), w_hi)
    o_ref[...] += with_w_hi[:n] + with_w_hi[n:] + _dot(c_hi, w_lo)


def _sigmoid(v):
    return 1.0 / (1.0 + jnp.exp(-v))


def _silu(v):
    return v * _sigmoid(v)


def _softplus(v):
    return jnp.maximum(v, 0.0) + jnp.log1p(jnp.exp(-jnp.abs(v)))


def _dot(a, b):
    return jnp.dot(a, b, preferred_element_type=f32)


def _split2(v):
    hi = v.astype(bf16)
    return hi, (v - hi.astype(f32)).astype(bf16)


def _split3(v):
    hi = v.astype(bf16)
    r1 = v - hi.astype(f32)
    mid = r1.astype(bf16)
    lo = (r1 - mid.astype(f32)).astype(bf16)
    return hi, mid, lo


def _load_weights(wt_hbm, square_hbm, win_ref, square_refs, stage, sems):
    dt_tile = OFF_DT // MXU_W
    jobs = []

    def w_in_tile(w_t, j):
        w = w_t.T
        if j == dt_tile:
            w = jnp.where(lax.broadcasted_iota(jnp.int32, w.shape, 1) < M_HEADS, w, 0.0)
        win_ref[:, j * MXU_W:(j + 1) * MXU_W] = w.astype(bf16)

    def square_rows(w, dst, r0):
        dst[r0:r0 + MXU_W, :] = w.astype(bf16)

    for j in range(W_COLS // MXU_W):
        row0 = j * MXU_W if j <= dt_tile else j * MXU_W - (MXU_W - M_HEADS)
        jobs.append((wt_hbm.at[pl.ds(row0, MXU_W), :], functools.partial(w_in_tile, j=j)))
    for src, dst in zip(square_hbm, square_refs):
        for r0 in range(0, D_MODEL, MXU_W):
            jobs.append((src.at[pl.ds(r0, MXU_W), :], functools.partial(square_rows, dst=dst, r0=r0)))

    def slab_copy(i):
        slot = i % W_SLABS_IN_FLIGHT
        return pltpu.make_async_copy(jobs[i][0], stage.at[slot], sems.at[slot])

    for i in range(W_SLABS_IN_FLIGHT - 1):
        slab_copy(i).start()
    for i in range(len(jobs)):
        if i + W_SLABS_IN_FLIGHT - 1 < len(jobs):
            slab_copy(i + W_SLABS_IN_FLIGHT - 1).start()
        slab_copy(i).wait()
        jobs[i][1](stage[i % W_SLABS_IN_FLIGHT])


def _block_kernel(x_ref, mod_ref, normw_ref, wt_hbm, convw_ref, convb_ref,
                  heads_ref, mnw_ref, wpm_hbm, wpr_hbm, wout_hbm, fnw_ref,
                  cos_ref, sin_ref, dmat_ref, qdec_ref, kdec_ref, rdec_ref, tril_ref,
                  o_ref,
                  xbc_buf, s_state, r_state, y_buf, o_buf, z_buf, gr_buf, gam_buf, gar_buf,
                  q_buf, k_buf, v_buf, win_ref, wpm_ref, wpr_ref, wout_ref, w_stage, w_sems):
    @pl.when(jnp.logical_and(pl.program_id(0) == 0, pl.program_id(1) == 0))
    def _first_step():
        _load_weights(wt_hbm, (wpm_hbm, wpr_hbm, wout_hbm), win_ref, (wpm_ref, wpr_ref, wout_ref), w_stage, w_sems)

    @pl.when(pl.program_id(1) == 0)
    def _start_of_sequence():
        xbc_buf[0:SUBLANES, :] = jnp.zeros((SUBLANES, CONV_DIM), f32)
        s_state[...] = jnp.zeros_like(s_state)
        r_state[...] = jnp.zeros_like(r_state)

    mod = mod_ref[pl.ds(pl.program_id(0), 1), :]
    shift, scale, gate = (mod[:, i * D_MODEL:(i + 1) * D_MODEL] for i in range(3))
    g1 = normw_ref[...] * (1.0 + scale)
    slabs = [slice(r0, r0 + EDGE_ROWS) for r0 in range(0, TOK, EDGE_ROWS)]
    hb_slabs = []
    for rows in slabs:
        x = x_ref[rows, :]
        rs = lax.rsqrt(jnp.mean(x * x, axis=-1, keepdims=True) + EPS)
        hb_slabs.append((x * rs * g1 + shift).astype(bf16))
    hb = jnp.concatenate(hb_slabs, axis=0)

    def proj(off, n):
        return _dot(hb, win_ref[:, off:off + n])

    def proj_tiles(dst, off):
        def tile(c0):
            dst[:, c0:c0 + MXU_W] = proj(off + c0, MXU_W).astype(dst.dtype)
        return [functools.partial(tile, c0) for c0 in range(0, dst.shape[1], MXU_W)]

    def emit(queue, n=1):
        for _ in range(min(n, len(queue))):
            queue.pop(0)()

    qkv_fill = proj_tiles(q_buf, OFF_Q) + proj_tiles(k_buf, OFF_K) + proj_tiles(v_buf, OFF_V)
    gate_fill = (proj_tiles(z_buf, OFF_Z) + proj_tiles(gr_buf, OFF_GR)
                 + proj_tiles(gam_buf, OFF_GAM) + proj_tiles(gar_buf, OFF_GAR))

    xbc_dt = jnp.concatenate([_dot(h, win_ref[:, OFF_XBC:OFF_XBC + CONV_DIM + MXU_W]) for h in hb_slabs],
                             axis=0)
    xbc_buf[SUBLANES:SUBLANES + TOK, :] = xbc_dt[:, :CONV_DIM]
    dt_raw = xbc_dt[:, CONV_DIM:CONV_DIM + LANES]
    emit(gate_fill, 2)

    conv = convb_ref[...] + xbc_buf[SUBLANES:SUBLANES + TOK, :] * convw_ref[M_CONV - 1:M_CONV, :]
    for kk in range(M_CONV - 1):
        r0 = SUBLANES - (M_CONV - 1) + kk
        conv = conv + xbc_buf[r0:r0 + TOK, :] * convw_ref[kk:kk + 1, :]
    xbc_buf[0:SUBLANES, :] = xbc_buf[TOK:TOK + SUBLANES, :]
    xa = _silu(conv)
    xm = xa[:, :M_INNER]
    bmat = xa[:, M_INNER:M_INNER + LANES]
    cmat = xa[:, M_INNER + LANES:CONV_DIM]
    xm_b = xm.astype(bf16)

    lane = lax.broadcasted_iota(jnp.int32, (1, LANES), 1)
    a_neg = jnp.where(lane < M_HEADS, -jnp.exp(heads_ref[1:2, :LANES]), 0.0)
    dt = _softplus(dt_raw + heads_ref[0:1, :LANES])
    a = dt * (a_neg * LOG2_E)
    a_split = jnp.concatenate(_split3(a), axis=1)
    a_parts = jnp.concatenate([_dot(tril_ref[...], a_split[r0:r0 + MXU_W, :]) for r0 in range(0, TOK, MXU_W)],
                              axis=0)
    a_cs = a_parts[:, :LANES] + a_parts[:, LANES:2 * LANES] + a_parts[:, 2 * LANES:]
    emit(gate_fill, 2)

    ii = lax.broadcasted_iota(jnp.int32, (CHUNK, CHUNK), 0)
    jj = lax.broadcasted_iota(jnp.int32, (CHUNK, CHUNK), 1)
    causal = ii >= jj
    left_head = jj < M_HEADDIM

    def blockdiag(slab):
        zero = jnp.zeros_like(slab)
        return jnp.concatenate([jnp.where(left_head, slab, zero), jnp.where(left_head, zero, slab)], axis=0)

    def ssd_chunk(c):
        r0 = c * CHUNK
        acs_c = a_cs[r0:r0 + CHUNK, :]
        acs_t = acs_c.T[:M_HEADS, :]
        dt_t = dt[r0:r0 + CHUNK, :].T[:M_HEADS, :]
        w_t = jnp.exp2(acs_t[:, CHUNK - 1:CHUNK] - acs_t) * dt_t
        src_t = acs_t - jnp.log2(dt_t)
        b_t = bmat[r0:r0 + CHUNK, :].T
        c_c = cmat[r0:r0 + CHUNK, :].astype(bf16)
        cb2 = lax.dot_general(c_c, blockdiag(bmat[r0:r0 + CHUNK, :].astype(bf16)), NT_DIMS,
                              preferred_element_type=f32)
        s_prev = [s_state[g] for g in range(M_GROUPS)]
        y_off = [_dot(c_c[:, g * M_STATE:(g + 1) * M_STATE], s_prev[g].astype(bf16))
                 for g in range(M_GROUPS)]
        return acs_c, src_t, w_t, b_t, [cb2[:, :CHUNK], cb2[:, CHUNK:]], s_prev, y_off

    def ssd_pair(c, pair, chunk):
        acs_c, src_t, w_t, b_t, cb, s_prev, y_off = chunk
        r0 = c * CHUNK
        g = pair // (HEADS_PER_GROUP // 2)
        gslab = slice((pair % (HEADS_PER_GROUP // 2)) * LANES, (pair % (HEADS_PER_GROUP // 2) + 1) * LANES)
        pcols = slice(pair * LANES, (pair + 1) * LANES)
        heads = (2 * pair, 2 * pair + 1)
        cols = [jnp.broadcast_to(acs_c[:, h:h + 1], (CHUNK, CHUNK)) for h in heads]
        w_rows = [(cb[g] * jnp.exp2(jnp.where(causal, col - src_t[h:h + 1, :], -jnp.inf))).astype(bf16)
                  for h, col in zip(heads, cols)]
        bt_g = b_t[g * M_STATE:(g + 1) * M_STATE, :]
        st_rows = [(bt_g * w_t[h:h + 1, :]).astype(bf16) for h in heads]
        lhs = jnp.concatenate([jnp.concatenate(w_rows, axis=1), jnp.concatenate(st_rows, axis=1)], axis=0)
        res = _dot(lhs, blockdiag(xm_b[r0:r0 + CHUNK, pcols]))
        decay = jnp.exp2(jnp.where(left_head, cols[0], cols[1]))
        y_buf[r0:r0 + CHUNK, pcols] = res[:CHUNK] + y_off[g][:, gslab] * decay
        s_state[g, :, gslab] = s_prev[g][:, gslab] * decay[CHUNK - 1:CHUNK, :] + res[CHUNK:]

    ret = {}

    def retention_prep():
        cos = jnp.concatenate([cos_ref[...]] * (R_QK // LANES), axis=1)
        sin = jnp.concatenate([sin_ref[...]] * (R_QK // LANES), axis=1)
        lane_q = lax.broadcasted_iota(jnp.int32, (TOK, R_QK), 1)
        first_half = (lane_q % R_HEAD_QK) < (R_HEAD_QK // 2)

        def rotary(t):
            swapped = jnp.where(first_half, pltpu.roll(t, R_QK - R_HEAD_QK // 2, 1),
                                pltpu.roll(t, R_HEAD_QK // 2, 1))
            return t * cos + swapped * sin

        q = rotary(q_buf[...])
        k = rotary(k_buf[...]) * (R_HEAD_QK ** -0.5)
        qdec = jnp.concatenate([qdec_ref[...]] * N_CHUNKS, axis=0)
        kdec = jnp.concatenate([kdec_ref[...]] * N_CHUNKS, axis=0)
        ret.update(q_b=q.astype(bf16), k_b=k.astype(bf16), qw_b=(q * qdec).astype(bf16), kw=k * kdec)

    s_b = {}

    def score_pair(c, pair):
        r0 = c * CHUNK
        pcols = slice(pair * LANES, (pair + 1) * LANES)
        s_pair = (lax.dot_general(ret["q_b"][r0:r0 + CHUNK, pcols], blockdiag(ret["k_b"][r0:r0 + CHUNK, pcols]),
                                  NT_DIMS, preferred_element_type=f32) * dmat_ref[pair]).astype(bf16)
        s_b[c, 2 * pair], s_b[c, 2 * pair + 1] = s_pair[:, :CHUNK], s_pair[:, CHUNK:]

    def retention_head(c, h, kw_t):
        r0 = c * CHUNK
        qk_cols = slice(h * R_HEAD_QK, (h + 1) * R_HEAD_QK)
        v_cols = slice(h * R_HEAD_V, (h + 1) * R_HEAD_V)
        r_prev = r_state[h]
        lhs = jnp.concatenate(
            [jnp.concatenate([s_b[c, h], ret["qw_b"][r0:r0 + CHUNK, qk_cols]], axis=1),
             jnp.concatenate([kw_t[qk_cols, :], jnp.zeros((R_HEAD_QK, R_HEAD_QK), bf16)], axis=1)], axis=0)
        res = _dot(lhs, jnp.concatenate([v_buf[r0:r0 + CHUNK, v_cols], r_prev.astype(bf16)], axis=0))
        o_h = res[:CHUNK]
        r_state[h] = r_prev * rdec_ref[h] + res[CHUNK:]
        o_buf[r0:r0 + CHUNK, v_cols] = o_h * lax.rsqrt(jnp.mean(o_h * o_h, axis=-1, keepdims=True) + EPS)

    def ssd_output():
        y = y_buf[...] + heads_ref[2:3, :] * xm
        yz = y * _silu(z_buf[...])
        y_m = []
        for g in range(M_GROUPS):
            yz_g = yz[:, g * GROUP_W:(g + 1) * GROUP_W]
            y_m.append(yz_g * lax.rsqrt(jnp.mean(yz_g * yz_g, axis=-1, keepdims=True) + EPS))
        return (jnp.concatenate(y_m, axis=1) * mnw_ref[...]).astype(bf16)

    half = R_HEADS // 2
    n_mid, n_gate = (N_CHUNKS - 1) * R_HEADS, len(gate_fill) - 2
    u_m_tiles = []
    for r in range(N_CHUNKS + 1):
        if r < N_CHUNKS:
            chunk = ssd_chunk(r)
        if r >= 1:
            kw_t = ret["kw"][(r - 1) * CHUNK:r * CHUNK, :].T.astype(bf16)
        if r == N_CHUNKS:
            y_m = ssd_output()
        for i in range(R_HEADS):
            if r < N_CHUNKS:
                ssd_pair(r, i, chunk)
                if r == 0 and i == half:
                    retention_prep()
                if i >= half:
                    score_pair(r, i - half)
            if r >= 1:
                retention_head(r - 1, i, kw_t)
            if r == 0 and i < R_HEADS - 2:
                emit(qkv_fill, 2 if i < 2 else 1)
            elif r == 0:
                emit(gate_fill)
            elif r < N_CHUNKS:
                it = (r - 1) * R_HEADS + i
                emit(gate_fill, (it + 1) * n_gate // n_mid - it * n_gate // n_mid)
            elif i < D_MODEL // MXU_W:
                u_m_tiles.append(_dot(y_m, wpm_ref[:, i * MXU_W:(i + 1) * MXU_W]))
    assert not qkv_fill and not gate_fill and M_HEADS // 2 == R_HEADS
    u_m = jnp.concatenate(u_m_tiles, axis=1)

    u_r = [_dot((_silu(gr_buf[rows, :]) * o_buf[rows, :]).astype(bf16), wpr_ref[...]) for rows in slabs]
    for rows, u_r_slab in zip(slabs, u_r):
        merged = _sigmoid(gam_buf[rows, :]) * u_m[rows, :] + _sigmoid(gar_buf[rows, :]) * u_r_slab
        xo = x_ref[rows, :] + gate * _dot(merged.astype(bf16), wout_ref[...])
        o_ref[rows, :] = xo * lax.rsqrt(jnp.mean(xo * xo, axis=-1, keepdims=True) + EPS) * fnw_ref[...]


def _resident(shape):
    zeros = (0,) * len(shape)
    return pl.BlockSpec(shape, lambda b, s: zeros, pipeline_mode=pl.Buffered(1))


def _tables(seq_len):
    half = R_HEAD_QK // 2
    pos = np.arange(seq_len, dtype=np.float64)
    inv = ROPE_BASE ** (-np.arange(half, dtype=np.float64) / half)
    ang = pos[:, None] * inv[None, :]
    cos_t = np.tile(np.cos(ang), (1, LANES // half))
    sin_t = np.tile(np.concatenate([-np.sin(ang), np.sin(ang)], axis=1), (1, LANES // R_HEAD_QK))
    log_g = np.log1p(-np.exp2(-5.0 - np.arange(R_HEADS, dtype=np.float64)))
    idx = np.arange(CHUNK, dtype=np.float64)
    rel = idx[:, None] - idx[None, :]
    dmat = np.where(rel[None] >= 0, np.exp(np.minimum(rel[None], CHUNK) * log_g[:, None, None]), 0.0)
    dmat = np.concatenate([dmat[0::2], dmat[1::2]], axis=2)
    qdec = np.repeat(np.exp((idx + 1)[:, None] * log_g[None, :]), R_HEAD_QK, axis=1)
    kdec = np.repeat(np.exp((CHUNK - 1 - idx)[:, None] * log_g[None, :]), R_HEAD_QK, axis=1)
    rdec = np.broadcast_to(np.exp(CHUNK * log_g)[:, None, None], (R_HEADS, 1, R_HEAD_V))
    t = np.arange(MXU_W)
    tril = (t[:, None] >= t[None, :]) & (t[:, None] // CHUNK == t[None, :] // CHUNK)
    as_f32 = lambda a: jnp.asarray(np.ascontiguousarray(a, dtype=np.float32))
    as_bf16 = lambda a: jnp.asarray(np.ascontiguousarray(a, dtype=np.float32), dtype=bf16)
    return (as_f32(cos_t), as_f32(sin_t), as_f32(dmat), as_f32(qdec), as_f32(kdec), as_f32(rdec),
            as_bf16(tril))


def _layer(x, c, w_ada, b_ada, norm_w, w_in_t, conv_w, conv_b, dt_bias, a_log, d_skip,
           m_norm_w, w_proj_m, w_proj_r, w_out, out_norm_w):
    bsz, seq_len, d = x.shape
    assert d == D_MODEL and seq_len % TOK == 0

    mod = pl.pallas_call(
        _mod_kernel,
        grid=(d // MOD_ROWS,),
        in_specs=[pl.BlockSpec((bsz, MOD_ROWS), lambda k: (0, k)),
                  pl.BlockSpec((MOD_ROWS, 3 * d), lambda k: (k, 0)),
                  pl.BlockSpec((1, 3 * d), lambda k: (0, 0))],
        out_specs=pl.BlockSpec((bsz, 3 * d), lambda k: (0, 0)),
        out_shape=jax.ShapeDtypeStruct((bsz, 3 * d), f32),
        name="adaln_mod",
    )(c, w_ada, b_ada.reshape(1, 3 * d))

    assert w_in_t.shape == (W_COLS - MXU_W + M_HEADS, d)
    pad_heads = lambda p: jnp.pad(p, (0, M_INNER - M_HEADS))
    per_head = jnp.stack([pad_heads(dt_bias), pad_heads(a_log), jnp.repeat(d_skip, M_HEADDIM)])
    cos_t, sin_t, dmat, qdec, kdec, rdec, tril = _tables(seq_len)

    tok_block = lambda w: pl.BlockSpec((TOK, w), lambda b, s: (s, 0))
    return pl.pallas_call(
        _block_kernel,
        grid=(bsz, seq_len // TOK),
        in_specs=[
            pl.BlockSpec((None, TOK, d), lambda b, s: (b, s, 0)),
            _resident((bsz, 3 * d)),
            _resident((1, d)),
            pl.BlockSpec(memory_space=pl.ANY),
            _resident((M_CONV, CONV_DIM)), _resident((1, CONV_DIM)),
            _resident((3, M_INNER)), _resident((1, M_INNER)),
            pl.BlockSpec(memory_space=pl.ANY), pl.BlockSpec(memory_space=pl.ANY),
            pl.BlockSpec(memory_space=pl.ANY),
            _resident((1, d)),
            tok_block(LANES), tok_block(LANES),
            _resident((R_HEADS // 2, CHUNK, 2 * CHUNK)),
            _resident((CHUNK, R_QK)), _resident((CHUNK, R_QK)),
            _resident((R_HEADS, 1, R_HEAD_V)),
            _resident((MXU_W, MXU_W)),
        ],
        out_specs=pl.BlockSpec((None, TOK, d), lambda b, s: (b, s, 0)),
        out_shape=jax.ShapeDtypeStruct((bsz, seq_len, d), x.dtype),
        scratch_shapes=[
            pltpu.VMEM((TOK + 2 * SUBLANES, CONV_DIM), f32),
            pltpu.VMEM((M_GROUPS, M_STATE, GROUP_W), f32),
            pltpu.VMEM((R_HEADS, R_HEAD_QK, R_HEAD_V), f32),
            pltpu.VMEM((TOK, M_INNER), f32),
            pltpu.VMEM((TOK, R_V), f32),
            pltpu.VMEM((TOK, M_INNER), f32),
            pltpu.VMEM((TOK, R_V), f32),
            pltpu.VMEM((TOK, D_MODEL), f32),
            pltpu.VMEM((TOK, D_MODEL), f32),
            pltpu.VMEM((TOK, R_QK), f32),
            pltpu.VMEM((TOK, R_QK), f32),
            pltpu.VMEM((TOK, R_V), bf16),
            pltpu.VMEM((d, W_COLS), bf16),
            pltpu.VMEM((M_INNER, d), bf16), pltpu.VMEM((R_V, d), bf16), pltpu.VMEM((d, d), bf16),
            pltpu.VMEM((W_SLABS_IN_FLIGHT, MXU_W, d), f32),
            pltpu.SemaphoreType.DMA((W_SLABS_IN_FLIGHT,)),
        ],
        compiler_params=pltpu.CompilerParams(
            dimension_semantics=("arbitrary", "arbitrary"),
            vmem_limit_bytes=VMEM_LIMIT_BYTES),
        name="hybrid_block",
    )(x, mod, norm_w.reshape(1, d), w_in_t, conv_w, conv_b.reshape(1, CONV_DIM),
      per_head, m_norm_w.reshape(1, M_INNER),
      w_proj_m, w_proj_r, w_out, out_norm_w.reshape(1, d),
      cos_t, sin_t, dmat, qdec, kdec, rdec, tril)


@jax.jit
def kernel(x, c, w_ada, b_ada, norm_w, w_in, conv_w, conv_b, dt_bias, a_log, d_skip, m_norm_w,
           w_proj_m, w_proj_r, w_out, final_norm_w):
    assert w_ada.shape[0] == 1
    return _layer(x, c, w_ada[0], b_ada[0], norm_w[0], jnp.swapaxes(w_in, 1, 2)[0], conv_w[0], conv_b[0], dt_bias[0],
                  a_log[0], d_skip[0], m_norm_w[0], w_proj_m[0], w_proj_r[0], w_out[0], final_norm_w)
```

```python
import functools

import jax
import jax.numpy as jnp
import numpy as np
from jax import lax
from jax.experimental import pallas as pl
from jax.experimental.pallas import tpu as pltpu

f32 = jnp.float32
bf16 = jnp.bfloat16

D_MODEL = 1024
M_HEADDIM = 64
M_HEADS = 16
M_GROUPS = 2
M_STATE = 64
M_CONV = 4
M_INNER = D_MODEL
GROUP_W = M_INNER // M_GROUPS
HEADS_PER_GROUP = M_HEADS // M_GROUPS
CONV_DIM = M_INNER + 2 * M_GROUPS * M_STATE
R_HEADS = 8
R_QK = 512
R_HEAD_QK = 64
R_V = 1024
R_HEAD_V = 128
CHUNK = 128
ROPE_BASE = 10000.0
EPS = 1e-6
LOG2_E = 1.4426950408889634

LANES = 128
SUBLANES = 8
MXU_W = 256
TOK = 512
N_CHUNKS = TOK // CHUNK
EDGE_ROWS = 256
MOD_ROWS = 256
W_SLABS_IN_FLIGHT = 4
VMEM_LIMIT_BYTES = 58 * 1024 * 1024

OFF_Z = 0
OFF_XBC = OFF_Z + M_INNER
OFF_DT = OFF_XBC + CONV_DIM
OFF_Q = OFF_DT + MXU_W
OFF_K = OFF_Q + R_QK
OFF_V = OFF_K + R_QK
OFF_GR = OFF_V + R_V
OFF_GAM = OFF_GR + R_V
OFF_GAR = OFF_GAM + D_MODEL
W_COLS = OFF_GAR + D_MODEL

NT_DIMS = (((1,), (1,)), ((), ()))


def _mod_kernel(c_ref, w_ref, b_ref, o_ref):
    @pl.when(pl.program_id(0) == 0)
    def _start():
        o_ref[...] = jnp.broadcast_to(b_ref[...], o_ref.shape)

    c_hi, c_lo = _split2(c_ref[...])
    w_hi, w_lo = _split2(w_ref[...])
    n = c_hi.shape[0]
    with_w_hi = _dot(jnp.concatenate([c_hi, c_lo], axis=0), w_hi)
    o_ref[...] += with_w_hi[:n] + with_w_hi[n:] + _dot(c_hi, w_lo)


def _sigmoid(v):
    return 1.0 / (1.0 + jnp.exp(-v))


def _silu(v):
    return v * _sigmoid(v)


def _softplus(v):
    return jnp.maximum(v, 0.0) + jnp.log1p(jnp.exp(-jnp.abs(v)))


def _dot(a, b):
    return jnp.dot(a, b, preferred_element_type=f32)


def _split2(v):
    hi = v.astype(bf16)
    return hi, (v - hi.astype(f32)).astype(bf16)


def _split3(v):
    hi = v.astype(bf16)
    r1 = v - hi.astype(f32)
    mid = r1.astype(bf16)
    lo = (r1 - mid.astype(f32)).astype(bf16)
    return hi, mid, lo


def _load_weights(wt_hbm, square_hbm, win_ref, square_refs, stage, sems):
    dt_tile = OFF_DT // MXU_W
    jobs = []

    def w_in_tile(w_t, j):
        w = w_t.T
        if j == dt_tile:
            w = jnp.where(lax.broadcasted_iota(jnp.int32, w.shape, 1) < M_HEADS, w, 0.0)
        win_ref[:, j * MXU_W:(j + 1) * MXU_W] = w.astype(bf16)

    def square_rows(w, dst, r0):
        dst[r0:r0 + MXU_W, :] = w.astype(bf16)

    for j in range(W_COLS // MXU_W):
        row0 = j * MXU_W if j <= dt_tile else j * MXU_W - (MXU_W - M_HEADS)
        jobs.append((wt_hbm.at[pl.ds(row0, MXU_W), :], functools.partial(w_in_tile, j=j)))
    for src, dst in zip(square_hbm, square_refs):
        for r0 in range(0, D_MODEL, MXU_W):
            jobs.append((src.at[pl.ds(r0, MXU_W), :], functools.partial(square_rows, dst=dst, r0=r0)))

    def slab_copy(i):
        slot = i % W_SLABS_IN_FLIGHT
        return pltpu.make_async_copy(jobs[i][0], stage.at[slot], sems.at[slot])

    for i in range(W_SLABS_IN_FLIGHT - 1):
        slab_copy(i).start()
    for i in range(len(jobs)):
        if i + W_SLABS_IN_FLIGHT - 1 < len(jobs):
            slab_copy(i + W_SLABS_IN_FLIGHT - 1).start()
        slab_copy(i).wait()
        jobs[i][1](stage[i % W_SLABS_IN_FLIGHT])


def _block_kernel(x_ref, mod_ref, normw_ref, wt_hbm, convw_ref, convb_ref,
                  heads_ref, mnw_ref, wpm_hbm, wpr_hbm, wout_hbm, fnw_ref,
                  cos_ref, sin_ref, dmat_ref, qdec_ref, kdec_ref, rdec_ref, tril_ref,
                  o_ref,
                  xbc_buf, s_state, r_state, y_buf, o_buf, z_buf, gr_buf, gam_buf, gar_buf,
                  q_buf, k_buf, v_buf, win_ref, wpm_ref, wpr_ref, wout_ref, w_stage, w_sems):
    @pl.when(jnp.logical_and(pl.program_id(0) == 0, pl.program_id(1) == 0))
    def _first_step():
        _load_weights(wt_hbm, (wpm_hbm, wpr_hbm, wout_hbm), win_ref, (wpm_ref, wpr_ref, wout_ref), w_stage, w_sems)

    @pl.when(pl.program_id(1) == 0)
    def _start_of_sequence():
        xbc_buf[0:SUBLANES, :] = jnp.zeros((SUBLANES, CONV_DIM), f32)
        s_state[...] = jnp.zeros_like(s_state)
        r_state[...] = jnp.zeros_like(r_state)

    mod = mod_ref[pl.ds(pl.program_id(0), 1), :]
    shift, scale, gate = (mod[:, i * D_MODEL:(i + 1) * D_MODEL] for i in range(3))
    g1 = normw_ref[...] * (1.0 + scale)
    slabs = [slice(r0, r0 + EDGE_ROWS) for r0 in range(0, TOK, EDGE_ROWS)]
    hb_slabs = []
    for rows in slabs:
        x = x_ref[rows, :]
        rs = lax.rsqrt(jnp.mean(x * x, axis=-1, keepdims=True) + EPS)
        hb_slabs.append((x * rs * g1 + shift).astype(bf16))
    hb = jnp.concatenate(hb_slabs, axis=0)

    def proj(off, n):
        return _dot(hb, win_ref[:, off:off + n])

    def proj_tiles(dst, off):
        def tile(c0):
            dst[:, c0:c0 + MXU_W] = proj(off + c0, MXU_W).astype(dst.dtype)
        return [functools.partial(tile, c0) for c0 in range(0, dst.shape[1], MXU_W)]

    def emit(queue, n=1):
        for _ in range(min(n, len(queue))):
            queue.pop(0)()

    qkv_fill = proj_tiles(q_buf, OFF_Q) + proj_tiles(k_buf, OFF_K) + proj_tiles(v_buf, OFF_V)
    gate_fill = (proj_tiles(z_buf, OFF_Z) + proj_tiles(gr_buf, OFF_GR)
                 + proj_tiles(gam_buf, OFF_GAM) + proj_tiles(gar_buf, OFF_GAR))

    xbc_dt = jnp.concatenate([_dot(h, win_ref[:, OFF_XBC:OFF_XBC + CONV_DIM + MXU_W]) for h in hb_slabs],
                             axis=0)
    xbc_buf[SUBLANES:SUBLANES + TOK, :] = xbc_dt[:, :CONV_DIM]
    dt_raw = xbc_dt[:, CONV_DIM:CONV_DIM + LANES]
    emit(gate_fill, 2)

    conv = convb_ref[...] + xbc_buf[SUBLANES:SUBLANES + TOK, :] * convw_ref[M_CONV - 1:M_CONV, :]
    for kk in range(M_CONV - 1):
        r0 = SUBLANES - (M_CONV - 1) + kk
        conv = conv + xbc_buf[r0:r0 + TOK, :] * convw_ref[kk:kk + 1, :]
    xbc_buf[0:SUBLANES, :] = xbc_buf[TOK:TOK + SUBLANES, :]
    xa = _silu(conv)
    xm = xa[:, :M_INNER]
    bmat = xa[:, M_INNER:M_INNER + LANES]
    cmat = xa[:, M_INNER + LANES:CONV_DIM]
    xm_b = xm.astype(bf16)

    lane = lax.broadcasted_iota(jnp.int32, (1, LANES), 1)
    a_neg = jnp.where(lane < M_HEADS, -jnp.exp(heads_ref[1:2, :LANES]), 0.0)
    dt = _softplus(dt_raw + heads_ref[0:1, :LANES])
    a = dt * (a_neg * LOG2_E)
    a_split = jnp.concatenate(_split3(a), axis=1)
    a_parts = jnp.concatenate([_dot(tril_ref[...], a_split[r0:r0 + MXU_W, :]) for r0 in range(0, TOK, MXU_W)],
                              axis=0)
    a_cs = a_parts[:, :LANES] + a_parts[:, LANES:2 * LANES] + a_parts[:, 2 * LANES:]
    emit(gate_fill, 2)

    ii = lax.broadcasted_iota(jnp.int32, (CHUNK, CHUNK), 0)
    jj = lax.broadcasted_iota(jnp.int32, (CHUNK, CHUNK), 1)
    causal = ii >= jj
    left_head = jj < M_HEADDIM

    def blockdiag(slab):
        zero = jnp.zeros_like(slab)
        return jnp.concatenate([jnp.where(left_head, slab, zero), jnp.where(left_head, zero, slab)], axis=0)

    def ssd_chunk(c):
        r0 = c * CHUNK
        acs_c = a_cs[r0:r0 + CHUNK, :]
        acs_t = acs_c.T[:M_HEADS, :]
        dt_t = dt[r0:r0 + CHUNK, :].T[:M_HEADS, :]
        w_t = jnp.exp2(acs_t[:, CHUNK - 1:CHUNK] - acs_t) * dt_t
        src_t = acs_t - jnp.log2(dt_t)
        b_t = bmat[r0:r0 + CHUNK, :].T
        c_c = cmat[r0:r0 + CHUNK, :].astype(bf16)
        cb2 = lax.dot_general(c_c, blockdiag(bmat[r0:r0 + CHUNK, :].astype(bf16)), NT_DIMS,
                              preferred_element_type=f32)
        s_prev = [s_state[g] for g in range(M_GROUPS)]
        y_off = [_dot(c_c[:, g * M_STATE:(g + 1) * M_STATE], s_prev[g].astype(bf16))
                 for g in range(M_GROUPS)]
        return acs_c, src_t, w_t, b_t, [cb2[:, :CHUNK], cb2[:, CHUNK:]], s_prev, y_off

    def ssd_pair(c, pair, chunk):
        acs_c, src_t, w_t, b_t, cb, s_prev, y_off = chunk
        r0 = c * CHUNK
        g = pair // (HEADS_PER_GROUP // 2)
        gslab = slice((pair % (HEADS_PER_GROUP // 2)) * LANES, (pair % (HEADS_PER_GROUP // 2) + 1) * LANES)
        pcols = slice(pair * LANES, (pair + 1) * LANES)
        heads = (2 * pair, 2 * pair + 1)
        cols = [jnp.broadcast_to(acs_c[:, h:h + 1], (CHUNK, CHUNK)) for h in heads]
        w_rows = [(cb[g] * jnp.exp2(jnp.where(causal, col - src_t[h:h + 1, :], -jnp.inf))).astype(bf16)
                  for h, col in zip(heads, cols)]
        bt_g = b_t[g * M_STATE:(g + 1) * M_STATE, :]
        st_rows = [(bt_g * w_t[h:h + 1, :]).astype(bf16) for h in heads]
        lhs = jnp.concatenate([jnp.concatenate(w_rows, axis=1), jnp.concatenate(st_rows, axis=1)], axis=0)
        res = _dot(lhs, blockdiag(xm_b[r0:r0 + CHUNK, pcols]))
        decay = jnp.exp2(jnp.where(left_head, cols[0], cols[1]))
        y_buf[r0:r0 + CHUNK, pcols] = res[:CHUNK] + y_off[g][:, gslab] * decay
        s_state[g, :, gslab] = s_prev[g][:, gslab] * decay[CHUNK - 1:CHUNK, :] + res[CHUNK:]

    ret = {}

    def retention_prep():
        cos = jnp.concatenate([cos_ref[...]] * (R_QK // LANES), axis=1)
        sin = jnp.concatenate([sin_ref[...]] * (R_QK // LANES), axis=1)
        lane_q = lax.broadcasted_iota(jnp.int32, (TOK, R_QK), 1)
        first_half = (lane_q % R_HEAD_QK) < (R_HEAD_QK // 2)

        def rotary(t):
            swapped = jnp.where(first_half, pltpu.roll(t, R_QK - R_HEAD_QK // 2, 1),
                                pltpu.roll(t, R_HEAD_QK // 2, 1))
            return t * cos + swapped * sin

        q = rotary(q_buf[...])
        k = rotary(k_buf[...]) * (R_HEAD_QK ** -0.5)
        qdec = jnp.concatenate([qdec_ref[...]] * N_CHUNKS, axis=0)
        kdec = jnp.concatenate([kdec_ref[...]] * N_CHUNKS, axis=0)
        ret.update(q_b=q.astype(bf16), k_b=k.astype(bf16), qw_b=(q * qdec).astype(bf16), kw=k * kdec)

    s_b = {}

    def score_pair(c, pair):
        r0 = c * CHUNK
        pcols = slice(pair * LANES, (pair + 1) * LANES)
        s_pair = (lax.dot_general(ret["q_b"][r0:r0 + CHUNK, pcols], blockdiag(ret["k_b"][r0:r0 + CHUNK, pcols]),
                                  NT_DIMS, preferred_element_type=f32) * dmat_ref[pair]).astype(bf16)
        s_b[c, 2 * pair], s_b[c, 2 * pair + 1] = s_pair[:, :CHUNK], s_pair[:, CHUNK:]

    def retention_head(c, h, kw_t):
        r0 = c * CHUNK
        qk_cols = slice(h * R_HEAD_QK, (h + 1) * R_HEAD_QK)
        v_cols = slice(h * R_HEAD_V, (h + 1) * R_HEAD_V)
        r_prev = r_state[h]
        lhs = jnp.concatenate(
            [jnp.concatenate([s_b[c, h], ret["qw_b"][r0:r0 + CHUNK, qk_cols]], axis=1),
             jnp.concatenate([kw_t[qk_cols, :], jnp.zeros((R_HEAD_QK, R_HEAD_QK), bf16)], axis=1)], axis=0)
        res = _dot(lhs, jnp.concatenate([v_buf[r0:r0 + CHUNK, v_cols], r_prev.astype(bf16)], axis=0))
        o_h = res[:CHUNK]
        r_state[h] = r_prev * rdec_ref[h] + res[CHUNK:]
        o_buf[r0:r0 + CHUNK, v_cols] = o_h * lax.rsqrt(jnp.mean(o_h * o_h, axis=-1, keepdims=True) + EPS)

    def ssd_output():
        y = y_buf[...] + heads_ref[2:3, :] * xm
        yz = y * _silu(z_buf[...])
        y_m = []
        for g in range(M_GROUPS):
            yz_g = yz[:, g * GROUP_W:(g + 1) * GROUP_W]
            y_m.append(yz_g * lax.rsqrt(jnp.mean(yz_g * yz_g, axis=-1, keepdims=True) + EPS))
        return (jnp.concatenate(y_m, axis=1) * mnw_ref[...]).astype(bf16)

    half = R_HEADS // 2
    n_mid, n_gate = (N_CHUNKS - 1) * R_HEADS, len(gate_fill) - 2
    u_m_tiles = []
    for r in range(N_CHUNKS + 1):
        if r < N_CHUNKS:
            chunk = ssd_chunk(r)
        if r >= 1:
            kw_t = ret["kw"][(r - 1) * CHUNK:r * CHUNK, :].T.astype(bf16)
        if r == N_CHUNKS:
            y_m = ssd_output()
        for i in range(R_HEADS):
            if r < N_CHUNKS:
                ssd_pair(r, i, chunk)
                if r == 0 and i == half:
                    retention_prep()
                if i >= half:
                    score_pair(r, i - half)
            if r >= 1:
                retention_head(r - 1, i, kw_t)
            if r == 0 and i < R_HEADS - 2:
                emit(qkv_fill, 2 if i < 2 else 1)
            elif r == 0:
                emit(gate_fill)
            elif r < N_CHUNKS:
                it = (r - 1) * R_HEADS + i
                emit(gate_fill, (it + 1) * n_gate // n_mid - it * n_gate // n_mid)
            elif i < D_MODEL // MXU_W:
                u_m_tiles.append(_dot(y_m, wpm_ref[:, i * MXU_W:(i + 1) * MXU_W]))
    assert not qkv_fill and not gate_fill and M_HEADS // 2 == R_HEADS
    u_m = jnp.concatenate(u_m_tiles, axis=1)

    u_r = [_dot((_silu(gr_buf[rows, :]) * o_buf[rows, :]).astype(bf16), wpr_ref[...]) for rows in slabs]
    for rows, u_r_slab in zip(slabs, u_r):
        merged = _sigmoid(gam_buf[rows, :]) * u_m[rows, :] + _sigmoid(gar_buf[rows, :]) * u_r_slab
        xo = x_ref[rows, :] + gate * _dot(merged.astype(bf16), wout_ref[...])
        o_ref[rows, :] = xo * lax.rsqrt(jnp.mean(xo * xo, axis=-1, keepdims=True) + EPS) * fnw_ref[...]


def _resident(shape):
    zeros = (0,) * len(shape)
    return pl.BlockSpec(shape, lambda b, s: zeros, pipeline_mode=pl.Buffered(1))


def _tables(seq_len):
    half = R_HEAD_QK // 2
    pos = np.arange(seq_len, dtype=np.float64)
    inv = ROPE_BASE ** (-np.arange(half, dtype=np.float64) / half)
    ang = pos[:, None] * inv[None, :]
    cos_t = np.tile(np.cos(ang), (1, LANES // half))
    sin_t = np.tile(np.concatenate([-np.sin(ang), np.sin(ang)], axis=1), (1, LANES // R_HEAD_QK))
    log_g = np.log1p(-np.exp2(-5.0 - np.arange(R_HEADS, dtype=np.float64)))
    idx = np.arange(CHUNK, dtype=np.float64)
    rel = idx[:, None] - idx[None, :]
    dmat = np.where(rel[None] >= 0, np.exp(np.minimum(rel[None], CHUNK) * log_g[:, None, None]), 0.0)
    dmat = np.concatenate([dmat[0::2], dmat[1::2]], axis=2)
    qdec = np.repeat(np.exp((idx + 1)[:, None] * log_g[None, :]), R_HEAD_QK, axis=1)
    kdec = np.repeat(np.exp((CHUNK - 1 - idx)[:, None] * log_g[None, :]), R_HEAD_QK, axis=1)
    rdec = np.broadcast_to(np.exp(CHUNK * log_g)[:, None, None], (R_HEADS, 1, R_HEAD_V))
    t = np.arange(MXU_W)
    tril = (t[:, None] >= t[None, :]) & (t[:, None] // CHUNK == t[None, :] // CHUNK)
    as_f32 = lambda a: jnp.asarray(np.ascontiguousarray(a, dtype=np.float32))
    as_bf16 = lambda a: jnp.asarray(np.ascontiguousarray(a, dtype=np.float32), dtype=bf16)
    return (as_f32(cos_t), as_f32(sin_t), as_f32(dmat), as_f32(qdec), as_f32(kdec), as_f32(rdec),
            as_bf16(tril))


def _layer(x, c, w_ada, b_ada, norm_w, w_in_t, conv_w, conv_b, dt_bias, a_log, d_skip,
           m_norm_w, w_proj_m, w_proj_r, w_out, out_norm_w):
    bsz, seq_len, d = x.shape
    assert d == D_MODEL and seq_len % TOK == 0

    mod = pl.pallas_call(
        _mod_kernel,
        grid=(d // MOD_ROWS,),
        in_specs=[pl.BlockSpec((bsz, MOD_ROWS), lambda k: (0, k)),
                  pl.BlockSpec((MOD_ROWS, 3 * d), lambda k: (k, 0)),
                  pl.BlockSpec((1, 3 * d), lambda k: (0, 0))],
        out_specs=pl.BlockSpec((bsz, 3 * d), lambda k: (0, 0)),
        out_shape=jax.ShapeDtypeStruct((bsz, 3 * d), f32),
        name="adaln_mod",
    )(c, w_ada, b_ada.reshape(1, 3 * d))

    assert w_in_t.shape == (W_COLS - MXU_W + M_HEADS, d)
    pad_heads = lambda p: jnp.pad(p, (0, M_INNER - M_HEADS))
    per_head = jnp.stack([pad_heads(dt_bias), pad_heads(a_log), jnp.repeat(d_skip, M_HEADDIM)])
    cos_t, sin_t, dmat, qdec, kdec, rdec, tril = _tables(seq_len)

    tok_block = lambda w: pl.BlockSpec((TOK, w), lambda b, s: (s, 0))
    return pl.pallas_call(
        _block_kernel,
        grid=(bsz, seq_len // TOK),
        in_specs=[
            pl.BlockSpec((None, TOK, d), lambda b, s: (b, s, 0)),
            _resident((bsz, 3 * d)),
            _resident((1, d)),
            pl.BlockSpec(memory_space=pl.ANY),
            _resident((M_CONV, CONV_DIM)), _resident((1, CONV_DIM)),
            _resident((3, M_INNER)), _resident((1, M_INNER)),
            pl.BlockSpec(memory_space=pl.ANY), pl.BlockSpec(memory_space=pl.ANY),
            pl.BlockSpec(memory_space=pl.ANY),
            _resident((1, d)),
            tok_block(LANES), tok_block(LANES),
            _resident((R_HEADS // 2, CHUNK, 2 * CHUNK)),
            _resident((CHUNK, R_QK)), _resident((CHUNK, R_QK)),
            _resident((R_HEADS, 1, R_HEAD_V)),
            _resident((MXU_W, MXU_W)),
        ],
        out_specs=pl.BlockSpec((None, TOK, d), lambda b, s: (b, s, 0)),
        out_shape=jax.ShapeDtypeStruct((bsz, seq_len, d), x.dtype),
        scratch_shapes=[
            pltpu.VMEM((TOK + 2 * SUBLANES, CONV_DIM), f32),
            pltpu.VMEM((M_GROUPS, M_STATE, GROUP_W), f32),
            pltpu.VMEM((R_HEADS, R_HEAD_QK, R_HEAD_V), f32),
            pltpu.VMEM((TOK, M_INNER), f32),
            pltpu.VMEM((TOK, R_V), f32),
            pltpu.VMEM((TOK, M_INNER), f32),
            pltpu.VMEM((TOK, R_V), f32),
            pltpu.VMEM((TOK, D_MODEL), f32),
            pltpu.VMEM((TOK, D_MODEL), f32),
            pltpu.VMEM((TOK, R_QK), f32),
            pltpu.VMEM((TOK, R_QK), f32),
            pltpu.VMEM((TOK, R_V), bf16),
            pltpu.VMEM((d, W_COLS), bf16),
            pltpu.VMEM((M_INNER, d), bf16), pltpu.VMEM((R_V, d), bf16), pltpu.VMEM((d, d), bf16),
            pltpu.VMEM((W_SLABS_IN_FLIGHT, MXU_W, d), f32),
            pltpu.SemaphoreType.DMA((W_SLABS_IN_FLIGHT,)),
        ],
        compiler_params=pltpu.CompilerParams(
            dimension_semantics=("arbitrary", "arbitrary"),
            vmem_limit_bytes=VMEM_LIMIT_BYTES),
        name="hybrid_block",
    )(x, mod, norm_w.reshape(1, d), w_in_t, conv_w, conv_b.reshape(1, CONV_DIM),
      per_head, m_norm_w.reshape(1, M_INNER),
      w_proj_m, w_proj_r, w_out, out_norm_w.reshape(1, d),
      cos_t, sin_t, dmat, qdec, kdec, rdec, tril)


@jax.jit
def kernel(x, c, w_ada, b_ada, norm_w, w_in, conv_w, conv_b, dt_bias, a_log, d_skip, m_norm_w,
           w_proj_m, w_proj_r, w_out, final_norm_w):
    assert w_ada.shape[0] == 1
    return _layer(x, c, w_ada[0], b_ada[0], norm_w[0], jnp.swapaxes(w_in, 1, 2)[0], conv_w[0], conv_b[0], dt_bias[0],
                  a_log[0], d_skip[0], m_norm_w[0], w_proj_m[0], w_proj_r[0], w_out[0], final_norm_w)
```

```python
import functools

import jax
import jax.numpy as jnp
import numpy as np
from jax import lax
from jax.experimental import pallas as pl
from jax.experimental.pallas import tpu as pltpu

f32 = jnp.float32
bf16 = jnp.bfloat16

D_MODEL = 1024
M_HEADDIM = 64
M_HEADS = 16
M_GROUPS = 2
M_STATE = 64
M_CONV = 4
M_INNER = D_MODEL
GROUP_W = M_INNER // M_GROUPS
HEADS_PER_GROUP = M_HEADS // M_GROUPS
CONV_DIM = M_INNER + 2 * M_GROUPS * M_STATE
R_HEADS = 8
R_QK = 512
R_HEAD_QK = 64
R_V = 1024
R_HEAD_V = 128
CHUNK = 128
ROPE_BASE = 10000.0
EPS = 1e-6
LOG2_E = 1.4426950408889634

LANES = 128
SUBLANES = 8
MXU_W = 256
TOK = 512
N_CHUNKS = TOK // CHUNK
EDGE_ROWS = 256
W_SLABS_IN_FLIGHT = 4
VMEM_LIMIT_BYTES = 58 * 1024 * 1024

OFF_Z = 0
OFF_XBC = OFF_Z + M_INNER
OFF_DT = OFF_XBC + CONV_DIM
OFF_Q = OFF_DT + MXU_W
OFF_K = OFF_Q + R_QK
OFF_V = OFF_K + R_QK
OFF_GR = OFF_V + R_V
OFF_GAM = OFF_GR + R_V
OFF_GAR = OFF_GAM + D_MODEL
W_COLS = OFF_GAR + D_MODEL

NT_DIMS = (((1,), (1,)), ((), ()))


def _sigmoid(v):
    return 1.0 / (1.0 + jnp.exp(-v))


def _silu(v):
    return v * _sigmoid(v)


def _softplus(v):
    return jnp.maximum(v, 0.0) + jnp.log1p(jnp.exp(-jnp.abs(v)))


def _dot(a, b):
    return jnp.dot(a, b, preferred_element_type=f32)


def _split2(v):
    hi = v.astype(bf16)
    return hi, (v - hi.astype(f32)).astype(bf16)


def _split3(v):
    hi = v.astype(bf16)
    r1 = v - hi.astype(f32)
    mid = r1.astype(bf16)
    lo = (r1 - mid.astype(f32)).astype(bf16)
    return hi, mid, lo


def _load_weights(wt_hbm, square_hbm, wada_hbm, c_ref, bada_ref, win_ref, square_refs, mod_ref, stage, sems):
    dt_tile = OFF_DT // MXU_W
    jobs = []
    mod_ref[...] = jnp.broadcast_to(bada_ref[...], mod_ref.shape)

    def adaln_slab(w, r0, c0):
        c_hi, c_lo = _split2(c_ref[:, r0:r0 + MXU_W])
        w_hi, w_lo = _split2(w)
        n = c_hi.shape[0]
        with_w_hi = _dot(jnp.concatenate([c_hi, c_lo], axis=0), w_hi)
        mod_ref[:, c0:c0 + D_MODEL] += with_w_hi[:n] + with_w_hi[n:] + _dot(c_hi, w_lo)

    def w_in_tile(w_t, j):
        w = w_t.T
        if j == dt_tile:
            w = jnp.where(lax.broadcasted_iota(jnp.int32, w.shape, 1) < M_HEADS, w, 0.0)
        win_ref[:, j * MXU_W:(j + 1) * MXU_W] = w.astype(bf16)

    def square_rows(w, dst, r0):
        dst[r0:r0 + MXU_W, :] = w.astype(bf16)

    for j in range(W_COLS // MXU_W):
        row0 = j * MXU_W if j <= dt_tile else j * MXU_W - (MXU_W - M_HEADS)
        jobs.append((wt_hbm.at[pl.ds(row0, MXU_W), :], functools.partial(w_in_tile, j=j)))
    for src, dst in zip(square_hbm, square_refs):
        for r0 in range(0, D_MODEL, MXU_W):
            jobs.append((src.at[pl.ds(r0, MXU_W), :], functools.partial(square_rows, dst=dst, r0=r0)))
    for c0 in range(0, 3 * D_MODEL, D_MODEL):
        for r0 in range(0, D_MODEL, MXU_W):
            jobs.append((wada_hbm.at[pl.ds(r0, MXU_W), pl.ds(c0, D_MODEL)], functools.partial(adaln_slab, r0=r0, c0=c0)))

    def slab_copy(i):
        slot = i % W_SLABS_IN_FLIGHT
        return pltpu.make_async_copy(jobs[i][0], stage.at[slot], sems.at[slot])

    for i in range(W_SLABS_IN_FLIGHT - 1):
        slab_copy(i).start()
    for i in range(len(jobs)):
        if i + W_SLABS_IN_FLIGHT - 1 < len(jobs):
            slab_copy(i + W_SLABS_IN_FLIGHT - 1).start()
        slab_copy(i).wait()
        jobs[i][1](stage[i % W_SLABS_IN_FLIGHT])


def _block_kernel(x_ref, c_ref, wada_hbm, bada_ref, normw_ref, wt_hbm, convw_ref, convb_ref,
                  heads_ref, mnw_ref, wpm_hbm, wpr_hbm, wout_hbm, fnw_ref,
                  cos_ref, sin_ref, dmat_ref, qdec_ref, kdec_ref, rdec_ref, tril_ref,
                  o_ref,
                  xbc_buf, s_state, r_state, y_buf, o_buf, z_buf, gr_buf, gam_buf, gar_buf,
                  q_buf, k_buf, v_buf, mod_ref, win_ref, wpm_ref, wpr_ref, wout_ref, w_stage, w_sems):
    @pl.when(jnp.logical_and(pl.program_id(0) == 0, pl.program_id(1) == 0))
    def _first_step():
        _load_weights(wt_hbm, (wpm_hbm, wpr_hbm, wout_hbm), wada_hbm, c_ref, bada_ref,
                      win_ref, (wpm_ref, wpr_ref, wout_ref), mod_ref, w_stage, w_sems)

    @pl.when(pl.program_id(1) == 0)
    def _start_of_sequence():
        xbc_buf[0:SUBLANES, :] = jnp.zeros((SUBLANES, CONV_DIM), f32)
        s_state[...] = jnp.zeros_like(s_state)
        r_state[...] = jnp.zeros_like(r_state)

    mod = mod_ref[pl.ds(pl.program_id(0), 1), :]
    shift, scale, gate = (mod[:, i * D_MODEL:(i + 1) * D_MODEL] for i in range(3))
    g1 = normw_ref[...] * (1.0 + scale)
    slabs = [slice(r0, r0 + EDGE_ROWS) for r0 in range(0, TOK, EDGE_ROWS)]
    hb_slabs = []
    for rows in slabs:
        x = x_ref[rows, :]
        rs = lax.rsqrt(jnp.mean(x * x, axis=-1, keepdims=True) + EPS)
        hb_slabs.append((x * rs * g1 + shift).astype(bf16))
    hb = jnp.concatenate(hb_slabs, axis=0)

    def proj(off, n):
        return _dot(hb, win_ref[:, off:off + n])

    def proj_tiles(dst, off):
        def tile(c0):
            dst[:, c0:c0 + MXU_W] = proj(off + c0, MXU_W).astype(dst.dtype)
        return [functools.partial(tile, c0) for c0 in range(0, dst.shape[1], MXU_W)]

    def emit(queue, n=1):
        for _ in range(min(n, len(queue))):
            queue.pop(0)()

    qkv_fill = proj_tiles(q_buf, OFF_Q) + proj_tiles(k_buf, OFF_K) + proj_tiles(v_buf, OFF_V)
    gate_fill = (proj_tiles(z_buf, OFF_Z) + proj_tiles(gr_buf, OFF_GR)
                 + proj_tiles(gam_buf, OFF_GAM) + proj_tiles(gar_buf, OFF_GAR))

    xbc_dt = jnp.concatenate([_dot(h, win_ref[:, OFF_XBC:OFF_XBC + CONV_DIM + MXU_W]) for h in hb_slabs],
                             axis=0)
    xbc_buf[SUBLANES:SUBLANES + TOK, :] = xbc_dt[:, :CONV_DIM]
    dt_raw = xbc_dt[:, CONV_DIM:CONV_DIM + LANES]
    emit(gate_fill, 2)

    conv = convb_ref[...] + xbc_buf[SUBLANES:SUBLANES + TOK, :] * convw_ref[M_CONV - 1:M_CONV, :]
    for kk in range(M_CONV - 1):
        r0 = SUBLANES - (M_CONV - 1) + kk
        conv = conv + xbc_buf[r0:r0 + TOK, :] * convw_ref[kk:kk + 1, :]
    xbc_buf[0:SUBLANES, :] = xbc_buf[TOK:TOK + SUBLANES, :]
    xa = _silu(conv)
    xm = xa[:, :M_INNER]
    bmat = xa[:, M_INNER:M_INNER + LANES]
    cmat = xa[:, M_INNER + LANES:CONV_DIM]
    xm_b = xm.astype(bf16)

    lane = lax.broadcasted_iota(jnp.int32, (1, LANES), 1)
    a_neg = jnp.where(lane < M_HEADS, -jnp.exp(heads_ref[1:2, :LANES]), 0.0)
    dt = _softplus(dt_raw + heads_ref[0:1, :LANES])
    a = dt * (a_neg * LOG2_E)
    a_split = jnp.concatenate(_split3(a), axis=1)
    a_parts = jnp.concatenate([_dot(tril_ref[...], a_split[r0:r0 + MXU_W, :]) for r0 in range(0, TOK, MXU_W)],
                              axis=0)
    a_cs = a_parts[:, :LANES] + a_parts[:, LANES:2 * LANES] + a_parts[:, 2 * LANES:]
    emit(gate_fill, 2)

    ii = lax.broadcasted_iota(jnp.int32, (CHUNK, CHUNK), 0)
    jj = lax.broadcasted_iota(jnp.int32, (CHUNK, CHUNK), 1)
    causal = ii >= jj
    left_head = jj < M_HEADDIM

    def blockdiag(slab):
        zero = jnp.zeros_like(slab)
        return jnp.concatenate([jnp.where(left_head, slab, zero), jnp.where(left_head, zero, slab)], axis=0)

    def ssd_chunk(c):
        r0 = c * CHUNK
        acs_c = a_cs[r0:r0 + CHUNK, :]
        acs_t = acs_c.T[:M_HEADS, :]
        dt_t = dt[r0:r0 + CHUNK, :].T[:M_HEADS, :]
        w_t = jnp.exp2(acs_t[:, CHUNK - 1:CHUNK] - acs_t) * dt_t
        src_t = acs_t - jnp.log2(dt_t)
        b_t = bmat[r0:r0 + CHUNK, :].T
        c_c = cmat[r0:r0 + CHUNK, :].astype(bf16)
        cb2 = lax.dot_general(c_c, blockdiag(bmat[r0:r0 + CHUNK, :].astype(bf16)), NT_DIMS,
                              preferred_element_type=f32)
        s_prev = [s_state[g] for g in range(M_GROUPS)]
        y_off = [_dot(c_c[:, g * M_STATE:(g + 1) * M_STATE], s_prev[g].astype(bf16))
                 for g in range(M_GROUPS)]
        return acs_c, src_t, w_t, b_t, [cb2[:, :CHUNK], cb2[:, CHUNK:]], s_prev, y_off

    def ssd_pair(c, pair, chunk):
        acs_c, src_t, w_t, b_t, cb, s_prev, y_off = chunk
        r0 = c * CHUNK
        g = pair // (HEADS_PER_GROUP // 2)
        gslab = slice((pair % (HEADS_PER_GROUP // 2)) * LANES, (pair % (HEADS_PER_GROUP // 2) + 1) * LANES)
        pcols = slice(pair * LANES, (pair + 1) * LANES)
        heads = (2 * pair, 2 * pair + 1)
        cols = [jnp.broadcast_to(acs_c[:, h:h + 1], (CHUNK, CHUNK)) for h in heads]
        w_rows = [(cb[g] * jnp.exp2(jnp.where(causal, col - src_t[h:h + 1, :], -jnp.inf))).astype(bf16)
                  for h, col in zip(heads, cols)]
        bt_g = b_t[g * M_STATE:(g + 1) * M_STATE, :]
        st_rows = [(bt_g * w_t[h:h + 1, :]).astype(bf16) for h in heads]
        lhs = jnp.concatenate([jnp.concatenate(w_rows, axis=1), jnp.concatenate(st_rows, axis=1)], axis=0)
        res = _dot(lhs, blockdiag(xm_b[r0:r0 + CHUNK, pcols]))
        decay = jnp.exp2(jnp.where(left_head, cols[0], cols[1]))
        y_buf[r0:r0 + CHUNK, pcols] = res[:CHUNK] + y_off[g][:, gslab] * decay
        s_state[g, :, gslab] = s_prev[g][:, gslab] * decay[CHUNK - 1:CHUNK, :] + res[CHUNK:]

    ret = {}

    def retention_prep():
        cos = jnp.concatenate([cos_ref[...]] * (R_QK // LANES), axis=1)
        sin = jnp.concatenate([sin_ref[...]] * (R_QK // LANES), axis=1)
        lane_q = lax.broadcasted_iota(jnp.int32, (TOK, R_QK), 1)
        first_half = (lane_q % R_HEAD_QK) < (R_HEAD_QK // 2)

        def rotary(t):
            swapped = jnp.where(first_half, pltpu.roll(t, R_QK - R_HEAD_QK // 2, 1),
                                pltpu.roll(t, R_HEAD_QK // 2, 1))
            return t * cos + swapped * sin

        q = rotary(q_buf[...])
        k = rotary(k_buf[...]) * (R_HEAD_QK ** -0.5)
        qdec = jnp.concatenate([qdec_ref[...]] * N_CHUNKS, axis=0)
        kdec = jnp.concatenate([kdec_ref[...]] * N_CHUNKS, axis=0)
        ret.update(q_b=q.astype(bf16), k_b=k.astype(bf16), qw_b=(q * qdec).astype(bf16), kw=k * kdec)

    s_b = {}

    def score_pair(c, pair):
        r0 = c * CHUNK
        pcols = slice(pair * LANES, (pair + 1) * LANES)
        s_pair = (lax.dot_general(ret["q_b"][r0:r0 + CHUNK, pcols], blockdiag(ret["k_b"][r0:r0 + CHUNK, pcols]),
                                  NT_DIMS, preferred_element_type=f32) * dmat_ref[pair]).astype(bf16)
        s_b[c, 2 * pair], s_b[c, 2 * pair + 1] = s_pair[:, :CHUNK], s_pair[:, CHUNK:]

    def retention_head(c, h, kw_t):
        r0 = c * CHUNK
        qk_cols = slice(h * R_HEAD_QK, (h + 1) * R_HEAD_QK)
        v_cols = slice(h * R_HEAD_V, (h + 1) * R_HEAD_V)
        r_prev = r_state[h]
        lhs = jnp.concatenate(
            [jnp.concatenate([s_b[c, h], ret["qw_b"][r0:r0 + CHUNK, qk_cols]], axis=1),
             jnp.concatenate([kw_t[qk_cols, :], jnp.zeros((R_HEAD_QK, R_HEAD_QK), bf16)], axis=1)], axis=0)
        res = _dot(lhs, jnp.concatenate([v_buf[r0:r0 + CHUNK, v_cols], r_prev.astype(bf16)], axis=0))
        o_h = res[:CHUNK]
        r_state[h] = r_prev * rdec_ref[h] + res[CHUNK:]
        o_buf[r0:r0 + CHUNK, v_cols] = o_h * lax.rsqrt(jnp.mean(o_h * o_h, axis=-1, keepdims=True) + EPS)

    def ssd_output():
        y = y_buf[...] + heads_ref[2:3, :] * xm
        yz = y * _silu(z_buf[...])
        y_m = []
        for g in range(M_GROUPS):
            yz_g = yz[:, g * GROUP_W:(g + 1) * GROUP_W]
            y_m.append(yz_g * lax.rsqrt(jnp.mean(yz_g * yz_g, axis=-1, keepdims=True) + EPS))
        return (jnp.concatenate(y_m, axis=1) * mnw_ref[...]).astype(bf16)

    half = R_HEADS // 2
    n_mid, n_gate = (N_CHUNKS - 1) * R_HEADS, len(gate_fill) - 2
    u_m_tiles = []
    for r in range(N_CHUNKS + 1):
        if r < N_CHUNKS:
            chunk = ssd_chunk(r)
        if r >= 1:
            kw_t = ret["kw"][(r - 1) * CHUNK:r * CHUNK, :].T.astype(bf16)
        if r == N_CHUNKS:
            y_m = ssd_output()
        for i in range(R_HEADS):
            if r < N_CHUNKS:
                ssd_pair(r, i, chunk)
                if r == 0 and i == half:
                    retention_prep()
                if i >= half:
                    score_pair(r, i - half)
            if r >= 1:
                retention_head(r - 1, i, kw_t)
            if r == 0 and i < R_HEADS - 2:
                emit(qkv_fill, 2 if i < 2 else 1)
            elif r == 0:
                emit(gate_fill)
            elif r < N_CHUNKS:
                it = (r - 1) * R_HEADS + i
                emit(gate_fill, (it + 1) * n_gate // n_mid - it * n_gate // n_mid)
            elif i < D_MODEL // MXU_W:
                u_m_tiles.append(_dot(y_m, wpm_ref[:, i * MXU_W:(i + 1) * MXU_W]))
    assert not qkv_fill and not gate_fill and M_HEADS // 2 == R_HEADS
    u_m = jnp.concatenate(u_m_tiles, axis=1)

    u_r = [_dot((_silu(gr_buf[rows, :]) * o_buf[rows, :]).astype(bf16), wpr_ref[...]) for rows in slabs]
    for rows, u_r_slab in zip(slabs, u_r):
        merged = _sigmoid(gam_buf[rows, :]) * u_m[rows, :] + _sigmoid(gar_buf[rows, :]) * u_r_slab
        xo = x_ref[rows, :] + gate * _dot(merged.astype(bf16), wout_ref[...])
        o_ref[rows, :] = xo * lax.rsqrt(jnp.mean(xo * xo, axis=-1, keepdims=True) + EPS) * fnw_ref[...]


def _resident(shape):
    zeros = (0,) * len(shape)
    return pl.BlockSpec(shape, lambda b, s: zeros, pipeline_mode=pl.Buffered(1))


def _tables(seq_len):
    half = R_HEAD_QK // 2
    pos = np.arange(seq_len, dtype=np.float64)
    inv = ROPE_BASE ** (-np.arange(half, dtype=np.float64) / half)
    ang = pos[:, None] * inv[None, :]
    cos_t = np.tile(np.cos(ang), (1, LANES // half))
    sin_t = np.tile(np.concatenate([-np.sin(ang), np.sin(ang)], axis=1), (1, LANES // R_HEAD_QK))
    log_g = np.log1p(-np.exp2(-5.0 - np.arange(R_HEADS, dtype=np.float64)))
    idx = np.arange(CHUNK, dtype=np.float64)
    rel = idx[:, None] - idx[None, :]
    dmat = np.where(rel[None] >= 0, np.exp(np.minimum(rel[None], CHUNK) * log_g[:, None, None]), 0.0)
    dmat = np.concatenate([dmat[0::2], dmat[1::2]], axis=2)
    qdec = np.repeat(np.exp((idx + 1)[:, None] * log_g[None, :]), R_HEAD_QK, axis=1)
    kdec = np.repeat(np.exp((CHUNK - 1 - idx)[:, None] * log_g[None, :]), R_HEAD_QK, axis=1)
    rdec = np.broadcast_to(np.exp(CHUNK * log_g)[:, None, None], (R_HEADS, 1, R_HEAD_V))
    t = np.arange(MXU_W)
    tril = (t[:, None] >= t[None, :]) & (t[:, None] // CHUNK == t[None, :] // CHUNK)
    as_f32 = lambda a: jnp.asarray(np.ascontiguousarray(a, dtype=np.float32))
    as_bf16 = lambda a: jnp.asarray(np.ascontiguousarray(a, dtype=np.float32), dtype=bf16)
    return (as_f32(cos_t), as_f32(sin_t), as_f32(dmat), as_f32(qdec), as_f32(kdec), as_f32(rdec),
            as_bf16(tril))


def _layer(x, c, w_ada, b_ada, norm_w, w_in_t, conv_w, conv_b, dt_bias, a_log, d_skip,
           m_norm_w, w_proj_m, w_proj_r, w_out, out_norm_w):
    bsz, seq_len, d = x.shape
    assert d == D_MODEL and seq_len % TOK == 0

    assert w_in_t.shape == (W_COLS - MXU_W + M_HEADS, d)
    pad_heads = lambda p: jnp.pad(p, (0, M_INNER - M_HEADS))
    per_head = jnp.stack([pad_heads(dt_bias), pad_heads(a_log), jnp.repeat(d_skip, M_HEADDIM)])
    cos_t, sin_t, dmat, qdec, kdec, rdec, tril = _tables(seq_len)

    tok_block = lambda w: pl.BlockSpec((TOK, w), lambda b, s: (s, 0))
    return pl.pallas_call(
        _block_kernel,
        grid=(bsz, seq_len // TOK),
        in_specs=[
            pl.BlockSpec((None, TOK, d), lambda b, s: (b, s, 0)),
            _resident((bsz, d)),
            pl.BlockSpec(memory_space=pl.ANY),
            _resident((1, 3 * d)),
            _resident((1, d)),
            pl.BlockSpec(memory_space=pl.ANY),
            _resident((M_CONV, CONV_DIM)), _resident((1, CONV_DIM)),
            _resident((3, M_INNER)), _resident((1, M_INNER)),
            pl.BlockSpec(memory_space=pl.ANY), pl.BlockSpec(memory_space=pl.ANY),
            pl.BlockSpec(memory_space=pl.ANY),
            _resident((1, d)),
            tok_block(LANES), tok_block(LANES),
            _resident((R_HEADS // 2, CHUNK, 2 * CHUNK)),
            _resident((CHUNK, R_QK)), _resident((CHUNK, R_QK)),
            _resident((R_HEADS, 1, R_HEAD_V)),
            _resident((MXU_W, MXU_W)),
        ],
        out_specs=pl.BlockSpec((None, TOK, d), lambda b, s: (b, s, 0)),
        out_shape=jax.ShapeDtypeStruct((bsz, seq_len, d), x.dtype),
        scratch_shapes=[
            pltpu.VMEM((TOK + 2 * SUBLANES, CONV_DIM), f32),
            pltpu.VMEM((M_GROUPS, M_STATE, GROUP_W), f32),
            pltpu.VMEM((R_HEADS, R_HEAD_QK, R_HEAD_V), f32),
            pltpu.VMEM((TOK, M_INNER), f32),
            pltpu.VMEM((TOK, R_V), f32),
            pltpu.VMEM((TOK, M_INNER), f32),
            pltpu.VMEM((TOK, R_V), f32),
            pltpu.VMEM((TOK, D_MODEL), f32),
            pltpu.VMEM((TOK, D_MODEL), f32),
            pltpu.VMEM((TOK, R_QK), f32),
            pltpu.VMEM((TOK, R_QK), f32),
            pltpu.VMEM((TOK, R_V), bf16),
            pltpu.VMEM((bsz, 3 * d), f32),
            pltpu.VMEM((d, W_COLS), bf16),
            pltpu.VMEM((M_INNER, d), bf16), pltpu.VMEM((R_V, d), bf16), pltpu.VMEM((d, d), bf16),
            pltpu.VMEM((W_SLABS_IN_FLIGHT, MXU_W, d), f32),
            pltpu.SemaphoreType.DMA((W_SLABS_IN_FLIGHT,)),
        ],
        compiler_params=pltpu.CompilerParams(
            dimension_semantics=("arbitrary", "arbitrary"),
            vmem_limit_bytes=VMEM_LIMIT_BYTES),
        name="hybrid_block",
    )(x, c, w_ada, b_ada.reshape(1, 3 * d), norm_w.reshape(1, d), w_in_t, conv_w, conv_b.reshape(1, CONV_DIM),
      per_head, m_norm_w.reshape(1, M_INNER),
      w_proj_m, w_proj_r, w_out, out_norm_w.reshape(1, d),
      cos_t, sin_t, dmat, qdec, kdec, rdec, tril)


@jax.jit
def kernel(x, c, w_ada, b_ada, norm_w, w_in, conv_w, conv_b, dt_bias, a_log, d_skip, m_norm_w,
           w_proj_m, w_proj_r, w_out, final_norm_w):
    assert w_ada.shape[0] == 1
    return _layer(x, c, w_ada[0], b_ada[0], norm_w[0], jnp.swapaxes(w_in, 1, 2)[0], conv_w[0], conv_b[0], dt_bias[0],
                  a_log[0], d_skip[0], m_norm_w[0], w_proj_m[0], w_proj_r[0], w_out[0], final_norm_w)
```

```python
import functools

import jax
import jax.numpy as jnp
import numpy as np
from jax import lax
from jax.experimental import pallas as pl
from jax.experimental.pallas import tpu as pltpu

f32 = jnp.float32
bf16 = jnp.bfloat16

D_MODEL = 1024
M_HEADDIM = 64
M_HEADS = 16
M_GROUPS = 2
M_STATE = 64
M_CONV = 4
M_INNER = D_MODEL
GROUP_W = M_INNER // M_GROUPS
HEADS_PER_GROUP = M_HEADS // M_GROUPS
CONV_DIM = M_INNER + 2 * M_GROUPS * M_STATE
R_HEADS = 8
R_QK = 512
R_HEAD_QK = 64
R_V = 1024
R_HEAD_V = 128
CHUNK = 128
ROPE_BASE = 10000.0
EPS = 1e-6
LOG2_E = 1.4426950408889634

LANES = 128
SUBLANES = 8
MXU_W = 256
TOK = 512
N_CHUNKS = TOK // CHUNK
EDGE_ROWS = 256
W_SLABS_IN_FLIGHT = 4
VMEM_LIMIT_BYTES = 58 * 1024 * 1024

OFF_Z = 0
OFF_XBC = OFF_Z + M_INNER
OFF_DT = OFF_XBC + CONV_DIM
OFF_Q = OFF_DT + MXU_W
OFF_K = OFF_Q + R_QK
OFF_V = OFF_K + R_QK
OFF_GR = OFF_V + R_V
OFF_GAM = OFF_GR + R_V
OFF_GAR = OFF_GAM + D_MODEL
W_COLS = OFF_GAR + D_MODEL

NT_DIMS = (((1,), (1,)), ((), ()))


def _sigmoid(v):
    return 1.0 / (1.0 + jnp.exp(-v))


def _silu(v):
    return v * _sigmoid(v)


def _softplus(v):
    return jnp.maximum(v, 0.0) + jnp.log1p(jnp.exp(-jnp.abs(v)))


def _dot(a, b):
    return jnp.dot(a, b, preferred_element_type=f32)


def _split2(v):
    hi = v.astype(bf16)
    return hi, (v - hi.astype(f32)).astype(bf16)


def _split3(v):
    hi = v.astype(bf16)
    r1 = v - hi.astype(f32)
    mid = r1.astype(bf16)
    lo = (r1 - mid.astype(f32)).astype(bf16)
    return hi, mid, lo


def _load_weights(wt_hbm, square_hbm, wada_hbm, c_ref, bada_ref, win_ref, square_refs, mod_ref, stage, sems):
    dt_tile = OFF_DT // MXU_W
    jobs = []
    mod_ref[...] = jnp.broadcast_to(bada_ref[...], mod_ref.shape)

    def adaln_slab(w, r0, c0):
        c_hi, c_lo = _split2(c_ref[:, r0:r0 + MXU_W])
        w_hi, w_lo = _split2(w)
        n = c_hi.shape[0]
        with_w_hi = _dot(jnp.concatenate([c_hi, c_lo], axis=0), w_hi)
        mod_ref[:, c0:c0 + D_MODEL] += with_w_hi[:n] + with_w_hi[n:] + _dot(c_hi, w_lo)

    def w_in_tile(w_t, j):
        w = w_t.T
        if j == dt_tile:
            w = jnp.where(lax.broadcasted_iota(jnp.int32, w.shape, 1) < M_HEADS, w, 0.0)
        win_ref[:, j * MXU_W:(j + 1) * MXU_W] = w.astype(bf16)

    def square_rows(w, dst, r0):
        dst[r0:r0 + MXU_W, :] = w.astype(bf16)

    for j in range(W_COLS // MXU_W):
        row0 = j * MXU_W if j <= dt_tile else j * MXU_W - (MXU_W - M_HEADS)
        jobs.append((wt_hbm.at[pl.ds(row0, MXU_W), :], functools.partial(w_in_tile, j=j)))
    for src, dst in zip(square_hbm, square_refs):
        for r0 in range(0, D_MODEL, MXU_W):
            jobs.append((src.at[pl.ds(r0, MXU_W), :], functools.partial(square_rows, dst=dst, r0=r0)))
    for c0 in range(0, 3 * D_MODEL, D_MODEL):
        for r0 in range(0, D_MODEL, MXU_W):
            jobs.append((wada_hbm.at[pl.ds(r0, MXU_W), pl.ds(c0, D_MODEL)], functools.partial(adaln_slab, r0=r0, c0=c0)))

    def slab_copy(i):
        slot = i % W_SLABS_IN_FLIGHT
        return pltpu.make_async_copy(jobs[i][0], stage.at[slot], sems.at[slot])

    for i in range(W_SLABS_IN_FLIGHT - 1):
        slab_copy(i).start()
    for i in range(len(jobs)):
        if i + W_SLABS_IN_FLIGHT - 1 < len(jobs):
            slab_copy(i + W_SLABS_IN_FLIGHT - 1).start()
        slab_copy(i).wait()
        jobs[i][1](stage[i % W_SLABS_IN_FLIGHT])


def _fill_per_head(dtb_smem, alog_smem, dskip_smem, heads_ref):
    lane = lax.broadcasted_iota(jnp.int32, (1, M_INNER), 1)
    dtb = alog = dskip = jnp.zeros((1, M_INNER), f32)
    for h in range(M_HEADS):
        dtb = jnp.where(lane == h, dtb_smem[h], dtb)
        alog = jnp.where(lane == h, alog_smem[h], alog)
        dskip = jnp.where(lane // M_HEADDIM == h, dskip_smem[h], dskip)
    heads_ref[0:1, :], heads_ref[1:2, :], heads_ref[2:3, :] = dtb, alog, dskip


def _block_kernel(x_ref, c_ref, wada_hbm, bada_ref, normw_ref, wt_hbm, convw_ref, convb_ref,
                  dtb_smem, alog_smem, dskip_smem, mnw_ref, wpm_hbm, wpr_hbm, wout_hbm, fnw_ref,
                  cos_ref, sin_ref, dmat_ref, qdec_ref, kdec_ref, rdec_ref, tril_ref,
                  o_ref,
                  xbc_buf, s_state, r_state, y_buf, o_buf, z_buf, gr_buf, gam_buf, gar_buf,
                  q_buf, k_buf, v_buf, heads_ref, mod_ref, win_ref, wpm_ref, wpr_ref, wout_ref, w_stage, w_sems):
    @pl.when(jnp.logical_and(pl.program_id(0) == 0, pl.program_id(1) == 0))
    def _first_step():
        _load_weights(wt_hbm, (wpm_hbm, wpr_hbm, wout_hbm), wada_hbm, c_ref, bada_ref,
                      win_ref, (wpm_ref, wpr_ref, wout_ref), mod_ref, w_stage, w_sems)
        _fill_per_head(dtb_smem, alog_smem, dskip_smem, heads_ref)

    @pl.when(pl.program_id(1) == 0)
    def _start_of_sequence():
        xbc_buf[0:SUBLANES, :] = jnp.zeros((SUBLANES, CONV_DIM), f32)
        s_state[...] = jnp.zeros_like(s_state)
        r_state[...] = jnp.zeros_like(r_state)

    mod = mod_ref[pl.ds(pl.program_id(0), 1), :]
    shift, scale, gate = (mod[:, i * D_MODEL:(i + 1) * D_MODEL] for i in range(3))
    g1 = normw_ref[...] * (1.0 + scale)
    slabs = [slice(r0, r0 + EDGE_ROWS) for r0 in range(0, TOK, EDGE_ROWS)]
    hb_slabs = []
    for rows in slabs:
        x = x_ref[rows, :]
        rs = lax.rsqrt(jnp.mean(x * x, axis=-1, keepdims=True) + EPS)
        hb_slabs.append((x * rs * g1 + shift).astype(bf16))
    hb = jnp.concatenate(hb_slabs, axis=0)

    def proj(off, n):
        return _dot(hb, win_ref[:, off:off + n])

    def proj_tiles(dst, off):
        def tile(c0):
            dst[:, c0:c0 + MXU_W] = proj(off + c0, MXU_W).astype(dst.dtype)
        return [functools.partial(tile, c0) for c0 in range(0, dst.shape[1], MXU_W)]

    def emit(queue, n=1):
        for _ in range(min(n, len(queue))):
            queue.pop(0)()

    qkv_fill = proj_tiles(q_buf, OFF_Q) + proj_tiles(k_buf, OFF_K) + proj_tiles(v_buf, OFF_V)
    gate_fill = (proj_tiles(z_buf, OFF_Z) + proj_tiles(gr_buf, OFF_GR)
                 + proj_tiles(gam_buf, OFF_GAM) + proj_tiles(gar_buf, OFF_GAR))

    xbc_dt = jnp.concatenate([_dot(h, win_ref[:, OFF_XBC:OFF_XBC + CONV_DIM + MXU_W]) for h in hb_slabs],
                             axis=0)
    xbc_buf[SUBLANES:SUBLANES + TOK, :] = xbc_dt[:, :CONV_DIM]
    dt_raw = xbc_dt[:, CONV_DIM:CONV_DIM + LANES]
    emit(gate_fill, 2)

    conv = convb_ref[...] + xbc_buf[SUBLANES:SUBLANES + TOK, :] * convw_ref[M_CONV - 1:M_CONV, :]
    for kk in range(M_CONV - 1):
        r0 = SUBLANES - (M_CONV - 1) + kk
        conv = conv + xbc_buf[r0:r0 + TOK, :] * convw_ref[kk:kk + 1, :]
    xbc_buf[0:SUBLANES, :] = xbc_buf[TOK:TOK + SUBLANES, :]
    xa = _silu(conv)
    xm = xa[:, :M_INNER]
    bmat = xa[:, M_INNER:M_INNER + LANES]
    cmat = xa[:, M_INNER + LANES:CONV_DIM]
    xm_b = xm.astype(bf16)

    lane = lax.broadcasted_iota(jnp.int32, (1, LANES), 1)
    a_neg = jnp.where(lane < M_HEADS, -jnp.exp(heads_ref[1:2, :LANES]), 0.0)
    dt = _softplus(dt_raw + heads_ref[0:1, :LANES])
    a = dt * (a_neg * LOG2_E)
    a_split = jnp.concatenate(_split3(a), axis=1)
    a_parts = jnp.concatenate([_dot(tril_ref[...], a_split[r0:r0 + MXU_W, :]) for r0 in range(0, TOK, MXU_W)],
                              axis=0)
    a_cs = a_parts[:, :LANES] + a_parts[:, LANES:2 * LANES] + a_parts[:, 2 * LANES:]
    emit(gate_fill, 2)

    ii = lax.broadcasted_iota(jnp.int32, (CHUNK, CHUNK), 0)
    jj = lax.broadcasted_iota(jnp.int32, (CHUNK, CHUNK), 1)
    causal = ii >= jj
    left_head = jj < M_HEADDIM

    def blockdiag(slab):
        zero = jnp.zeros_like(slab)
        return jnp.concatenate([jnp.where(left_head, slab, zero), jnp.where(left_head, zero, slab)], axis=0)

    def ssd_chunk(c):
        r0 = c * CHUNK
        acs_c = a_cs[r0:r0 + CHUNK, :]
        acs_t = acs_c.T[:M_HEADS, :]
        dt_t = dt[r0:r0 + CHUNK, :].T[:M_HEADS, :]
        w_t = jnp.exp2(acs_t[:, CHUNK - 1:CHUNK] - acs_t) * dt_t
        src_t = acs_t - jnp.log2(dt_t)
        b_t = bmat[r0:r0 + CHUNK, :].T
        c_c = cmat[r0:r0 + CHUNK, :].astype(bf16)
        cb2 = lax.dot_general(c_c, blockdiag(bmat[r0:r0 + CHUNK, :].astype(bf16)), NT_DIMS,
                              preferred_element_type=f32)
        s_prev = [s_state[g] for g in range(M_GROUPS)]
        y_off = [_dot(c_c[:, g * M_STATE:(g + 1) * M_STATE], s_prev[g].astype(bf16))
                 for g in range(M_GROUPS)]
        return acs_c, src_t, w_t, b_t, [cb2[:, :CHUNK], cb2[:, CHUNK:]], s_prev, y_off

    def ssd_pair(c, pair, chunk):
        acs_c, src_t, w_t, b_t, cb, s_prev, y_off = chunk
        r0 = c * CHUNK
        g = pair // (HEADS_PER_GROUP // 2)
        gslab = slice((pair % (HEADS_PER_GROUP // 2)) * LANES, (pair % (HEADS_PER_GROUP // 2) + 1) * LANES)
        pcols = slice(pair * LANES, (pair + 1) * LANES)
        heads = (2 * pair, 2 * pair + 1)
        cols = [jnp.broadcast_to(acs_c[:, h:h + 1], (CHUNK, CHUNK)) for h in heads]
        w_rows = [(cb[g] * jnp.exp2(jnp.where(causal, col - src_t[h:h + 1, :], -jnp.inf))).astype(bf16)
                  for h, col in zip(heads, cols)]
        bt_g = b_t[g * M_STATE:(g + 1) * M_STATE, :]
        st_rows = [(bt_g * w_t[h:h + 1, :]).astype(bf16) for h in heads]
        lhs = jnp.concatenate([jnp.concatenate(w_rows, axis=1), jnp.concatenate(st_rows, axis=1)], axis=0)
        res = _dot(lhs, blockdiag(xm_b[r0:r0 + CHUNK, pcols]))
        decay = jnp.exp2(jnp.where(left_head, cols[0], cols[1]))
        y_buf[r0:r0 + CHUNK, pcols] = res[:CHUNK] + y_off[g][:, gslab] * decay
        s_state[g, :, gslab] = s_prev[g][:, gslab] * decay[CHUNK - 1:CHUNK, :] + res[CHUNK:]

    ret = {}

    def retention_prep():
        cos = jnp.concatenate([cos_ref[...]] * (R_QK // LANES), axis=1)
        sin = jnp.concatenate([sin_ref[...]] * (R_QK // LANES), axis=1)
        lane_q = lax.broadcasted_iota(jnp.int32, (TOK, R_QK), 1)
        first_half = (lane_q % R_HEAD_QK) < (R_HEAD_QK // 2)

        def rotary(t):
            swapped = jnp.where(first_half, pltpu.roll(t, R_QK - R_HEAD_QK // 2, 1),
                                pltpu.roll(t, R_HEAD_QK // 2, 1))
            return t * cos + swapped * sin

        q = rotary(q_buf[...])
        k = rotary(k_buf[...]) * (R_HEAD_QK ** -0.5)
        qdec = jnp.concatenate([qdec_ref[...]] * N_CHUNKS, axis=0)
        kdec = jnp.concatenate([kdec_ref[...]] * N_CHUNKS, axis=0)
        ret.update(q_b=q.astype(bf16), k_b=k.astype(bf16), qw_b=(q * qdec).astype(bf16), kw=k * kdec)

    s_b = {}

    def score_pair(c, pair):
        r0 = c * CHUNK
        pcols = slice(pair * LANES, (pair + 1) * LANES)
        s_pair = (lax.dot_general(ret["q_b"][r0:r0 + CHUNK, pcols], blockdiag(ret["k_b"][r0:r0 + CHUNK, pcols]),
                                  NT_DIMS, preferred_element_type=f32) * dmat_ref[pair]).astype(bf16)
        s_b[c, 2 * pair], s_b[c, 2 * pair + 1] = s_pair[:, :CHUNK], s_pair[:, CHUNK:]

    def retention_head(c, h, kw_t):
        r0 = c * CHUNK
        qk_cols = slice(h * R_HEAD_QK, (h + 1) * R_HEAD_QK)
        v_cols = slice(h * R_HEAD_V, (h + 1) * R_HEAD_V)
        r_prev = r_state[h]
        lhs = jnp.concatenate(
            [jnp.concatenate([s_b[c, h], ret["qw_b"][r0:r0 + CHUNK, qk_cols]], axis=1),
             jnp.concatenate([kw_t[qk_cols, :], jnp.zeros((R_HEAD_QK, R_HEAD_QK), bf16)], axis=1)], axis=0)
        res = _dot(lhs, jnp.concatenate([v_buf[r0:r0 + CHUNK, v_cols], r_prev.astype(bf16)], axis=0))
        o_h = res[:CHUNK]
        r_state[h] = r_prev * rdec_ref[h] + res[CHUNK:]
        o_buf[r0:r0 + CHUNK, v_cols] = o_h * lax.rsqrt(jnp.mean(o_h * o_h, axis=-1, keepdims=True) + EPS)

    def ssd_output():
        y = y_buf[...] + heads_ref[2:3, :] * xm
        yz = y * _silu(z_buf[...])
        y_m = []
        for g in range(M_GROUPS):
            yz_g = yz[:, g * GROUP_W:(g + 1) * GROUP_W]
            y_m.append(yz_g * lax.rsqrt(jnp.mean(yz_g * yz_g, axis=-1, keepdims=True) + EPS))
        return (jnp.concatenate(y_m, axis=1) * mnw_ref[...]).astype(bf16)

    half = R_HEADS // 2
    n_mid, n_gate = (N_CHUNKS - 1) * R_HEADS, len(gate_fill) - 2
    u_m_tiles = []
    for r in range(N_CHUNKS + 1):
        if r < N_CHUNKS:
            chunk = ssd_chunk(r)
        if r >= 1:
            kw_t = ret["kw"][(r - 1) * CHUNK:r * CHUNK, :].T.astype(bf16)
        if r == N_CHUNKS:
            y_m = ssd_output()
        for i in range(R_HEADS):
            if r < N_CHUNKS:
                ssd_pair(r, i, chunk)
                if r == 0 and i == half:
                    retention_prep()
                if i >= half:
                    score_pair(r, i - half)
            if r >= 1:
                retention_head(r - 1, i, kw_t)
            if r == 0 and i < R_HEADS - 2:
                emit(qkv_fill, 2 if i < 2 else 1)
            elif r == 0:
                emit(gate_fill)
            elif r < N_CHUNKS:
                it = (r - 1) * R_HEADS + i
                emit(gate_fill, (it + 1) * n_gate // n_mid - it * n_gate // n_mid)
            elif i < D_MODEL // MXU_W:
                u_m_tiles.append(_dot(y_m, wpm_ref[:, i * MXU_W:(i + 1) * MXU_W]))
    assert not qkv_fill and not gate_fill and M_HEADS // 2 == R_HEADS
    u_m = jnp.concatenate(u_m_tiles, axis=1)

    u_r = [_dot((_silu(gr_buf[rows, :]) * o_buf[rows, :]).astype(bf16), wpr_ref[...]) for rows in slabs]
    for rows, u_r_slab in zip(slabs, u_r):
        merged = _sigmoid(gam_buf[rows, :]) * u_m[rows, :] + _sigmoid(gar_buf[rows, :]) * u_r_slab
        xo = x_ref[rows, :] + gate * _dot(merged.astype(bf16), wout_ref[...])
        o_ref[rows, :] = xo * lax.rsqrt(jnp.mean(xo * xo, axis=-1, keepdims=True) + EPS) * fnw_ref[...]


def _resident(shape):
    zeros = (0,) * len(shape)
    return pl.BlockSpec(shape, lambda b, s: zeros, pipeline_mode=pl.Buffered(1))


def _tables(seq_len):
    half = R_HEAD_QK // 2
    pos = np.arange(seq_len, dtype=np.float64)
    inv = ROPE_BASE ** (-np.arange(half, dtype=np.float64) / half)
    ang = pos[:, None] * inv[None, :]
    cos_t = np.tile(np.cos(ang), (1, LANES // half))
    sin_t = np.tile(np.concatenate([-np.sin(ang), np.sin(ang)], axis=1), (1, LANES // R_HEAD_QK))
    log_g = np.log1p(-np.exp2(-5.0 - np.arange(R_HEADS, dtype=np.float64)))
    idx = np.arange(CHUNK, dtype=np.float64)
    rel = idx[:, None] - idx[None, :]
    dmat = np.where(rel[None] >= 0, np.exp(np.minimum(rel[None], CHUNK) * log_g[:, None, None]), 0.0)
    dmat = np.concatenate([dmat[0::2], dmat[1::2]], axis=2)
    qdec = np.repeat(np.exp((idx + 1)[:, None] * log_g[None, :]), R_HEAD_QK, axis=1)
    kdec = np.repeat(np.exp((CHUNK - 1 - idx)[:, None] * log_g[None, :]), R_HEAD_QK, axis=1)
    rdec = np.broadcast_to(np.exp(CHUNK * log_g)[:, None, None], (R_HEADS, 1, R_HEAD_V))
    t = np.arange(MXU_W)
    tril = (t[:, None] >= t[None, :]) & (t[:, None] // CHUNK == t[None, :] // CHUNK)
    as_f32 = lambda a: jnp.asarray(np.ascontiguousarray(a, dtype=np.float32))
    as_bf16 = lambda a: jnp.asarray(np.ascontiguousarray(a, dtype=np.float32), dtype=bf16)
    return (as_f32(cos_t), as_f32(sin_t), as_f32(dmat), as_f32(qdec), as_f32(kdec), as_f32(rdec),
            as_bf16(tril))


def _layer(x, c, w_ada, b_ada, norm_w, w_in_t, conv_w, conv_b, dt_bias, a_log, d_skip,
           m_norm_w, w_proj_m, w_proj_r, w_out, out_norm_w):
    bsz, seq_len, d = x.shape
    assert d == D_MODEL and seq_len % TOK == 0

    assert w_in_t.shape == (W_COLS - MXU_W + M_HEADS, d)
    cos_t, sin_t, dmat, qdec, kdec, rdec, tril = _tables(seq_len)

    tok_block = lambda w: pl.BlockSpec((TOK, w), lambda b, s: (s, 0))
    return pl.pallas_call(
        _block_kernel,
        grid=(bsz, seq_len // TOK),
        in_specs=[
            pl.BlockSpec((None, TOK, d), lambda b, s: (b, s, 0)),
            _resident((bsz, d)),
            pl.BlockSpec(memory_space=pl.ANY),
            _resident((1, 3 * d)),
            _resident((1, d)),
            pl.BlockSpec(memory_space=pl.ANY),
            _resident((M_CONV, CONV_DIM)), _resident((1, CONV_DIM)),
            pl.BlockSpec(memory_space=pltpu.SMEM), pl.BlockSpec(memory_space=pltpu.SMEM),
            pl.BlockSpec(memory_space=pltpu.SMEM),
            _resident((1, M_INNER)),
            pl.BlockSpec(memory_space=pl.ANY), pl.BlockSpec(memory_space=pl.ANY),
            pl.BlockSpec(memory_space=pl.ANY),
            _resident((1, d)),
            tok_block(LANES), tok_block(LANES),
            _resident((R_HEADS // 2, CHUNK, 2 * CHUNK)),
            _resident((CHUNK, R_QK)), _resident((CHUNK, R_QK)),
            _resident((R_HEADS, 1, R_HEAD_V)),
            _resident((MXU_W, MXU_W)),
        ],
        out_specs=pl.BlockSpec((None, TOK, d), lambda b, s: (b, s, 0)),
        out_shape=jax.ShapeDtypeStruct((bsz, seq_len, d), x.dtype),
        scratch_shapes=[
            pltpu.VMEM((TOK + 2 * SUBLANES, CONV_DIM), f32),
            pltpu.VMEM((M_GROUPS, M_STATE, GROUP_W), f32),
            pltpu.VMEM((R_HEADS, R_HEAD_QK, R_HEAD_V), f32),
            pltpu.VMEM((TOK, M_INNER), f32),
            pltpu.VMEM((TOK, R_V), f32),
            pltpu.VMEM((TOK, M_INNER), f32),
            pltpu.VMEM((TOK, R_V), f32),
            pltpu.VMEM((TOK, D_MODEL), f32),
            pltpu.VMEM((TOK, D_MODEL), f32),
            pltpu.VMEM((TOK, R_QK), f32),
            pltpu.VMEM((TOK, R_QK), f32),
            pltpu.VMEM((TOK, R_V), bf16),
            pltpu.VMEM((SUBLANES, M_INNER), f32),
            pltpu.VMEM((bsz, 3 * d), f32),
            pltpu.VMEM((d, W_COLS), bf16),
            pltpu.VMEM((M_INNER, d), bf16), pltpu.VMEM((R_V, d), bf16), pltpu.VMEM((d, d), bf16),
            pltpu.VMEM((W_SLABS_IN_FLIGHT, MXU_W, d), f32),
            pltpu.SemaphoreType.DMA((W_SLABS_IN_FLIGHT,)),
        ],
        compiler_params=pltpu.CompilerParams(
            dimension_semantics=("arbitrary", "arbitrary"),
            vmem_limit_bytes=VMEM_LIMIT_BYTES),
        name="hybrid_block",
    )(x, c, w_ada, b_ada.reshape(1, 3 * d), norm_w.reshape(1, d), w_in_t, conv_w, conv_b.reshape(1, CONV_DIM),
      dt_bias, a_log, d_skip, m_norm_w.reshape(1, M_INNER),
      w_proj_m, w_proj_r, w_out, out_norm_w.reshape(1, d),
      cos_t, sin_t, dmat, qdec, kdec, rdec, tril)


@jax.jit
def kernel(x, c, w_ada, b_ada, norm_w, w_in, conv_w, conv_b, dt_bias, a_log, d_skip, m_norm_w,
           w_proj_m, w_proj_r, w_out, final_norm_w):
    assert w_ada.shape[0] == 1
    return _layer(x, c, w_ada[0], b_ada[0], norm_w[0], jnp.swapaxes(w_in, 1, 2)[0], conv_w[0], conv_b[0], dt_bias[0],
                  a_log[0], d_skip[0], m_norm_w[0], w_proj_m[0], w_proj_r[0], w_out[0], final_norm_w)
```

```python
import functools

import jax
import jax.numpy as jnp
import numpy as np
from jax import lax
from jax.experimental import pallas as pl
from jax.experimental.pallas import tpu as pltpu

f32 = jnp.float32
bf16 = jnp.bfloat16

D_MODEL = 1024
M_HEADDIM = 64
M_HEADS = 16
M_GROUPS = 2
M_STATE = 64
M_CONV = 4
M_INNER = D_MODEL
GROUP_W = M_INNER // M_GROUPS
HEADS_PER_GROUP = M_HEADS // M_GROUPS
CONV_DIM = M_INNER + 2 * M_GROUPS * M_STATE
R_HEADS = 8
R_QK = 512
R_HEAD_QK = 64
R_V = 1024
R_HEAD_V = 128
CHUNK = 128
ROPE_BASE = 10000.0
EPS = 1e-6
LOG2_E = 1.4426950408889634

LANES = 128
SUBLANES = 8
MXU_W = 256
TOK = 512
N_CHUNKS = TOK // CHUNK
EDGE_ROWS = 256
W_SLABS_IN_FLIGHT = 4
VMEM_LIMIT_BYTES = 58 * 1024 * 1024

OFF_Z = 0
OFF_XBC = OFF_Z + M_INNER
OFF_DT = OFF_XBC + CONV_DIM
OFF_Q = OFF_DT + MXU_W
OFF_K = OFF_Q + R_QK
OFF_V = OFF_K + R_QK
OFF_GR = OFF_V + R_V
OFF_GAM = OFF_GR + R_V
OFF_GAR = OFF_GAM + D_MODEL
W_COLS = OFF_GAR + D_MODEL

NT_DIMS = (((1,), (1,)), ((), ()))


def _sigmoid(v):
    return 1.0 / (1.0 + jnp.exp(-v))


def _silu(v):
    return v * _sigmoid(v)


def _softplus(v):
    return jnp.maximum(v, 0.0) + jnp.log1p(jnp.exp(-jnp.abs(v)))


def _dot(a, b):
    return jnp.dot(a, b, preferred_element_type=f32)


def _split2(v):
    hi = v.astype(bf16)
    return hi, (v - hi.astype(f32)).astype(bf16)


def _split3(v):
    hi = v.astype(bf16)
    r1 = v - hi.astype(f32)
    mid = r1.astype(bf16)
    lo = (r1 - mid.astype(f32)).astype(bf16)
    return hi, mid, lo


def _load_weights(wt_hbm, square_hbm, wada_hbm, c_ref, bada_ref, win_ref, wdt_ref, square_refs, mod_ref, stage, sems):
    dt_tile = OFF_DT // MXU_W
    jobs = []
    mod_ref[...] = jnp.broadcast_to(bada_ref[...], mod_ref.shape)

    def adaln_slab(w, r0, c0):
        c_hi, c_lo = _split2(c_ref[:, r0:r0 + MXU_W])
        w_hi, w_lo = _split2(w)
        n = c_hi.shape[0]
        with_w_hi = _dot(jnp.concatenate([c_hi, c_lo], axis=0), w_hi)
        mod_ref[:, c0:c0 + D_MODEL] += with_w_hi[:n] + with_w_hi[n:] + _dot(c_hi, w_lo)

    def w_in_tile(w_t, j):
        w = w_t.T
        if j == dt_tile:
            wdt_ref[...] = w_t[:M_HEADS, :].astype(bf16)
            w = jnp.where(lax.broadcasted_iota(jnp.int32, w.shape, 1) < M_HEADS, w, 0.0)
        win_ref[:, j * MXU_W:(j + 1) * MXU_W] = w.astype(bf16)

    def square_rows(w, dst, r0):
        dst[r0:r0 + MXU_W, :] = w.astype(bf16)

    for j in range(W_COLS // MXU_W):
        row0 = j * MXU_W if j <= dt_tile else j * MXU_W - (MXU_W - M_HEADS)
        jobs.append((wt_hbm.at[pl.ds(row0, MXU_W), :], functools.partial(w_in_tile, j=j)))
    for src, dst in zip(square_hbm, square_refs):
        for r0 in range(0, D_MODEL, MXU_W):
            jobs.append((src.at[pl.ds(r0, MXU_W), :], functools.partial(square_rows, dst=dst, r0=r0)))
    for c0 in range(0, 3 * D_MODEL, D_MODEL):
        for r0 in range(0, D_MODEL, MXU_W):
            jobs.append((wada_hbm.at[pl.ds(r0, MXU_W), pl.ds(c0, D_MODEL)], functools.partial(adaln_slab, r0=r0, c0=c0)))

    def slab_copy(i):
        slot = i % W_SLABS_IN_FLIGHT
        return pltpu.make_async_copy(jobs[i][0], stage.at[slot], sems.at[slot])

    for i in range(W_SLABS_IN_FLIGHT - 1):
        slab_copy(i).start()
    for i in range(len(jobs)):
        if i + W_SLABS_IN_FLIGHT - 1 < len(jobs):
            slab_copy(i + W_SLABS_IN_FLIGHT - 1).start()
        slab_copy(i).wait()
        jobs[i][1](stage[i % W_SLABS_IN_FLIGHT])


def _fill_per_head(dtb_smem, alog_smem, dskip_smem, heads_ref, hcol_ref):
    lane = lax.broadcasted_iota(jnp.int32, (1, M_INNER), 1)
    sub = lax.broadcasted_iota(jnp.int32, (M_HEADS, LANES), 0)
    dskip = jnp.zeros((1, M_INNER), f32)
    dtb = alog = jnp.zeros((M_HEADS, LANES), f32)
    for h in range(M_HEADS):
        dskip = jnp.where(lane // M_HEADDIM == h, dskip_smem[h], dskip)
        dtb = jnp.where(sub == h, dtb_smem[h], dtb)
        alog = jnp.where(sub == h, alog_smem[h], alog)
    heads_ref[0:1, :] = dskip
    hcol_ref[0] = dtb
    hcol_ref[1] = -jnp.exp(alog) * LOG2_E


def _block_kernel(x_ref, c_ref, wada_hbm, bada_ref, normw_ref, wt_hbm, convw_ref, convb_ref,
                  dtb_smem, alog_smem, dskip_smem, mnw_ref, wpm_hbm, wpr_hbm, wout_hbm, fnw_ref,
                  cos_ref, sin_ref, dmat_ref, qdec_ref, kdec_ref, rdec_ref, tril_ref,
                  o_ref,
                  xbc_buf, s_state, r_state, y_buf, o_buf, z_buf, gr_buf, gam_buf, gar_buf,
                  q_buf, k_buf, v_buf, heads_ref, hcol_ref, mod_ref, wdt_ref, win_ref, wpm_ref, wpr_ref, wout_ref, w_stage, w_sems):
    @pl.when(jnp.logical_and(pl.program_id(0) == 0, pl.program_id(1) == 0))
    def _first_step():
        _load_weights(wt_hbm, (wpm_hbm, wpr_hbm, wout_hbm), wada_hbm, c_ref, bada_ref,
                      win_ref, wdt_ref, (wpm_ref, wpr_ref, wout_ref), mod_ref, w_stage, w_sems)
        _fill_per_head(dtb_smem, alog_smem, dskip_smem, heads_ref, hcol_ref)

    @pl.when(pl.program_id(1) == 0)
    def _start_of_sequence():
        xbc_buf[0:SUBLANES, :] = jnp.zeros((SUBLANES, CONV_DIM), f32)
        s_state[...] = jnp.zeros_like(s_state)
        r_state[...] = jnp.zeros_like(r_state)

    mod = mod_ref[pl.ds(pl.program_id(0), 1), :]
    shift, scale, gate = (mod[:, i * D_MODEL:(i + 1) * D_MODEL] for i in range(3))
    g1 = normw_ref[...] * (1.0 + scale)
    slabs = [slice(r0, r0 + EDGE_ROWS) for r0 in range(0, TOK, EDGE_ROWS)]
    hb_slabs = []
    for rows in slabs:
        x = x_ref[rows, :]
        rs = lax.rsqrt(jnp.mean(x * x, axis=-1, keepdims=True) + EPS)
        hb_slabs.append((x * rs * g1 + shift).astype(bf16))
    hb = jnp.concatenate(hb_slabs, axis=0)

    def proj(off, n):
        return _dot(hb, win_ref[:, off:off + n])

    def proj_tiles(dst, off):
        def tile(c0):
            dst[:, c0:c0 + MXU_W] = proj(off + c0, MXU_W).astype(dst.dtype)
        return [functools.partial(tile, c0) for c0 in range(0, dst.shape[1], MXU_W)]

    def emit(queue, n=1):
        for _ in range(min(n, len(queue))):
            queue.pop(0)()

    qkv_fill = proj_tiles(q_buf, OFF_Q) + proj_tiles(k_buf, OFF_K) + proj_tiles(v_buf, OFF_V)
    gate_fill = (proj_tiles(z_buf, OFF_Z) + proj_tiles(gr_buf, OFF_GR)
                 + proj_tiles(gam_buf, OFF_GAM) + proj_tiles(gar_buf, OFF_GAR))

    xbc_buf[SUBLANES:SUBLANES + TOK, :] = jnp.concatenate(
        [_dot(h, win_ref[:, OFF_XBC:OFF_XBC + CONV_DIM]) for h in hb_slabs], axis=0)
    per_token = lambda col: jnp.concatenate([col] * (TOK // LANES), axis=1)
    dt_t = _softplus(lax.dot_general(wdt_ref[...], hb, NT_DIMS, preferred_element_type=f32)
                     + per_token(hcol_ref[0]))
    a_t = jnp.concatenate(_split3(dt_t * per_token(hcol_ref[1])), axis=0)
    acs_parts = jnp.concatenate([lax.dot_general(a_t[:, r0:r0 + MXU_W], tril_ref[...], NT_DIMS,
                                                 preferred_element_type=f32) for r0 in range(0, TOK, MXU_W)], axis=1)
    acs_t_all = acs_parts[:M_HEADS] + acs_parts[M_HEADS:2 * M_HEADS] + acs_parts[2 * M_HEADS:]
    emit(gate_fill, 2)

    conv = convb_ref[...] + xbc_buf[SUBLANES:SUBLANES + TOK, :] * convw_ref[M_CONV - 1:M_CONV, :]
    for kk in range(M_CONV - 1):
        r0 = SUBLANES - (M_CONV - 1) + kk
        conv = conv + xbc_buf[r0:r0 + TOK, :] * convw_ref[kk:kk + 1, :]
    xbc_buf[0:SUBLANES, :] = xbc_buf[TOK:TOK + SUBLANES, :]
    xa = _silu(conv)
    xm = xa[:, :M_INNER]
    bmat = xa[:, M_INNER:M_INNER + LANES]
    cmat = xa[:, M_INNER + LANES:CONV_DIM]
    xm_b = xm.astype(bf16)

    emit(gate_fill, 2)

    ii = lax.broadcasted_iota(jnp.int32, (CHUNK, CHUNK), 0)
    jj = lax.broadcasted_iota(jnp.int32, (CHUNK, CHUNK), 1)
    causal = ii >= jj
    left_head = jj < M_HEADDIM

    def blockdiag(slab):
        zero = jnp.zeros_like(slab)
        return jnp.concatenate([jnp.where(left_head, slab, zero), jnp.where(left_head, zero, slab)], axis=0)

    def ssd_chunk(c):
        r0 = c * CHUNK
        acs_t = acs_t_all[:, r0:r0 + CHUNK]
        acs_c = jnp.concatenate([acs_t, jnp.zeros((CHUNK - M_HEADS, CHUNK), f32)], axis=0).T
        dt_c = dt_t[:, r0:r0 + CHUNK]
        w_t = jnp.exp2(acs_t[:, CHUNK - 1:CHUNK] - acs_t) * dt_c
        src_t = acs_t - jnp.log2(dt_c)
        b_t = bmat[r0:r0 + CHUNK, :].T
        c_c = cmat[r0:r0 + CHUNK, :].astype(bf16)
        cb2 = lax.dot_general(c_c, blockdiag(bmat[r0:r0 + CHUNK, :].astype(bf16)), NT_DIMS,
                              preferred_element_type=f32)
        s_prev = [s_state[g] for g in range(M_GROUPS)]
        y_off = [_dot(c_c[:, g * M_STATE:(g + 1) * M_STATE], s_prev[g].astype(bf16))
                 for g in range(M_GROUPS)]
        return acs_c, src_t, w_t, b_t, [cb2[:, :CHUNK], cb2[:, CHUNK:]], s_prev, y_off

    def ssd_pair(c, pair, chunk):
        acs_c, src_t, w_t, b_t, cb, s_prev, y_off = chunk
        r0 = c * CHUNK
        g = pair // (HEADS_PER_GROUP // 2)
        gslab = slice((pair % (HEADS_PER_GROUP // 2)) * LANES, (pair % (HEADS_PER_GROUP // 2) + 1) * LANES)
        pcols = slice(pair * LANES, (pair + 1) * LANES)
        heads = (2 * pair, 2 * pair + 1)
        cols = [jnp.broadcast_to(acs_c[:, h:h + 1], (CHUNK, CHUNK)) for h in heads]
        w_rows = [(cb[g] * jnp.exp2(jnp.where(causal, col - src_t[h:h + 1, :], -jnp.inf))).astype(bf16)
                  for h, col in zip(heads, cols)]
        bt_g = b_t[g * M_STATE:(g + 1) * M_STATE, :]
        st_rows = [(bt_g * w_t[h:h + 1, :]).astype(bf16) for h in heads]
        lhs = jnp.concatenate([jnp.concatenate(w_rows, axis=1), jnp.concatenate(st_rows, axis=1)], axis=0)
        res = _dot(lhs, blockdiag(xm_b[r0:r0 + CHUNK, pcols]))
        decay = jnp.exp2(jnp.where(left_head, cols[0], cols[1]))
        y_buf[r0:r0 + CHUNK, pcols] = res[:CHUNK] + y_off[g][:, gslab] * decay
        s_state[g, :, gslab] = s_prev[g][:, gslab] * decay[CHUNK - 1:CHUNK, :] + res[CHUNK:]

    ret = {}

    def retention_prep():
        cos = jnp.concatenate([cos_ref[...]] * (R_QK // LANES), axis=1)
        sin = jnp.concatenate([sin_ref[...]] * (R_QK // LANES), axis=1)
        lane_q = lax.broadcasted_iota(jnp.int32, (TOK, R_QK), 1)
        first_half = (lane_q % R_HEAD_QK) < (R_HEAD_QK // 2)

        def rotary(t):
            swapped = jnp.where(first_half, pltpu.roll(t, R_QK - R_HEAD_QK // 2, 1),
                                pltpu.roll(t, R_HEAD_QK // 2, 1))
            return t * cos + swapped * sin

        q = rotary(q_buf[...])
        k = rotary(k_buf[...]) * (R_HEAD_QK ** -0.5)
        qdec = jnp.concatenate([qdec_ref[...]] * N_CHUNKS, axis=0)
        kdec = jnp.concatenate([kdec_ref[...]] * N_CHUNKS, axis=0)
        ret.update(q_b=q.astype(bf16), k_b=k.astype(bf16), qw_b=(q * qdec).astype(bf16), kw=k * kdec)

    s_b = {}

    def score_pair(c, pair):
        r0 = c * CHUNK
        pcols = slice(pair * LANES, (pair + 1) * LANES)
        s_pair = (lax.dot_general(ret["q_b"][r0:r0 + CHUNK, pcols], blockdiag(ret["k_b"][r0:r0 + CHUNK, pcols]),
                                  NT_DIMS, preferred_element_type=f32) * dmat_ref[pair]).astype(bf16)
        s_b[c, 2 * pair], s_b[c, 2 * pair + 1] = s_pair[:, :CHUNK], s_pair[:, CHUNK:]

    def retention_head(c, h, kw_t):
        r0 = c * CHUNK
        qk_cols = slice(h * R_HEAD_QK, (h + 1) * R_HEAD_QK)
        v_cols = slice(h * R_HEAD_V, (h + 1) * R_HEAD_V)
        r_prev = r_state[h]
        lhs = jnp.concatenate(
            [jnp.concatenate([s_b[c, h], ret["qw_b"][r0:r0 + CHUNK, qk_cols]], axis=1),
             jnp.concatenate([kw_t[qk_cols, :], jnp.zeros((R_HEAD_QK, R_HEAD_QK), bf16)], axis=1)], axis=0)
        res = _dot(lhs, jnp.concatenate([v_buf[r0:r0 + CHUNK, v_cols], r_prev.astype(bf16)], axis=0))
        o_h = res[:CHUNK]
        r_state[h] = r_prev * rdec_ref[h] + res[CHUNK:]
        o_buf[r0:r0 + CHUNK, v_cols] = o_h * lax.rsqrt(jnp.mean(o_h * o_h, axis=-1, keepdims=True) + EPS)

    def ssd_output():
        y = y_buf[...] + heads_ref[0:1, :] * xm
        yz = y * _silu(z_buf[...])
        y_m = []
        for g in range(M_GROUPS):
            yz_g = yz[:, g * GROUP_W:(g + 1) * GROUP_W]
            y_m.append(yz_g * lax.rsqrt(jnp.mean(yz_g * yz_g, axis=-1, keepdims=True) + EPS))
        return (jnp.concatenate(y_m, axis=1) * mnw_ref[...]).astype(bf16)

    half = R_HEADS // 2
    n_mid, n_gate = (N_CHUNKS - 1) * R_HEADS, len(gate_fill) - 2
    u_m_tiles = []
    for r in range(N_CHUNKS + 1):
        if r < N_CHUNKS:
            chunk = ssd_chunk(r)
        if r >= 1:
            kw_t = ret["kw"][(r - 1) * CHUNK:r * CHUNK, :].T.astype(bf16)
        if r == N_CHUNKS:
            y_m = ssd_output()
        for i in range(R_HEADS):
            if r < N_CHUNKS:
                ssd_pair(r, i, chunk)
                if r == 0 and i == half:
                    retention_prep()
                if i >= half:
                    score_pair(r, i - half)
            if r >= 1:
                retention_head(r - 1, i, kw_t)
            if r == 0 and i < R_HEADS - 2:
                emit(qkv_fill, 2 if i < 2 else 1)
            elif r == 0:
                emit(gate_fill)
            elif r < N_CHUNKS:
                it = (r - 1) * R_HEADS + i
                emit(gate_fill, (it + 1) * n_gate // n_mid - it * n_gate // n_mid)
            elif i < D_MODEL // MXU_W:
                u_m_tiles.append(_dot(y_m, wpm_ref[:, i * MXU_W:(i + 1) * MXU_W]))
    assert not qkv_fill and not gate_fill and M_HEADS // 2 == R_HEADS
    u_m = jnp.concatenate(u_m_tiles, axis=1)

    u_r = [_dot((_silu(gr_buf[rows, :]) * o_buf[rows, :]).astype(bf16), wpr_ref[...]) for rows in slabs]
    for rows, u_r_slab in zip(slabs, u_r):
        merged = _sigmoid(gam_buf[rows, :]) * u_m[rows, :] + _sigmoid(gar_buf[rows, :]) * u_r_slab
        xo = x_ref[rows, :] + gate * _dot(merged.astype(bf16), wout_ref[...])
        o_ref[rows, :] = xo * lax.rsqrt(jnp.mean(xo * xo, axis=-1, keepdims=True) + EPS) * fnw_ref[...]


def _resident(shape):
    zeros = (0,) * len(shape)
    return pl.BlockSpec(shape, lambda b, s: zeros, pipeline_mode=pl.Buffered(1))


def _tables(seq_len):
    half = R_HEAD_QK // 2
    pos = np.arange(seq_len, dtype=np.float64)
    inv = ROPE_BASE ** (-np.arange(half, dtype=np.float64) / half)
    ang = pos[:, None] * inv[None, :]
    cos_t = np.tile(np.cos(ang), (1, LANES // half))
    sin_t = np.tile(np.concatenate([-np.sin(ang), np.sin(ang)], axis=1), (1, LANES // R_HEAD_QK))
    log_g = np.log1p(-np.exp2(-5.0 - np.arange(R_HEADS, dtype=np.float64)))
    idx = np.arange(CHUNK, dtype=np.float64)
    rel = idx[:, None] - idx[None, :]
    dmat = np.where(rel[None] >= 0, np.exp(np.minimum(rel[None], CHUNK) * log_g[:, None, None]), 0.0)
    dmat = np.concatenate([dmat[0::2], dmat[1::2]], axis=2)
    qdec = np.repeat(np.exp((idx + 1)[:, None] * log_g[None, :]), R_HEAD_QK, axis=1)
    kdec = np.repeat(np.exp((CHUNK - 1 - idx)[:, None] * log_g[None, :]), R_HEAD_QK, axis=1)
    rdec = np.broadcast_to(np.exp(CHUNK * log_g)[:, None, None], (R_HEADS, 1, R_HEAD_V))
    t = np.arange(MXU_W)
    tril = (t[:, None] >= t[None, :]) & (t[:, None] // CHUNK == t[None, :] // CHUNK)
    as_f32 = lambda a: jnp.asarray(np.ascontiguousarray(a, dtype=np.float32))
    as_bf16 = lambda a: jnp.asarray(np.ascontiguousarray(a, dtype=np.float32), dtype=bf16)
    return (as_f32(cos_t), as_f32(sin_t), as_f32(dmat), as_f32(qdec), as_f32(kdec), as_f32(rdec),
            as_bf16(tril))


def _layer(x, c, w_ada, b_ada, norm_w, w_in_t, conv_w, conv_b, dt_bias, a_log, d_skip,
           m_norm_w, w_proj_m, w_proj_r, w_out, out_norm_w):
    bsz, seq_len, d = x.shape
    assert d == D_MODEL and seq_len % TOK == 0

    assert w_in_t.shape == (W_COLS - MXU_W + M_HEADS, d)
    cos_t, sin_t, dmat, qdec, kdec, rdec, tril = _tables(seq_len)

    tok_block = lambda w: pl.BlockSpec((TOK, w), lambda b, s: (s, 0))
    return pl.pallas_call(
        _block_kernel,
        grid=(bsz, seq_len // TOK),
        in_specs=[
            pl.BlockSpec((None, TOK, d), lambda b, s: (b, s, 0)),
            _resident((bsz, d)),
            pl.BlockSpec(memory_space=pl.ANY),
            _resident((1, 3 * d)),
            _resident((1, d)),
            pl.BlockSpec(memory_space=pl.ANY),
            _resident((M_CONV, CONV_DIM)), _resident((1, CONV_DIM)),
            pl.BlockSpec(memory_space=pltpu.SMEM), pl.BlockSpec(memory_space=pltpu.SMEM),
            pl.BlockSpec(memory_space=pltpu.SMEM),
            _resident((1, M_INNER)),
            pl.BlockSpec(memory_space=pl.ANY), pl.BlockSpec(memory_space=pl.ANY),
            pl.BlockSpec(memory_space=pl.ANY),
            _resident((1, d)),
            tok_block(LANES), tok_block(LANES),
            _resident((R_HEADS // 2, CHUNK, 2 * CHUNK)),
            _resident((CHUNK, R_QK)), _resident((CHUNK, R_QK)),
            _resident((R_HEADS, 1, R_HEAD_V)),
            _resident((MXU_W, MXU_W)),
        ],
        out_specs=pl.BlockSpec((None, TOK, d), lambda b, s: (b, s, 0)),
        out_shape=jax.ShapeDtypeStruct((bsz, seq_len, d), x.dtype),
        scratch_shapes=[
            pltpu.VMEM((TOK + 2 * SUBLANES, CONV_DIM), f32),
            pltpu.VMEM((M_GROUPS, M_STATE, GROUP_W), f32),
            pltpu.VMEM((R_HEADS, R_HEAD_QK, R_HEAD_V), f32),
            pltpu.VMEM((TOK, M_INNER), f32),
            pltpu.VMEM((TOK, R_V), f32),
            pltpu.VMEM((TOK, M_INNER), f32),
            pltpu.VMEM((TOK, R_V), f32),
            pltpu.VMEM((TOK, D_MODEL), f32),
            pltpu.VMEM((TOK, D_MODEL), f32),
            pltpu.VMEM((TOK, R_QK), f32),
            pltpu.VMEM((TOK, R_QK), f32),
            pltpu.VMEM((TOK, R_V), bf16),
            pltpu.VMEM((SUBLANES, M_INNER), f32),
            pltpu.VMEM((2, M_HEADS, LANES), f32),
            pltpu.VMEM((bsz, 3 * d), f32),
            pltpu.VMEM((M_HEADS, d), bf16),
            pltpu.VMEM((d, W_COLS), bf16),
            pltpu.VMEM((M_INNER, d), bf16), pltpu.VMEM((R_V, d), bf16), pltpu.VMEM((d, d), bf16),
            pltpu.VMEM((W_SLABS_IN_FLIGHT, MXU_W, d), f32),
            pltpu.SemaphoreType.DMA((W_SLABS_IN_FLIGHT,)),
        ],
        compiler_params=pltpu.CompilerParams(
            dimension_semantics=("arbitrary", "arbitrary"),
            vmem_limit_bytes=VMEM_LIMIT_BYTES),
        name="hybrid_block",
    )(x, c, w_ada, b_ada.reshape(1, 3 * d), norm_w.reshape(1, d), w_in_t, conv_w, conv_b.reshape(1, CONV_DIM),
      dt_bias, a_log, d_skip, m_norm_w.reshape(1, M_INNER),
      w_proj_m, w_proj_r, w_out, out_norm_w.reshape(1, d),
      cos_t, sin_t, dmat, qdec, kdec, rdec, tril)


@jax.jit
def kernel(x, c, w_ada, b_ada, norm_w, w_in, conv_w, conv_b, dt_bias, a_log, d_skip, m_norm_w,
           w_proj_m, w_proj_r, w_out, final_norm_w):
    assert w_ada.shape[0] == 1
    return _layer(x, c, w_ada[0], b_ada[0], norm_w[0], jnp.swapaxes(w_in, 1, 2)[0], conv_w[0], conv_b[0], dt_bias[0],
                  a_log[0], d_skip[0], m_norm_w[0], w_proj_m[0], w_proj_r[0], w_out[0], final_norm_w)
```

```python
import functools

import jax
import jax.numpy as jnp
import numpy as np
from jax import lax
from jax.experimental import pallas as pl
from jax.experimental.pallas import tpu as pltpu

f32 = jnp.float32
bf16 = jnp.bfloat16

D_MODEL = 1024
M_HEADDIM = 64
M_HEADS = 16
M_GROUPS = 2
M_STATE = 64
M_CONV = 4
M_INNER = D_MODEL
GROUP_W = M_INNER // M_GROUPS
HEADS_PER_GROUP = M_HEADS // M_GROUPS
CONV_DIM = M_INNER + 2 * M_GROUPS * M_STATE
R_HEADS = 8
R_QK = 512
R_HEAD_QK = 64
R_V = 1024
R_HEAD_V = 128
CHUNK = 128
ROPE_BASE = 10000.0
EPS = 1e-6
LOG2_E = 1.4426950408889634

LANES = 128
SUBLANES = 8
MXU_W = 256
TOK = 512
N_CHUNKS = TOK // CHUNK
EDGE_ROWS = 256
W_SLABS_IN_FLIGHT = 4
VMEM_LIMIT_BYTES = 58 * 1024 * 1024

OFF_Z = 0
OFF_XBC = OFF_Z + M_INNER
OFF_DT = OFF_XBC + CONV_DIM
OFF_Q = OFF_DT + MXU_W
OFF_K = OFF_Q + R_QK
OFF_V = OFF_K + R_QK
OFF_GR = OFF_V + R_V
OFF_GAM = OFF_GR + R_V
OFF_GAR = OFF_GAM + D_MODEL
W_COLS = OFF_GAR + D_MODEL

NT_DIMS = (((1,), (1,)), ((), ()))


def _sigmoid(v):
    return 1.0 / (1.0 + jnp.exp(-v))


def _silu(v):
    return v * _sigmoid(v)


def _softplus(v):
    return jnp.maximum(v, 0.0) + jnp.log1p(jnp.exp(-jnp.abs(v)))


def _dot(a, b):
    return jnp.dot(a, b, preferred_element_type=f32)


def _split2(v):
    hi = v.astype(bf16)
    return hi, (v - hi.astype(f32)).astype(bf16)


def _split3(v):
    hi = v.astype(bf16)
    r1 = v - hi.astype(f32)
    mid = r1.astype(bf16)
    lo = (r1 - mid.astype(f32)).astype(bf16)
    return hi, mid, lo


def _load_weights(wt_hbm, square_hbm, wada_hbm, c_ref, bada_ref, win_ref, square_refs, mod_ref, stage, sems):
    dt_tile = OFF_DT // MXU_W
    jobs = []
    mod_ref[...] = jnp.broadcast_to(bada_ref[...], mod_ref.shape)

    def adaln_slab(w, r0, c0):
        c_hi, c_lo = _split2(c_ref[:, r0:r0 + MXU_W])
        w_hi, w_lo = _split2(w)
        n = c_hi.shape[0]
        with_w_hi = _dot(jnp.concatenate([c_hi, c_lo], axis=0), w_hi)
        mod_ref[:, c0:c0 + D_MODEL] += with_w_hi[:n] + with_w_hi[n:] + _dot(c_hi, w_lo)

    def w_in_tile(w_t, j):
        w = w_t.T
        if j == dt_tile:
            w = jnp.where(lax.broadcasted_iota(jnp.int32, w.shape, 1) < M_HEADS, w, 0.0)
        win_ref[:, j * MXU_W:(j + 1) * MXU_W] = w.astype(bf16)

    def square_rows(w, dst, r0):
        dst[r0:r0 + MXU_W, :] = w.astype(bf16)

    for j in range(W_COLS // MXU_W):
        row0 = j * MXU_W if j <= dt_tile else j * MXU_W - (MXU_W - M_HEADS)
        jobs.append((wt_hbm.at[pl.ds(row0, MXU_W), :], functools.partial(w_in_tile, j=j)))
    for src, dst in zip(square_hbm, square_refs):
        for r0 in range(0, D_MODEL, MXU_W):
            jobs.append((src.at[pl.ds(r0, MXU_W), :], functools.partial(square_rows, dst=dst, r0=r0)))
    for c0 in range(0, 3 * D_MODEL, D_MODEL):
        for r0 in range(0, D_MODEL, MXU_W):
            jobs.append((wada_hbm.at[pl.ds(r0, MXU_W), pl.ds(c0, D_MODEL)], functools.partial(adaln_slab, r0=r0, c0=c0)))

    def slab_copy(i):
        slot = i % W_SLABS_IN_FLIGHT
        return pltpu.make_async_copy(jobs[i][0], stage.at[slot], sems.at[slot])

    for i in range(W_SLABS_IN_FLIGHT - 1):
        slab_copy(i).start()
    for i in range(len(jobs)):
        if i + W_SLABS_IN_FLIGHT - 1 < len(jobs):
            slab_copy(i + W_SLABS_IN_FLIGHT - 1).start()
        slab_copy(i).wait()
        jobs[i][1](stage[i % W_SLABS_IN_FLIGHT])


def _fill_per_head(dtb_smem, alog_smem, dskip_smem, heads_ref):
    lane = lax.broadcasted_iota(jnp.int32, (1, M_INNER), 1)
    dtb = alog = dskip = jnp.zeros((1, M_INNER), f32)
    for h in range(M_HEADS):
        dtb = jnp.where(lane == h, dtb_smem[h], dtb)
        alog = jnp.where(lane == h, alog_smem[h], alog)
        dskip = jnp.where(lane // M_HEADDIM == h, dskip_smem[h], dskip)
    heads_ref[0:1, :], heads_ref[1:2, :], heads_ref[2:3, :] = dtb, alog, dskip


def _block_kernel(x_ref, c_ref, wada_hbm, bada_ref, normw_ref, wt_hbm, convw_ref, convb_ref,
                  dtb_smem, alog_smem, dskip_smem, mnw_ref, wpm_hbm, wpr_hbm, wout_hbm, fnw_ref,
                  cos_ref, sin_ref, dmat_ref, qdec_ref, kdec_ref, rdec_ref, tril_ref,
                  o_ref,
                  xbc_buf, s_state, r_state, y_buf, o_buf, z_buf, gr_buf, gam_buf, gar_buf,
                  q_buf, k_buf, v_buf, heads_ref, mod_ref, win_ref, wpm_ref, wpr_ref, wout_ref, w_stage, w_sems):
    @pl.when(jnp.logical_and(pl.program_id(0) == 0, pl.program_id(1) == 0))
    def _first_step():
        _load_weights(wt_hbm, (wpm_hbm, wpr_hbm, wout_hbm), wada_hbm, c_ref, bada_ref,
                      win_ref, (wpm_ref, wpr_ref, wout_ref), mod_ref, w_stage, w_sems)
        _fill_per_head(dtb_smem, alog_smem, dskip_smem, heads_ref)

    @pl.when(pl.program_id(1) == 0)
    def _start_of_sequence():
        xbc_buf[0:SUBLANES, :] = jnp.zeros((SUBLANES, CONV_DIM), f32)
        s_state[...] = jnp.zeros_like(s_state)
        r_state[...] = jnp.zeros_like(r_state)

    mod = mod_ref[pl.ds(pl.program_id(0), 1), :]
    shift, scale, gate = (mod[:, i * D_MODEL:(i + 1) * D_MODEL] for i in range(3))
    g1 = normw_ref[...] * (1.0 + scale)
    slabs = [slice(r0, r0 + EDGE_ROWS) for r0 in range(0, TOK, EDGE_ROWS)]
    hb_slabs = []
    for rows in slabs:
        x = x_ref[rows, :]
        rs = lax.rsqrt(jnp.mean(x * x, axis=-1, keepdims=True) + EPS)
        hb_slabs.append((x * rs * g1 + shift).astype(bf16))
    hb = jnp.concatenate(hb_slabs, axis=0)

    def proj(off, n):
        return _dot(hb, win_ref[:, off:off + n])

    def proj_tiles(dst, off):
        def tile(c0):
            dst[:, c0:c0 + MXU_W] = proj(off + c0, MXU_W).astype(dst.dtype)
        return [functools.partial(tile, c0) for c0 in range(0, dst.shape[1], MXU_W)]

    def emit(queue, n=1):
        for _ in range(min(n, len(queue))):
            queue.pop(0)()

    qkv_fill = proj_tiles(q_buf, OFF_Q) + proj_tiles(k_buf, OFF_K) + proj_tiles(v_buf, OFF_V)
    gate_fill = (proj_tiles(z_buf, OFF_Z) + proj_tiles(gr_buf, OFF_GR)
                 + proj_tiles(gam_buf, OFF_GAM) + proj_tiles(gar_buf, OFF_GAR))

    xbc_dt = jnp.concatenate([_dot(h, win_ref[:, OFF_XBC:OFF_XBC + CONV_DIM + MXU_W]) for h in hb_slabs],
                             axis=0)
    xbc_buf[SUBLANES:SUBLANES + TOK, :] = xbc_dt[:, :CONV_DIM]
    dt_raw = xbc_dt[:, CONV_DIM:CONV_DIM + LANES]
    emit(gate_fill, 2)

    def conv_chunk(c):
        base = SUBLANES + c * CHUNK
        conv = convb_ref[...] + xbc_buf[base:base + CHUNK, :] * convw_ref[M_CONV - 1:M_CONV, :]
        for kk in range(M_CONV - 1):
            r0 = base - (M_CONV - 1) + kk
            conv = conv + xbc_buf[r0:r0 + CHUNK, :] * convw_ref[kk:kk + 1, :]
        return _silu(conv)

    lane = lax.broadcasted_iota(jnp.int32, (1, LANES), 1)
    a_neg = jnp.where(lane < M_HEADS, -jnp.exp(heads_ref[1:2, :LANES]), 0.0)
    dt = _softplus(dt_raw + heads_ref[0:1, :LANES])
    a = dt * (a_neg * LOG2_E)
    a_split = jnp.concatenate(_split3(a), axis=1)
    a_parts = jnp.concatenate([_dot(tril_ref[...], a_split[r0:r0 + MXU_W, :]) for r0 in range(0, TOK, MXU_W)],
                              axis=0)
    a_cs = a_parts[:, :LANES] + a_parts[:, LANES:2 * LANES] + a_parts[:, 2 * LANES:]
    emit(gate_fill, 2)

    ii = lax.broadcasted_iota(jnp.int32, (CHUNK, CHUNK), 0)
    jj = lax.broadcasted_iota(jnp.int32, (CHUNK, CHUNK), 1)
    causal = ii >= jj
    left_head = jj < M_HEADDIM

    def blockdiag(slab):
        zero = jnp.zeros_like(slab)
        return jnp.concatenate([jnp.where(left_head, slab, zero), jnp.where(left_head, zero, slab)], axis=0)

    def ssd_chunk(c):
        r0 = c * CHUNK
        acs_c = a_cs[r0:r0 + CHUNK, :]
        acs_t = acs_c.T[:M_HEADS, :]
        dt_t = dt[r0:r0 + CHUNK, :].T[:M_HEADS, :]
        w_t = jnp.exp2(acs_t[:, CHUNK - 1:CHUNK] - acs_t) * dt_t
        src_t = acs_t - jnp.log2(dt_t)
        xa = conv_chunk(c)
        xm_c = xa[:, :M_INNER]
        b_c = xa[:, M_INNER:M_INNER + LANES]
        b_t = b_c.T
        c_c = xa[:, M_INNER + LANES:CONV_DIM].astype(bf16)
        cb2 = lax.dot_general(c_c, blockdiag(b_c.astype(bf16)), NT_DIMS,
                              preferred_element_type=f32)
        s_prev = [s_state[g] for g in range(M_GROUPS)]
        y_off = [_dot(c_c[:, g * M_STATE:(g + 1) * M_STATE], s_prev[g].astype(bf16))
                 for g in range(M_GROUPS)]
        return acs_c, src_t, w_t, b_t, [cb2[:, :CHUNK], cb2[:, CHUNK:]], s_prev, y_off, xm_c

    def ssd_pair(c, pair, chunk):
        acs_c, src_t, w_t, b_t, cb, s_prev, y_off, xm_c = chunk
        r0 = c * CHUNK
        g = pair // (HEADS_PER_GROUP // 2)
        gslab = slice((pair % (HEADS_PER_GROUP // 2)) * LANES, (pair % (HEADS_PER_GROUP // 2) + 1) * LANES)
        pcols = slice(pair * LANES, (pair + 1) * LANES)
        heads = (2 * pair, 2 * pair + 1)
        cols = [jnp.broadcast_to(acs_c[:, h:h + 1], (CHUNK, CHUNK)) for h in heads]
        w_rows = [(cb[g] * jnp.exp2(jnp.where(causal, col - src_t[h:h + 1, :], -jnp.inf))).astype(bf16)
                  for h, col in zip(heads, cols)]
        bt_g = b_t[g * M_STATE:(g + 1) * M_STATE, :]
        st_rows = [(bt_g * w_t[h:h + 1, :]).astype(bf16) for h in heads]
        lhs = jnp.concatenate([jnp.concatenate(w_rows, axis=1), jnp.concatenate(st_rows, axis=1)], axis=0)
        x_pair = xm_c[:, pcols]
        res = _dot(lhs, blockdiag(x_pair.astype(bf16)))
        decay = jnp.exp2(jnp.where(left_head, cols[0], cols[1]))
        y_buf[r0:r0 + CHUNK, pcols] = res[:CHUNK] + y_off[g][:, gslab] * decay + heads_ref[2:3, pcols] * x_pair
        s_state[g, :, gslab] = s_prev[g][:, gslab] * decay[CHUNK - 1:CHUNK, :] + res[CHUNK:]

    ret = {}

    def retention_prep():
        cos = jnp.concatenate([cos_ref[...]] * (R_QK // LANES), axis=1)
        sin = jnp.concatenate([sin_ref[...]] * (R_QK // LANES), axis=1)
        lane_q = lax.broadcasted_iota(jnp.int32, (TOK, R_QK), 1)
        first_half = (lane_q % R_HEAD_QK) < (R_HEAD_QK // 2)

        def rotary(t):
            swapped = jnp.where(first_half, pltpu.roll(t, R_QK - R_HEAD_QK // 2, 1),
                                pltpu.roll(t, R_HEAD_QK // 2, 1))
            return t * cos + swapped * sin

        q = rotary(q_buf[...])
        k = rotary(k_buf[...]) * (R_HEAD_QK ** -0.5)
        qdec = jnp.concatenate([qdec_ref[...]] * N_CHUNKS, axis=0)
        kdec = jnp.concatenate([kdec_ref[...]] * N_CHUNKS, axis=0)
        ret.update(q_b=q.astype(bf16), k_b=k.astype(bf16), qw_b=(q * qdec).astype(bf16), kw=k * kdec)

    s_b = {}

    def score_pair(c, pair):
        r0 = c * CHUNK
        pcols = slice(pair * LANES, (pair + 1) * LANES)
        s_pair = (lax.dot_general(ret["q_b"][r0:r0 + CHUNK, pcols], blockdiag(ret["k_b"][r0:r0 + CHUNK, pcols]),
                                  NT_DIMS, preferred_element_type=f32) * dmat_ref[pair]).astype(bf16)
        s_b[c, 2 * pair], s_b[c, 2 * pair + 1] = s_pair[:, :CHUNK], s_pair[:, CHUNK:]

    def retention_head(c, h, kw_t):
        r0 = c * CHUNK
        qk_cols = slice(h * R_HEAD_QK, (h + 1) * R_HEAD_QK)
        v_cols = slice(h * R_HEAD_V, (h + 1) * R_HEAD_V)
        r_prev = r_state[h]
        lhs = jnp.concatenate(
            [jnp.concatenate([s_b[c, h], ret["qw_b"][r0:r0 + CHUNK, qk_cols]], axis=1),
             jnp.concatenate([kw_t[qk_cols, :], jnp.zeros((R_HEAD_QK, R_HEAD_QK), bf16)], axis=1)], axis=0)
        res = _dot(lhs, jnp.concatenate([v_buf[r0:r0 + CHUNK, v_cols], r_prev.astype(bf16)], axis=0))
        o_h = res[:CHUNK]
        r_state[h] = r_prev * rdec_ref[h] + res[CHUNK:]
        o_buf[r0:r0 + CHUNK, v_cols] = o_h * lax.rsqrt(jnp.mean(o_h * o_h, axis=-1, keepdims=True) + EPS)

    def ssd_output():
        yz = y_buf[...] * _silu(z_buf[...])
        y_m = []
        for g in range(M_GROUPS):
            yz_g = yz[:, g * GROUP_W:(g + 1) * GROUP_W]
            y_m.append(yz_g * lax.rsqrt(jnp.mean(yz_g * yz_g, axis=-1, keepdims=True) + EPS))
        return (jnp.concatenate(y_m, axis=1) * mnw_ref[...]).astype(bf16)

    half = R_HEADS // 2
    n_mid, n_gate = (N_CHUNKS - 1) * R_HEADS, len(gate_fill) - 2
    u_m_tiles = []
    for r in range(N_CHUNKS + 1):
        if r < N_CHUNKS:
            chunk = ssd_chunk(r)
        if r >= 1:
            kw_t = ret["kw"][(r - 1) * CHUNK:r * CHUNK, :].T.astype(bf16)
        if r == N_CHUNKS:
            y_m = ssd_output()
        for i in range(R_HEADS):
            if r < N_CHUNKS:
                ssd_pair(r, i, chunk)
                if r == 0 and i == half:
                    retention_prep()
                if i >= half:
                    score_pair(r, i - half)
            if r >= 1:
                retention_head(r - 1, i, kw_t)
            if r == 0 and i < R_HEADS - 2:
                emit(qkv_fill, 2 if i < 2 else 1)
            elif r == 0:
                emit(gate_fill)
            elif r < N_CHUNKS:
                it = (r - 1) * R_HEADS + i
                emit(gate_fill, (it + 1) * n_gate // n_mid - it * n_gate // n_mid)
            elif i < D_MODEL // MXU_W:
                u_m_tiles.append(_dot(y_m, wpm_ref[:, i * MXU_W:(i + 1) * MXU_W]))
    assert not qkv_fill and not gate_fill and M_HEADS // 2 == R_HEADS
    u_m = jnp.concatenate(u_m_tiles, axis=1)
    xbc_buf[0:SUBLANES, :] = xbc_buf[TOK:TOK + SUBLANES, :]

    u_r = [_dot((_silu(gr_buf[rows, :]) * o_buf[rows, :]).astype(bf16), wpr_ref[...]) for rows in slabs]
    for rows, u_r_slab in zip(slabs, u_r):
        merged = _sigmoid(gam_buf[rows, :]) * u_m[rows, :] + _sigmoid(gar_buf[rows, :]) * u_r_slab
        xo = x_ref[rows, :] + gate * _dot(merged.astype(bf16), wout_ref[...])
        o_ref[rows, :] = xo * lax.rsqrt(jnp.mean(xo * xo, axis=-1, keepdims=True) + EPS) * fnw_ref[...]


def _resident(shape):
    zeros = (0,) * len(shape)
    return pl.BlockSpec(shape, lambda b, s: zeros, pipeline_mode=pl.Buffered(1))


def _tables(seq_len):
    half = R_HEAD_QK // 2
    pos = np.arange(seq_len, dtype=np.float64)
    inv = ROPE_BASE ** (-np.arange(half, dtype=np.float64) / half)
    ang = pos[:, None] * inv[None, :]
    cos_t = np.tile(np.cos(ang), (1, LANES // half))
    sin_t = np.tile(np.concatenate([-np.sin(ang), np.sin(ang)], axis=1), (1, LANES // R_HEAD_QK))
    log_g = np.log1p(-np.exp2(-5.0 - np.arange(R_HEADS, dtype=np.float64)))
    idx = np.arange(CHUNK, dtype=np.float64)
    rel = idx[:, None] - idx[None, :]
    dmat = np.where(rel[None] >= 0, np.exp(np.minimum(rel[None], CHUNK) * log_g[:, None, None]), 0.0)
    dmat = np.concatenate([dmat[0::2], dmat[1::2]], axis=2)
    qdec = np.repeat(np.exp((idx + 1)[:, None] * log_g[None, :]), R_HEAD_QK, axis=1)
    kdec = np.repeat(np.exp((CHUNK - 1 - idx)[:, None] * log_g[None, :]), R_HEAD_QK, axis=1)
    rdec = np.broadcast_to(np.exp(CHUNK * log_g)[:, None, None], (R_HEADS, 1, R_HEAD_V))
    t = np.arange(MXU_W)
    tril = (t[:, None] >= t[None, :]) & (t[:, None] // CHUNK == t[None, :] // CHUNK)
    as_f32 = lambda a: jnp.asarray(np.ascontiguousarray(a, dtype=np.float32))
    as_bf16 = lambda a: jnp.asarray(np.ascontiguousarray(a, dtype=np.float32), dtype=bf16)
    return (as_f32(cos_t), as_f32(sin_t), as_f32(dmat), as_f32(qdec), as_f32(kdec), as_f32(rdec),
            as_bf16(tril))


def _layer(x, c, w_ada, b_ada, norm_w, w_in_t, conv_w, conv_b, dt_bias, a_log, d_skip,
           m_norm_w, w_proj_m, w_proj_r, w_out, out_norm_w):
    bsz, seq_len, d = x.shape
    assert d == D_MODEL and seq_len % TOK == 0

    assert w_in_t.shape == (W_COLS - MXU_W + M_HEADS, d)
    cos_t, sin_t, dmat, qdec, kdec, rdec, tril = _tables(seq_len)

    tok_block = lambda w: pl.BlockSpec((TOK, w), lambda b, s: (s, 0))
    return pl.pallas_call(
        _block_kernel,
        grid=(bsz, seq_len // TOK),
        in_specs=[
            pl.BlockSpec((None, TOK, d), lambda b, s: (b, s, 0)),
            _resident((bsz, d)),
            pl.BlockSpec(memory_space=pl.ANY),
            _resident((1, 3 * d)),
            _resident((1, d)),
            pl.BlockSpec(memory_space=pl.ANY),
            _resident((M_CONV, CONV_DIM)), _resident((1, CONV_DIM)),
            pl.BlockSpec(memory_space=pltpu.SMEM), pl.BlockSpec(memory_space=pltpu.SMEM),
            pl.BlockSpec(memory_space=pltpu.SMEM),
            _resident((1, M_INNER)),
            pl.BlockSpec(memory_space=pl.ANY), pl.BlockSpec(memory_space=pl.ANY),
            pl.BlockSpec(memory_space=pl.ANY),
            _resident((1, d)),
            tok_block(LANES), tok_block(LANES),
            _resident((R_HEADS // 2, CHUNK, 2 * CHUNK)),
            _resident((CHUNK, R_QK)), _resident((CHUNK, R_QK)),
            _resident((R_HEADS, 1, R_HEAD_V)),
            _resident((MXU_W, MXU_W)),
        ],
        out_specs=pl.BlockSpec((None, TOK, d), lambda b, s: (b, s, 0)),
        out_shape=jax.ShapeDtypeStruct((bsz, seq_len, d), x.dtype),
        scratch_shapes=[
            pltpu.VMEM((TOK + 2 * SUBLANES, CONV_DIM), f32),
            pltpu.VMEM((M_GROUPS, M_STATE, GROUP_W), f32),
            pltpu.VMEM((R_HEADS, R_HEAD_QK, R_HEAD_V), f32),
            pltpu.VMEM((TOK, M_INNER), f32),
            pltpu.VMEM((TOK, R_V), f32),
            pltpu.VMEM((TOK, M_INNER), f32),
            pltpu.VMEM((TOK, R_V), f32),
            pltpu.VMEM((TOK, D_MODEL), f32),
            pltpu.VMEM((TOK, D_MODEL), f32),
            pltpu.VMEM((TOK, R_QK), f32),
            pltpu.VMEM((TOK, R_QK), f32),
            pltpu.VMEM((TOK, R_V), bf16),
            pltpu.VMEM((SUBLANES, M_INNER), f32),
            pltpu.VMEM((bsz, 3 * d), f32),
            pltpu.VMEM((d, W_COLS), bf16),
            pltpu.VMEM((M_INNER, d), bf16), pltpu.VMEM((R_V, d), bf16), pltpu.VMEM((d, d), bf16),
            pltpu.VMEM((W_SLABS_IN_FLIGHT, MXU_W, d), f32),
            pltpu.SemaphoreType.DMA((W_SLABS_IN_FLIGHT,)),
        ],
        compiler_params=pltpu.CompilerParams(
            dimension_semantics=("arbitrary", "arbitrary"),
            vmem_limit_bytes=VMEM_LIMIT_BYTES),
        name="hybrid_block",
    )(x, c, w_ada, b_ada.reshape(1, 3 * d), norm_w.reshape(1, d), w_in_t, conv_w, conv_b.reshape(1, CONV_DIM),
      dt_bias, a_log, d_skip, m_norm_w.reshape(1, M_INNER),
      w_proj_m, w_proj_r, w_out, out_norm_w.reshape(1, d),
      cos_t, sin_t, dmat, qdec, kdec, rdec, tril)


@jax.jit
def kernel(x, c, w_ada, b_ada, norm_w, w_in, conv_w, conv_b, dt_bias, a_log, d_skip, m_norm_w,
           w_proj_m, w_proj_r, w_out, final_norm_w):
    assert w_ada.shape[0] == 1
    return _layer(x, c, w_ada[0], b_ada[0], norm_w[0], jnp.swapaxes(w_in, 1, 2)[0], conv_w[0], conv_b[0], dt_bias[0],
                  a_log[0], d_skip[0], m_norm_w[0], w_proj_m[0], w_proj_r[0], w_out[0], final_norm_w)
```

```python
import functools

import jax
import jax.numpy as jnp
import numpy as np
from jax import lax
from jax.experimental import pallas as pl
from jax.experimental.pallas import tpu as pltpu

f32 = jnp.float32
bf16 = jnp.bfloat16

D_MODEL = 1024
M_HEADDIM = 64
M_HEADS = 16
M_GROUPS = 2
M_STATE = 64
M_CONV = 4
M_INNER = D_MODEL
GROUP_W = M_INNER // M_GROUPS
HEADS_PER_GROUP = M_HEADS // M_GROUPS
CONV_DIM = M_INNER + 2 * M_GROUPS * M_STATE
R_HEADS = 8
R_QK = 512
R_HEAD_QK = 64
R_V = 1024
R_HEAD_V = 128
CHUNK = 128
ROPE_BASE = 10000.0
EPS = 1e-6
LOG2_E = 1.4426950408889634

LANES = 128
SUBLANES = 8
MXU_W = 256
TOK = 512
N_CHUNKS = TOK // CHUNK
EDGE_ROWS = 256
W_SLABS_IN_FLIGHT = 4
VMEM_LIMIT_BYTES = 58 * 1024 * 1024

OFF_Z = 0
OFF_XBC = OFF_Z + M_INNER
OFF_DT = OFF_XBC + CONV_DIM
OFF_Q = OFF_DT + MXU_W
OFF_K = OFF_Q + R_QK
OFF_V = OFF_K + R_QK
OFF_GR = OFF_V + R_V
OFF_GAM = OFF_GR + R_V
OFF_GAR = OFF_GAM + D_MODEL
W_COLS = OFF_GAR + D_MODEL

NT_DIMS = (((1,), (1,)), ((), ()))


def _sigmoid(v):
    return 1.0 / (1.0 + jnp.exp(-v))


def _silu(v):
    return v * _sigmoid(v)


def _softplus(v):
    return jnp.maximum(v, 0.0) + jnp.log1p(jnp.exp(-jnp.abs(v)))


def _dot(a, b):
    return jnp.dot(a, b, preferred_element_type=f32)


def _split2(v):
    hi = v.astype(bf16)
    return hi, (v - hi.astype(f32)).astype(bf16)


def _split3(v):
    hi = v.astype(bf16)
    r1 = v - hi.astype(f32)
    mid = r1.astype(bf16)
    lo = (r1 - mid.astype(f32)).astype(bf16)
    return hi, mid, lo


def _load_weights(wt_hbm, square_hbm, wada_hbm, c_ref, bada_ref, win_ref, square_refs, mod_ref, stage, sems):
    dt_tile = OFF_DT // MXU_W
    jobs = []
    mod_ref[...] = jnp.broadcast_to(bada_ref[...], mod_ref.shape)

    def adaln_slab(w, r0, c0):
        c_hi, c_lo = _split2(c_ref[:, r0:r0 + MXU_W])
        w_hi, w_lo = _split2(w)
        n = c_hi.shape[0]
        with_w_hi = _dot(jnp.concatenate([c_hi, c_lo], axis=0), w_hi)
        mod_ref[:, c0:c0 + D_MODEL] += with_w_hi[:n] + with_w_hi[n:] + _dot(c_hi, w_lo)

    def w_in_tile(w_t, j):
        w = w_t.T
        if j == dt_tile:
            w = jnp.where(lax.broadcasted_iota(jnp.int32, w.shape, 1) < M_HEADS, w, 0.0)
        win_ref[:, j * MXU_W:(j + 1) * MXU_W] = w.astype(bf16)

    def square_rows(w, dst, r0):
        dst[r0:r0 + MXU_W, :] = w.astype(bf16)

    for j in range(W_COLS // MXU_W):
        row0 = j * MXU_W if j <= dt_tile else j * MXU_W - (MXU_W - M_HEADS)
        jobs.append((wt_hbm.at[pl.ds(row0, MXU_W), :], functools.partial(w_in_tile, j=j)))
    for src, dst in zip(square_hbm, square_refs):
        for r0 in range(0, D_MODEL, MXU_W):
            jobs.append((src.at[pl.ds(r0, MXU_W), :], functools.partial(square_rows, dst=dst, r0=r0)))
    for c0 in range(0, 3 * D_MODEL, D_MODEL):
        for r0 in range(0, D_MODEL, MXU_W):
            jobs.append((wada_hbm.at[pl.ds(r0, MXU_W), pl.ds(c0, D_MODEL)], functools.partial(adaln_slab, r0=r0, c0=c0)))

    def slab_copy(i):
        slot = i % W_SLABS_IN_FLIGHT
        return pltpu.make_async_copy(jobs[i][0], stage.at[slot], sems.at[slot])

    for i in range(W_SLABS_IN_FLIGHT - 1):
        slab_copy(i).start(priority=i % 2)
    for i in range(len(jobs)):
        if i + W_SLABS_IN_FLIGHT - 1 < len(jobs):
            slab_copy(i + W_SLABS_IN_FLIGHT - 1).start(priority=(i + W_SLABS_IN_FLIGHT - 1) % 2)
        slab_copy(i).wait()
        jobs[i][1](stage[i % W_SLABS_IN_FLIGHT])


def _fill_per_head(dtb_smem, alog_smem, dskip_smem, heads_ref):
    lane = lax.broadcasted_iota(jnp.int32, (1, M_INNER), 1)
    dtb = alog = dskip = jnp.zeros((1, M_INNER), f32)
    for h in range(M_HEADS):
        dtb = jnp.where(lane == h, dtb_smem[h], dtb)
        alog = jnp.where(lane == h, alog_smem[h], alog)
        dskip = jnp.where(lane // M_HEADDIM == h, dskip_smem[h], dskip)
    heads_ref[0:1, :], heads_ref[1:2, :], heads_ref[2:3, :] = dtb, alog, dskip


def _block_kernel(x_ref, c_ref, wada_hbm, bada_ref, normw_ref, wt_hbm, convw_ref, convb_ref,
                  dtb_smem, alog_smem, dskip_smem, mnw_ref, wpm_hbm, wpr_hbm, wout_hbm, fnw_ref,
                  cos_ref, sin_ref, dmat_ref, qdec_ref, kdec_ref, rdec_ref, tril_ref,
                  o_ref,
                  xbc_buf, s_state, r_state, y_buf, o_buf, z_buf, gr_buf, gam_buf, gar_buf,
                  q_buf, k_buf, v_buf, heads_ref, mod_ref, win_ref, wpm_ref, wpr_ref, wout_ref, w_stage, w_sems):
    @pl.when(jnp.logical_and(pl.program_id(0) == 0, pl.program_id(1) == 0))
    def _first_step():
        _load_weights(wt_hbm, (wpm_hbm, wpr_hbm, wout_hbm), wada_hbm, c_ref, bada_ref,
                      win_ref, (wpm_ref, wpr_ref, wout_ref), mod_ref, w_stage, w_sems)
        _fill_per_head(dtb_smem, alog_smem, dskip_smem, heads_ref)

    @pl.when(pl.program_id(1) == 0)
    def _start_of_sequence():
        xbc_buf[0:SUBLANES, :] = jnp.zeros((SUBLANES, CONV_DIM), f32)
        s_state[...] = jnp.zeros_like(s_state)
        r_state[...] = jnp.zeros_like(r_state)

    mod = mod_ref[pl.ds(pl.program_id(0), 1), :]
    shift, scale, gate = (mod[:, i * D_MODEL:(i + 1) * D_MODEL] for i in range(3))
    g1 = normw_ref[...] * (1.0 + scale)
    slabs = [slice(r0, r0 + EDGE_ROWS) for r0 in range(0, TOK, EDGE_ROWS)]
    hb_slabs = []
    for rows in slabs:
        x = x_ref[rows, :]
        rs = lax.rsqrt(jnp.mean(x * x, axis=-1, keepdims=True) + EPS)
        hb_slabs.append((x * rs * g1 + shift).astype(bf16))
    hb = jnp.concatenate(hb_slabs, axis=0)

    def proj(off, n):
        return _dot(hb, win_ref[:, off:off + n])

    def proj_tiles(dst, off):
        def tile(c0):
            dst[:, c0:c0 + MXU_W] = proj(off + c0, MXU_W).astype(dst.dtype)
        return [functools.partial(tile, c0) for c0 in range(0, dst.shape[1], MXU_W)]

    def emit(queue, n=1):
        for _ in range(min(n, len(queue))):
            queue.pop(0)()

    qkv_fill = proj_tiles(q_buf, OFF_Q) + proj_tiles(k_buf, OFF_K) + proj_tiles(v_buf, OFF_V)
    gate_fill = (proj_tiles(z_buf, OFF_Z) + proj_tiles(gr_buf, OFF_GR)
                 + proj_tiles(gam_buf, OFF_GAM) + proj_tiles(gar_buf, OFF_GAR))

    xbc_dt = jnp.concatenate([_dot(h, win_ref[:, OFF_XBC:OFF_XBC + CONV_DIM + MXU_W]) for h in hb_slabs],
                             axis=0)
    xbc_buf[SUBLANES:SUBLANES + TOK, :] = xbc_dt[:, :CONV_DIM]
    dt_raw = xbc_dt[:, CONV_DIM:CONV_DIM + LANES]
    emit(gate_fill, 2)

    conv = convb_ref[...] + xbc_buf[SUBLANES:SUBLANES + TOK, :] * convw_ref[M_CONV - 1:M_CONV, :]
    for kk in range(M_CONV - 1):
        r0 = SUBLANES - (M_CONV - 1) + kk
        conv = conv + xbc_buf[r0:r0 + TOK, :] * convw_ref[kk:kk + 1, :]
    xbc_buf[0:SUBLANES, :] = xbc_buf[TOK:TOK + SUBLANES, :]
    xa = _silu(conv)
    xm = xa[:, :M_INNER]
    bmat = xa[:, M_INNER:M_INNER + LANES]
    cmat = xa[:, M_INNER + LANES:CONV_DIM]
    xm_b = xm.astype(bf16)

    lane = lax.broadcasted_iota(jnp.int32, (1, LANES), 1)
    a_neg = jnp.where(lane < M_HEADS, -jnp.exp(heads_ref[1:2, :LANES]), 0.0)
    dt = _softplus(dt_raw + heads_ref[0:1, :LANES])
    a = dt * (a_neg * LOG2_E)
    a_split = jnp.concatenate(_split3(a), axis=1)
    a_parts = jnp.concatenate([_dot(tril_ref[...], a_split[r0:r0 + MXU_W, :]) for r0 in range(0, TOK, MXU_W)],
                              axis=0)
    a_cs = a_parts[:, :LANES] + a_parts[:, LANES:2 * LANES] + a_parts[:, 2 * LANES:]
    emit(gate_fill, 2)

    ii = lax.broadcasted_iota(jnp.int32, (CHUNK, CHUNK), 0)
    jj = lax.broadcasted_iota(jnp.int32, (CHUNK, CHUNK), 1)
    causal = ii >= jj
    left_head = jj < M_HEADDIM

    def blockdiag(slab):
        zero = jnp.zeros_like(slab)
        return jnp.concatenate([jnp.where(left_head, slab, zero), jnp.where(left_head, zero, slab)], axis=0)

    def ssd_chunk(c):
        r0 = c * CHUNK
        acs_c = a_cs[r0:r0 + CHUNK, :]
        acs_t = acs_c.T[:M_HEADS, :]
        dt_t = dt[r0:r0 + CHUNK, :].T[:M_HEADS, :]
        w_t = jnp.exp2(acs_t[:, CHUNK - 1:CHUNK] - acs_t) * dt_t
        src_t = acs_t - jnp.log2(dt_t)
        b_t = bmat[r0:r0 + CHUNK, :].T
        c_c = cmat[r0:r0 + CHUNK, :].astype(bf16)
        cb2 = lax.dot_general(c_c, blockdiag(bmat[r0:r0 + CHUNK, :].astype(bf16)), NT_DIMS,
                              preferred_element_type=f32)
        s_prev = [s_state[g] for g in range(M_GROUPS)]
        y_off = [_dot(c_c[:, g * M_STATE:(g + 1) * M_STATE], s_prev[g].astype(bf16))
                 for g in range(M_GROUPS)]
        return acs_c, src_t, w_t, b_t, [cb2[:, :CHUNK], cb2[:, CHUNK:]], s_prev, y_off

    def ssd_pair(c, pair, chunk):
        acs_c, src_t, w_t, b_t, cb, s_prev, y_off = chunk
        r0 = c * CHUNK
        g = pair // (HEADS_PER_GROUP // 2)
        gslab = slice((pair % (HEADS_PER_GROUP // 2)) * LANES, (pair % (HEADS_PER_GROUP // 2) + 1) * LANES)
        pcols = slice(pair * LANES, (pair + 1) * LANES)
        heads = (2 * pair, 2 * pair + 1)
        cols = [jnp.broadcast_to(acs_c[:, h:h + 1], (CHUNK, CHUNK)) for h in heads]
        w_rows = [(cb[g] * jnp.exp2(jnp.where(causal, col - src_t[h:h + 1, :], -jnp.inf))).astype(bf16)
                  for h, col in zip(heads, cols)]
        bt_g = b_t[g * M_STATE:(g + 1) * M_STATE, :]
        st_rows = [(bt_g * w_t[h:h + 1, :]).astype(bf16) for h in heads]
        lhs = jnp.concatenate([jnp.concatenate(w_rows, axis=1), jnp.concatenate(st_rows, axis=1)], axis=0)
        res = _dot(lhs, blockdiag(xm_b[r0:r0 + CHUNK, pcols]))
        decay = jnp.exp2(jnp.where(left_head, cols[0], cols[1]))
        y_buf[r0:r0 + CHUNK, pcols] = res[:CHUNK] + y_off[g][:, gslab] * decay
        s_state[g, :, gslab] = s_prev[g][:, gslab] * decay[CHUNK - 1:CHUNK, :] + res[CHUNK:]

    ret = {}

    def retention_prep():
        cos = jnp.concatenate([cos_ref[...]] * (R_QK // LANES), axis=1)
        sin = jnp.concatenate([sin_ref[...]] * (R_QK // LANES), axis=1)
        lane_q = lax.broadcasted_iota(jnp.int32, (TOK, R_QK), 1)
        first_half = (lane_q % R_HEAD_QK) < (R_HEAD_QK // 2)

        def rotary(t):
            swapped = jnp.where(first_half, pltpu.roll(t, R_QK - R_HEAD_QK // 2, 1),
                                pltpu.roll(t, R_HEAD_QK // 2, 1))
            return t * cos + swapped * sin

        q = rotary(q_buf[...])
        k = rotary(k_buf[...]) * (R_HEAD_QK ** -0.5)
        qdec = jnp.concatenate([qdec_ref[...]] * N_CHUNKS, axis=0)
        kdec = jnp.concatenate([kdec_ref[...]] * N_CHUNKS, axis=0)
        ret.update(q_b=q.astype(bf16), k_b=k.astype(bf16), qw_b=(q * qdec).astype(bf16), kw=k * kdec)

    s_b = {}

    def score_pair(c, pair):
        r0 = c * CHUNK
        pcols = slice(pair * LANES, (pair + 1) * LANES)
        s_pair = (lax.dot_general(ret["q_b"][r0:r0 + CHUNK, pcols], blockdiag(ret["k_b"][r0:r0 + CHUNK, pcols]),
                                  NT_DIMS, preferred_element_type=f32) * dmat_ref[pair]).astype(bf16)
        s_b[c, 2 * pair], s_b[c, 2 * pair + 1] = s_pair[:, :CHUNK], s_pair[:, CHUNK:]

    def retention_head(c, h, kw_t):
        r0 = c * CHUNK
        qk_cols = slice(h * R_HEAD_QK, (h + 1) * R_HEAD_QK)
        v_cols = slice(h * R_HEAD_V, (h + 1) * R_HEAD_V)
        r_prev = r_state[h]
        lhs = jnp.concatenate(
            [jnp.concatenate([s_b[c, h], ret["qw_b"][r0:r0 + CHUNK, qk_cols]], axis=1),
             jnp.concatenate([kw_t[qk_cols, :], jnp.zeros((R_HEAD_QK, R_HEAD_QK), bf16)], axis=1)], axis=0)
        res = _dot(lhs, jnp.concatenate([v_buf[r0:r0 + CHUNK, v_cols], r_prev.astype(bf16)], axis=0))
        o_h = res[:CHUNK]
        r_state[h] = r_prev * rdec_ref[h] + res[CHUNK:]
        o_buf[r0:r0 + CHUNK, v_cols] = o_h * lax.rsqrt(jnp.mean(o_h * o_h, axis=-1, keepdims=True) + EPS)

    def ssd_output():
        y = y_buf[...] + heads_ref[2:3, :] * xm
        yz = y * _silu(z_buf[...])
        y_m = []
        for g in range(M_GROUPS):
            yz_g = yz[:, g * GROUP_W:(g + 1) * GROUP_W]
            y_m.append(yz_g * lax.rsqrt(jnp.mean(yz_g * yz_g, axis=-1, keepdims=True) + EPS))
        return (jnp.concatenate(y_m, axis=1) * mnw_ref[...]).astype(bf16)

    half = R_HEADS // 2
    n_mid, n_gate = (N_CHUNKS - 1) * R_HEADS, len(gate_fill) - 2
    u_m_tiles = []
    for r in range(N_CHUNKS + 1):
        if r < N_CHUNKS:
            chunk = ssd_chunk(r)
        if r >= 1:
            kw_t = ret["kw"][(r - 1) * CHUNK:r * CHUNK, :].T.astype(bf16)
        if r == N_CHUNKS:
            y_m = ssd_output()
        for i in range(R_HEADS):
            if r < N_CHUNKS:
                ssd_pair(r, i, chunk)
                if r == 0 and i == half:
                    retention_prep()
                if i >= half:
                    score_pair(r, i - half)
            if r >= 1:
                retention_head(r - 1, i, kw_t)
            if r == 0 and i < R_HEADS - 2:
                emit(qkv_fill, 2 if i < 2 else 1)
            elif r == 0:
                emit(gate_fill)
            elif r < N_CHUNKS:
                it = (r - 1) * R_HEADS + i
                emit(gate_fill, (it + 1) * n_gate // n_mid - it * n_gate // n_mid)
            elif i < D_MODEL // MXU_W:
                u_m_tiles.append(_dot(y_m, wpm_ref[:, i * MXU_W:(i + 1) * MXU_W]))
    assert not qkv_fill and not gate_fill and M_HEADS // 2 == R_HEADS
    u_m = jnp.concatenate(u_m_tiles, axis=1)

    u_r = [_dot((_silu(gr_buf[rows, :]) * o_buf[rows, :]).astype(bf16), wpr_ref[...]) for rows in slabs]
    for rows, u_r_slab in zip(slabs, u_r):
        merged = _sigmoid(gam_buf[rows, :]) * u_m[rows, :] + _sigmoid(gar_buf[rows, :]) * u_r_slab
        xo = x_ref[rows, :] + gate * _dot(merged.astype(bf16), wout_ref[...])
        o_ref[rows, :] = xo * lax.rsqrt(jnp.mean(xo * xo, axis=-1, keepdims=True) + EPS) * fnw_ref[...]


def _resident(shape):
    zeros = (0,) * len(shape)
    return pl.BlockSpec(shape, lambda b, s: zeros, pipeline_mode=pl.Buffered(1))


def _tables(seq_len):
    half = R_HEAD_QK // 2
    pos = np.arange(seq_len, dtype=np.float64)
    inv = ROPE_BASE ** (-np.arange(half, dtype=np.float64) / half)
    ang = pos[:, None] * inv[None, :]
    cos_t = np.tile(np.cos(ang), (1, LANES // half))
    sin_t = np.tile(np.concatenate([-np.sin(ang), np.sin(ang)], axis=1), (1, LANES // R_HEAD_QK))
    log_g = np.log1p(-np.exp2(-5.0 - np.arange(R_HEADS, dtype=np.float64)))
    idx = np.arange(CHUNK, dtype=np.float64)
    rel = idx[:, None] - idx[None, :]
    dmat = np.where(rel[None] >= 0, np.exp(np.minimum(rel[None], CHUNK) * log_g[:, None, None]), 0.0)
    dmat = np.concatenate([dmat[0::2], dmat[1::2]], axis=2)
    qdec = np.repeat(np.exp((idx + 1)[:, None] * log_g[None, :]), R_HEAD_QK, axis=1)
    kdec = np.repeat(np.exp((CHUNK - 1 - idx)[:, None] * log_g[None, :]), R_HEAD_QK, axis=1)
    rdec = np.broadcast_to(np.exp(CHUNK * log_g)[:, None, None], (R_HEADS, 1, R_HEAD_V))
    t = np.arange(MXU_W)
    tril = (t[:, None] >= t[None, :]) & (t[:, None] // CHUNK == t[None, :] // CHUNK)
    as_f32 = lambda a: jnp.asarray(np.ascontiguousarray(a, dtype=np.float32))
    as_bf16 = lambda a: jnp.asarray(np.ascontiguousarray(a, dtype=np.float32), dtype=bf16)
    return (as_f32(cos_t), as_f32(sin_t), as_f32(dmat), as_f32(qdec), as_f32(kdec), as_f32(rdec),
            as_bf16(tril))


def _layer(x, c, w_ada, b_ada, norm_w, w_in_t, conv_w, conv_b, dt_bias, a_log, d_skip,
           m_norm_w, w_proj_m, w_proj_r, w_out, out_norm_w):
    bsz, seq_len, d = x.shape
    assert d == D_MODEL and seq_len % TOK == 0

    assert w_in_t.shape == (W_COLS - MXU_W + M_HEADS, d)
    cos_t, sin_t, dmat, qdec, kdec, rdec, tril = _tables(seq_len)

    tok_block = lambda w: pl.BlockSpec((TOK, w), lambda b, s: (s, 0))
    return pl.pallas_call(
        _block_kernel,
        grid=(bsz, seq_len // TOK),
        in_specs=[
            pl.BlockSpec((None, TOK, d), lambda b, s: (b, s, 0)),
            _resident((bsz, d)),
            pl.BlockSpec(memory_space=pl.ANY),
            _resident((1, 3 * d)),
            _resident((1, d)),
            pl.BlockSpec(memory_space=pl.ANY),
            _resident((M_CONV, CONV_DIM)), _resident((1, CONV_DIM)),
            pl.BlockSpec(memory_space=pltpu.SMEM), pl.BlockSpec(memory_space=pltpu.SMEM),
            pl.BlockSpec(memory_space=pltpu.SMEM),
            _resident((1, M_INNER)),
            pl.BlockSpec(memory_space=pl.ANY), pl.BlockSpec(memory_space=pl.ANY),
            pl.BlockSpec(memory_space=pl.ANY),
            _resident((1, d)),
            tok_block(LANES), tok_block(LANES),
            _resident((R_HEADS // 2, CHUNK, 2 * CHUNK)),
            _resident((CHUNK, R_QK)), _resident((CHUNK, R_QK)),
            _resident((R_HEADS, 1, R_HEAD_V)),
            _resident((MXU_W, MXU_W)),
        ],
        out_specs=pl.BlockSpec((None, TOK, d), lambda b, s: (b, s, 0)),
        out_shape=jax.ShapeDtypeStruct((bsz, seq_len, d), x.dtype),
        scratch_shapes=[
            pltpu.VMEM((TOK + 2 * SUBLANES, CONV_DIM), f32),
            pltpu.VMEM((M_GROUPS, M_STATE, GROUP_W), f32),
            pltpu.VMEM((R_HEADS, R_HEAD_QK, R_HEAD_V), f32),
            pltpu.VMEM((TOK, M_INNER), f32),
            pltpu.VMEM((TOK, R_V), f32),
            pltpu.VMEM((TOK, M_INNER), f32),
            pltpu.VMEM((TOK, R_V), f32),
            pltpu.VMEM((TOK, D_MODEL), f32),
            pltpu.VMEM((TOK, D_MODEL), f32),
            pltpu.VMEM((TOK, R_QK), f32),
            pltpu.VMEM((TOK, R_QK), f32),
            pltpu.VMEM((TOK, R_V), bf16),
            pltpu.VMEM((SUBLANES, M_INNER), f32),
            pltpu.VMEM((bsz, 3 * d), f32),
            pltpu.VMEM((d, W_COLS), bf16),
            pltpu.VMEM((M_INNER, d), bf16), pltpu.VMEM((R_V, d), bf16), pltpu.VMEM((d, d), bf16),
            pltpu.VMEM((W_SLABS_IN_FLIGHT, MXU_W, d), f32),
            pltpu.SemaphoreType.DMA((W_SLABS_IN_FLIGHT,)),
        ],
        compiler_params=pltpu.CompilerParams(
            dimension_semantics=("arbitrary", "arbitrary"),
            vmem_limit_bytes=VMEM_LIMIT_BYTES),
        name="hybrid_block",
    )(x, c, w_ada, b_ada.reshape(1, 3 * d), norm_w.reshape(1, d), w_in_t, conv_w, conv_b.reshape(1, CONV_DIM),
      dt_bias, a_log, d_skip, m_norm_w.reshape(1, M_INNER),
      w_proj_m, w_proj_r, w_out, out_norm_w.reshape(1, d),
      cos_t, sin_t, dmat, qdec, kdec, rdec, tril)


@jax.jit
def kernel(x, c, w_ada, b_ada, norm_w, w_in, conv_w, conv_b, dt_bias, a_log, d_skip, m_norm_w,
           w_proj_m, w_proj_r, w_out, final_norm_w):
    assert w_ada.shape[0] == 1
    return _layer(x, c, w_ada[0], b_ada[0], norm_w[0], jnp.swapaxes(w_in, 1, 2)[0], conv_w[0], conv_b[0], dt_bias[0],
                  a_log[0], d_skip[0], m_norm_w[0], w_proj_m[0], w_proj_r[0], w_out[0], final_norm_w)
```
